```python
import math
import jax, jax.numpy as jnp
from jax import lax
import numpy as np

D_MODEL = 2048
BATCH = 4
SEQ = 2048
DEPTH = 1
DEC_BATCH = 128
DEC_SEQ = 4
PAST_LEN = 16384
PAGE_SIZE = 128

N_MEM = 256
C_CONV = 1024
CONV_WIDTH = 31
DN_HEADS = 8
DN_DK = 128
DN_DV = 128
DN_CONV = 4
DN_CHUNK = 64
XA_HEADS = 4
XA_DIM = 256
QKV_W = DN_HEADS * (2 * DN_DK + DN_DV)
D_FF = -(-(8 * D_MODEL) // (3 * 256)) * 256
IN_SIZES = (2 * C_CONV, QKV_W, DN_HEADS * DN_DV, DN_HEADS, DN_HEADS, XA_HEADS * XA_DIM, 3 * D_MODEL)
IN_TOTAL = sum(IN_SIZES)
IN_SPLITS = tuple(sum(IN_SIZES[:i + 1]) for i in range(len(IN_SIZES) - 1))

kernel_name = 'hybrid_conformer_gdn_memxattn_decode_step'


def rmsnorm(x, g, eps=1e-6):
    xf = x.astype(jnp.float32)
    y = xf * lax.rsqrt(jnp.mean(xf * xf, axis=-1, keepdims=True) + eps)
    return (y * g.astype(jnp.float32)).astype(x.dtype)


def layernorm(x, g, b, eps=1e-5):
    xf = x.astype(jnp.float32)
    mu = jnp.mean(xf, axis=-1, keepdims=True)
    xc = xf - mu
    y = xc * lax.rsqrt(jnp.mean(xc * xc, axis=-1, keepdims=True) + eps)
    return (y * g.astype(jnp.float32) + b.astype(jnp.float32)).astype(x.dtype)


def l2norm(x, eps=1e-6):
    return x * lax.rsqrt(jnp.sum(x * x, axis=-1, keepdims=True) + eps)


def causal_depthwise(x_ext, w):
    c = x_ext.shape[-1]
    return lax.conv_general_dilated(x_ext, w[:, None, :].astype(x_ext.dtype), window_strides=(1,),
                                    padding='VALID', dimension_numbers=('NWC', 'WIO', 'NWC'),
                                    feature_group_count=c)


def gated_delta_chunked(q, k, v, g, beta, s0):
    B, T, H, DK = q.shape
    DV = v.shape[-1]
    C = min(DN_CHUNK, T)
    n = -(-T // C)
    pad = n * C - T

    def prep(a):
        a = jnp.pad(a, [(0, 0), (0, pad)] + [(0, 0)] * (a.ndim - 2))
        a = a.reshape((B, n, C) + a.shape[2:])
        return jnp.moveaxis(a, 3, 1)

    q, k, v, g, beta = (prep(a) for a in (q, k, v, g, beta))
    gam = jnp.cumsum(g, axis=-1)
    diff = gam[..., :, None] - gam[..., None, :]
    causal = jnp.tril(jnp.ones((C, C), dtype=bool))
    strict = jnp.tril(jnp.ones((C, C), dtype=bool), -1)
    decay_incl = jnp.exp(jnp.where(causal, diff, -jnp.inf))
    decay_strict = jnp.where(strict, decay_incl, 0.0)
    kb = k * beta[..., None]
    a_mat = jnp.einsum('bhncd,bhnsd->bhncs', kb, k) * decay_strict + jnp.eye(C, dtype=jnp.float32)
    rhs = jnp.concatenate([v * beta[..., None], kb * jnp.exp(gam)[..., None]], axis=-1)
    sol = lax.linalg.triangular_solve(a_mat, rhs, left_side=True, lower=True)
    u, kcum = sol[..., :DV], sol[..., DV:]
    qk = jnp.einsum('bhncd,bhnsd->bhncs', q, k) * decay_incl
    qg = q * jnp.exp(gam)[..., None]
    kg = k * jnp.exp(gam[..., -1:] - gam)[..., None]
    g_last = jnp.exp(gam[..., -1])
    xs = tuple(jnp.moveaxis(a, 2, 0) for a in (qg, kg, u, kcum, qk, g_last))

    def step(s, inp):
        qg_i, kg_i, u_i, kc_i, qk_i, gl_i = inp
        w = u_i - jnp.einsum('bhcd,bhde->bhce', kc_i, s)
        o = jnp.einsum('bhcd,bhde->bhce', qg_i, s) + jnp.einsum('bhcs,bhse->bhce', qk_i, w)
        s = s * gl_i[..., None, None] + jnp.einsum('bhcd,bhce->bhde', kg_i, w)
        return s, o

    s_fin, o = lax.scan(step, s0, xs)
    o = jnp.transpose(o, (1, 0, 3, 2, 4)).reshape(B, n * C, H, DV)[:, :T]
    return o, s_fin


def memory_kv(mem, g_mem, w_mem_kv):
    B = mem.shape[0]
    m = rmsnorm(mem, g_mem) @ w_mem_kv
    mk, mv = jnp.split(m, 2, axis=-1)
    return (mk.reshape(B, N_MEM, XA_HEADS, XA_DIM), mv.reshape(B, N_MEM, XA_HEADS, XA_DIM))


def decoder_layer(x, mem_k, mem_v, conv_buf, dconv_buf, s0, lw):
    B, T, _ = x.shape
    dt = x.dtype
    f32 = jnp.float32
    u = rmsnorm(x, lw['g_pre_mix'])
    proj = u @ lw['w_in']
    glu, qkv, z, b_raw, a_raw, xq, gates = jnp.split(proj, IN_SPLITS, axis=-1)

    ga, gb = jnp.split(glu, 2, axis=-1)
    c_ext = jnp.concatenate([conv_buf.astype(dt), ga * jax.nn.sigmoid(gb)], axis=1)
    c = causal_depthwise(c_ext, lw['w_dw']) + lw['b_dw']
    c = jax.nn.silu(layernorm(c, lw['ln_g'], lw['ln_b']))
    br_conv = c @ lw['w_p_conv'] + lw['b_p_conv']
    new_conv_buf = c_ext[:, -(CONV_WIDTH - 1):]

    d_ext = jnp.concatenate([dconv_buf.astype(dt), qkv], axis=1)
    qkv_c = jax.nn.silu(causal_depthwise(d_ext, lw['w_dconv'])).astype(f32)
    new_dconv_buf = d_ext[:, -(DN_CONV - 1):]
    q, k, v = jnp.split(qkv_c, [DN_HEADS * DN_DK, 2 * DN_HEADS * DN_DK], axis=-1)
    q = l2norm(q.reshape(B, T, DN_HEADS, DN_DK)) * (DN_DK ** -0.5)
    k = l2norm(k.reshape(B, T, DN_HEADS, DN_DK))
    v = v.reshape(B, T, DN_HEADS, DN_DV)
    beta = jax.nn.sigmoid(b_raw.astype(f32))
    g = -jnp.exp(lw['a_log'].astype(f32)) * jax.nn.softplus(a_raw.astype(f32) + lw['dt_bias'].astype(f32))
    o, s_new = gated_delta_chunked(q, k, v, g, beta, s0.astype(f32))
    o = rmsnorm(o, lw['g_dn_norm']) * jax.nn.silu(z.astype(f32).reshape(B, T, DN_HEADS, DN_DV))
    br_delta = o.reshape(B, T, DN_HEADS * DN_DV).astype(dt) @ lw['w_p_delta']

    xq = xq.reshape(B, T, XA_HEADS, XA_DIM)
    sc = jnp.einsum('bthd,bmhd->bhtm', xq, mem_k.astype(dt)).astype(f32) * (XA_DIM ** -0.5)
    pr = jax.nn.softmax(sc, axis=-1).astype(dt)
    xo = jnp.einsum('bhtm,bmhd->bthd', pr, mem_v.astype(dt)).reshape(B, T, XA_HEADS * XA_DIM)
    br_mem = xo @ lw['w_p_mem']

    gc, gd, gx = jnp.split(jax.nn.sigmoid(gates.astype(f32)).astype(dt), 3, axis=-1)
    mix = (gc * br_conv + gd * br_delta + gx * br_mem) @ lw['w_o']
    h = x + rmsnorm(mix, lw['g_post_mix'])

    f = rmsnorm(h, lw['g_pre_ffn'])
    f = (jax.nn.silu(f @ lw['w_gate']) * (f @ lw['w_up'])) @ lw['w_down']
    y = h + rmsnorm(f, lw['g_post_ffn'])
    return y, new_conv_buf, new_dconv_buf, s_new


def setup_inputs(seed: int = 0) -> dict:
    key = jax.random.key(seed)
    ks = iter(list(jax.random.split(key, 48)))
    f32 = jnp.float32

    def nrm(shape, scale):
        return scale * jax.random.normal(next(ks), shape, f32)

    def gain(shape):
        return 1.0 + nrm(shape, 0.02)

    L = DEPTH
    XW = XA_HEADS * XA_DIM
    inp = {}
    inp['x_prompt'] = nrm((BATCH, SEQ, D_MODEL), 1.0)
    inp['x_sample'] = nrm((DEC_BATCH, DEC_SEQ, D_MODEL), 1.0)
    inp['mem_prompt'] = nrm((BATCH, N_MEM, D_MODEL), 1.0)
    inp['cache_mem_k'] = nrm((L, DEC_BATCH, N_MEM, XA_HEADS, XA_DIM), 1.0)
    inp['cache_mem_v'] = nrm((L, DEC_BATCH, N_MEM, XA_HEADS, XA_DIM), 1.0)
    inp['state_conv'] = nrm((L, DEC_BATCH, CONV_WIDTH - 1, C_CONV), 0.5)
    inp['state_delta_conv'] = nrm((L, DEC_BATCH, DN_CONV - 1, QKV_W), 1.0)
    inp['state_delta'] = nrm((L, DEC_BATCH, DN_HEADS, DN_DK, DN_DV), 0.05)
    inp['g_pre_mix'] = gain((L, D_MODEL))
    inp['w_in'] = nrm((L, D_MODEL, IN_TOTAL), D_MODEL ** -0.5)
    inp['w_dw'] = nrm((L, CONV_WIDTH, C_CONV), CONV_WIDTH ** -0.5)
    inp['b_dw'] = nrm((L, C_CONV), 0.02)
    inp['ln_g'] = gain((L, C_CONV))
    inp['ln_b'] = nrm((L, C_CONV), 0.02)
    inp['w_p_conv'] = nrm((L, C_CONV, D_MODEL), C_CONV ** -0.5)
    inp['b_p_conv'] = nrm((L, D_MODEL), 0.02)
    inp['w_dconv'] = nrm((L, DN_CONV, QKV_W), DN_CONV ** -0.5)
    inp['a_log'] = jnp.log(jax.random.uniform(next(ks), (L, DN_HEADS), f32, 1.0, 16.0))
    dtv = jnp.exp(jax.random.uniform(next(ks), (L, DN_HEADS), f32, math.log(1e-3), math.log(1e-1)))
    inp['dt_bias'] = dtv + jnp.log(-jnp.expm1(-dtv))
    inp['g_dn_norm'] = gain((L, DN_DV))
    inp['w_p_delta'] = nrm((L, DN_HEADS * DN_DV, D_MODEL), (DN_HEADS * DN_DV) ** -0.5)
    inp['g_mem'] = gain((L, D_MODEL))
    inp['w_mem_kv'] = nrm((L, D_MODEL, 2 * XW), D_MODEL ** -0.5)
    inp['w_p_mem'] = nrm((L, XW, D_MODEL), XW ** -0.5)
    inp['w_o'] = nrm((L, D_MODEL, D_MODEL), D_MODEL ** -0.5)
    inp['g_post_mix'] = gain((L, D_MODEL))
    inp['g_pre_ffn'] = gain((L, D_MODEL))
    inp['w_gate'] = nrm((L, D_MODEL, D_FF), D_MODEL ** -0.5)
    inp['w_up'] = nrm((L, D_MODEL, D_FF), D_MODEL ** -0.5)
    inp['w_down'] = nrm((L, D_FF, D_MODEL), D_FF ** -0.5)
    inp['g_post_ffn'] = gain((L, D_MODEL))
    return inp


def reference(x_prompt, x_sample, mem_prompt, cache_mem_k, cache_mem_v, state_conv, state_delta_conv,
              state_delta, g_pre_mix, w_in, w_dw, b_dw, ln_g, ln_b, w_p_conv, b_p_conv, w_dconv, a_log,
              dt_bias, g_dn_norm, w_p_delta, g_mem, w_mem_kv, w_p_mem, w_o, g_post_mix, g_pre_ffn, w_gate,
              w_up, w_down, g_post_ffn):
    B = x_prompt.shape[0]
    hp, hs = x_prompt, x_sample
    mk_l, mv_l, pc_l, pd_l, ps_l, sc_l, sd_l, ss_l = [], [], [], [], [], [], [], []
    for l in range(DEPTH):
        lw = dict(g_pre_mix=g_pre_mix[l], w_in=w_in[l], w_dw=w_dw[l], b_dw=b_dw[l], ln_g=ln_g[l],
                  ln_b=ln_b[l], w_p_conv=w_p_conv[l], b_p_conv=b_p_conv[l], w_dconv=w_dconv[l],
                  a_log=a_log[l], dt_bias=dt_bias[l], g_dn_norm=g_dn_norm[l], w_p_delta=w_p_delta[l],
                  w_p_mem=w_p_mem[l], w_o=w_o[l], g_post_mix=g_post_mix[l], g_pre_ffn=g_pre_ffn[l],
                  w_gate=w_gate[l], w_up=w_up[l], w_down=w_down[l], g_post_ffn=g_post_ffn[l])
        mk, mv = memory_kv(mem_prompt, g_mem[l], w_mem_kv[l])
        zc = jnp.zeros((B, CONV_WIDTH - 1, C_CONV), hp.dtype)
        zd = jnp.zeros((B, DN_CONV - 1, QKV_W), hp.dtype)
        zs = jnp.zeros((B, DN_HEADS, DN_DK, DN_DV), jnp.float32)
        hp, pc, pd, ps = decoder_layer(hp, mk, mv, zc, zd, zs, lw)
        hs, sc, sd, ss = decoder_layer(hs, cache_mem_k[l], cache_mem_v[l], state_conv[l],
                                       state_delta_conv[l], state_delta[l], lw)
        mk_l.append(mk); mv_l.append(mv); pc_l.append(pc); pd_l.append(pd); ps_l.append(ps)
        sc_l.append(sc); sd_l.append(sd); ss_l.append(ss)
    return (hp, hs, jnp.stack(mk_l), jnp.stack(mv_l), jnp.stack(pc_l), jnp.stack(pd_l), jnp.stack(ps_l),
            jnp.stack(sc_l), jnp.stack(sd_l), jnp.stack(ss_l))
```

```python
import functools

import jax
import jax.numpy as jnp
from jax import lax
from jax.experimental import pallas as pl
from jax.experimental.pallas import tpu as pltpu

F32 = jnp.float32
BF16 = jnp.bfloat16

D_MODEL = 2048
N_MEM = 256
C_CONV = 1024
CONV_WIDTH = 31
DN_HEADS = 8
DN_DK = 128
DN_DV = 128
DN_CONV = 4
DN_CHUNK = 64
XA_HEADS = 4
XA_DIM = 256
QKV_W = DN_HEADS * (2 * DN_DK + DN_DV)
XW = XA_HEADS * XA_DIM

COL_GATES = 0
COL_GA, COL_GB = 6, 7
COL_Q, COL_K, COL_V = 8, 9, 10
COL_Z = 11
COL_XQ = 12
N_MAIN = 13 * 1024
SMALL_W = 128

VMEM_LIMIT = 52 * 1024 * 1024

_NT = (((1,), (1,)), ((), ()))
_TN = (((0,), (0,)), ((), ()))


def _dot(a, b):
    return jnp.dot(a, b, preferred_element_type=F32)


def _dot_nt(a, b):
    return lax.dot_general(a, b, _NT, preferred_element_type=F32)


def _dot_tn(a, b):
    return lax.dot_general(a, b, _TN, preferred_element_type=F32)


def _sigmoid(x):
    return jax.nn.sigmoid(x)


def _silu(x):
    return x * jax.nn.sigmoid(x)


def _rms_rows(x, g, eps=1e-6):
    ms = jnp.mean(x * x, axis=-1, keepdims=True)
    return x * lax.rsqrt(ms + eps) * g


def _params(sem):
    return pltpu.CompilerParams(dimension_semantics=sem, vmem_limit_bytes=VMEM_LIMIT)


def _const_spec(shape):
    nd = len(shape)
    return pl.BlockSpec(shape, lambda *_: (0,) * nd, pipeline_mode=pl.Buffered(1))


def _norm_rows_to(x_ref, g_ref, u_ref, rows):
    g = g_ref[...]
    tm = x_ref.shape[0]

    def body(i, c):
        r = pl.multiple_of(i * rows, rows)
        xx = x_ref[pl.ds(r, rows), :]
        u_ref[pl.ds(r, rows), :] = _rms_rows(xx, g).astype(BF16)
        return c

    lax.fori_loop(0, tm // rows, body, 0)


def _in_proj_kernel(x_ref, g_ref, w_ref, ws_ref, o_ref, ba_ref, u_ref):
    @pl.when(pl.program_id(1) == 0)
    def _():
        _norm_rows_to(x_ref, g_ref, u_ref, 64)
        ba_ref[...] = _dot(u_ref[...], ws_ref[...])

    o_ref[...] = _dot(u_ref[...], w_ref[...])


def _norm_proj_kernel(x_ref, g_ref, w_ref, o_ref, u_ref):
    @pl.when(pl.program_id(1) == 0)
    def _():
        _norm_rows_to(x_ref, g_ref, u_ref, 64)

    o_ref[...] = _dot(u_ref[...], w_ref[...])


def _in_proj(x2d, g, w, ws, tm, tn):
    m, k = x2d.shape
    n = w.shape[1]
    return pl.pallas_call(
        _in_proj_kernel,
        grid=(m // tm, n // tn),
        in_specs=[
            pl.BlockSpec((tm, k), lambda i, j: (i, 0)),
            pl.BlockSpec((1, k), lambda i, j: (0, 0)),
            pl.BlockSpec((k, tn), lambda i, j: (0, j)),
            pl.BlockSpec((k, SMALL_W), lambda i, j: (0, 0)),
        ],
        out_specs=[
            pl.BlockSpec((tm, tn), lambda i, j: (i, j)),
            pl.BlockSpec((tm, SMALL_W), lambda i, j: (i, 0)),
        ],
        out_shape=[jax.ShapeDtypeStruct((m, n), F32), jax.ShapeDtypeStruct((m, SMALL_W), F32)],
        scratch_shapes=[pltpu.VMEM((tm, k), BF16)],
        compiler_params=_params(("parallel", "arbitrary")),
        name="in_proj",
    )(x2d, g, w, ws)


def _norm_proj(x2d, g, w, tm, tn):
    m, k = x2d.shape
    n = w.shape[1]
    return pl.pallas_call(
        _norm_proj_kernel,
        grid=(m // tm, n // tn),
        in_specs=[
            pl.BlockSpec((tm, k), lambda i, j: (i, 0)),
            pl.BlockSpec((1, k), lambda i, j: (0, 0)),
            pl.BlockSpec((k, tn), lambda i, j: (0, j)),
        ],
        out_specs=pl.BlockSpec((tm, tn), lambda i, j: (i, j)),
        out_shape=jax.ShapeDtypeStruct((m, n), F32),
        scratch_shapes=[pltpu.VMEM((tm, k), BF16)],
        compiler_params=_params(("parallel", "arbitrary")),
        name="mem_kv_proj",
    )(x2d, g, w)


def _ln_silu(c, g, b, eps=1e-5):
    mu = jnp.mean(c, axis=-1, keepdims=True)
    xc = c - mu
    var = jnp.mean(xc * xc, axis=-1, keepdims=True)
    return _silu(xc * lax.rsqrt(var + eps) * g + b)


CONV_PAD = 32
CONV_ROWS = 32


def _conv_prompt_kernel(ga_ref, gb_ref, wdw_ref, bdw_ref, lng_ref, lnb_ref, c_ref, st_ref, ext_ref):
    t = pl.program_id(1)
    tt = ga_ref.shape[0]
    hist = CONV_WIDTH - 1

    @pl.when(t == 0)
    def _():
        ext_ref[0:CONV_PAD, :] = jnp.zeros((CONV_PAD, C_CONV), F32)

    ext_ref[CONV_PAD:CONV_PAD + tt, :] = ga_ref[...] * _sigmoid(gb_ref[...])
    bias = bdw_ref[...]
    lng = lng_ref[...]
    lnb = lnb_ref[...]
    for r0 in range(0, tt, CONV_ROWS):
        acc = jnp.broadcast_to(bias, (CONV_ROWS, C_CONV))
        for j in range(CONV_WIDTH):
            s = CONV_PAD - hist + r0 + j
            acc = acc + wdw_ref[j:j + 1, :] * ext_ref[s:s + CONV_ROWS, :]
        c_ref[r0:r0 + CONV_ROWS, :] = _ln_silu(acc, lng, lnb).astype(c_ref.dtype)

    @pl.when(t == pl.num_programs(1) - 1)
    def _():
        st_ref[0] = ext_ref[CONV_PAD + tt - hist:CONV_PAD + tt, :]

    ext_ref[0:CONV_PAD, :] = ext_ref[tt:tt + CONV_PAD, :]


def _conv_prompt(proj, wdw, bdw, lng, lnb, nb, t_len, tt):
    nt = t_len // tt
    return pl.pallas_call(
        _conv_prompt_kernel,
        grid=(nb, nt),
        in_specs=[
            pl.BlockSpec((tt, C_CONV), lambda b, t: (b * nt + t, COL_GA)),
            pl.BlockSpec((tt, C_CONV), lambda b, t: (b * nt + t, COL_GB)),
            pl.BlockSpec((CONV_WIDTH, C_CONV), lambda b, t: (0, 0)),
            pl.BlockSpec((1, C_CONV), lambda b, t: (0, 0)),
            pl.BlockSpec((1, C_CONV), lambda b, t: (0, 0)),
            pl.BlockSpec((1, C_CONV), lambda b, t: (0, 0)),
        ],
        out_specs=[
            pl.BlockSpec((tt, C_CONV), lambda b, t: (b * nt + t, 0)),
            pl.BlockSpec((1, CONV_WIDTH - 1, C_CONV), lambda b, t: (b, 0, 0)),
        ],
        out_shape=[
            jax.ShapeDtypeStruct((nb * t_len, C_CONV), BF16),
            jax.ShapeDtypeStruct((nb, CONV_WIDTH - 1, C_CONV), F32),
        ],
        scratch_shapes=[pltpu.VMEM((CONV_PAD + tt, C_CONV), F32)],
        compiler_params=_params(("parallel", "arbitrary")),
        name="conv_prompt",
    )(proj, proj, wdw, bdw, lng, lnb)


def _conv_sample_kernel(ga_ref, gb_ref, st_ref, wdw_ref, bdw_ref, lng_ref, lnb_ref, c_ref, nst_ref, ext_ref,
                        *, bb, t_len):
    hist = CONV_WIDTH - 1
    cin = ga_ref[...] * _sigmoid(gb_ref[...])
    bias = bdw_ref[...]
    rows = []
    for i in range(bb):
        ext_ref[i, 0:hist, :] = st_ref[i]
        ext_ref[i, hist:hist + t_len, :] = cin[i * t_len:(i + 1) * t_len, :]
        acc = jnp.broadcast_to(bias, (t_len, C_CONV))
        for j in range(CONV_WIDTH):
            acc = acc + wdw_ref[j:j + 1, :] * ext_ref[i, j:j + t_len, :]
        rows.append(acc)
        nst_ref[i] = ext_ref[i, t_len:t_len + hist, :]
    conv = jnp.concatenate(rows, axis=0)
    c_ref[...] = _ln_silu(conv, lng_ref[...], lnb_ref[...]).astype(c_ref.dtype)


def _conv_sample(proj, state, wdw, bdw, lng, lnb, nb, t_len, bb):
    hist = CONV_WIDTH - 1
    rows = bb * t_len
    return pl.pallas_call(
        functools.partial(_conv_sample_kernel, bb=bb, t_len=t_len),
        grid=(nb // bb,),
        in_specs=[
            pl.BlockSpec((rows, C_CONV), lambda i: (i, COL_GA)),
            pl.BlockSpec((rows, C_CONV), lambda i: (i, COL_GB)),
            pl.BlockSpec((bb, hist, C_CONV), lambda i: (i, 0, 0)),
            pl.BlockSpec((CONV_WIDTH, C_CONV), lambda i: (0, 0)),
            pl.BlockSpec((1, C_CONV), lambda i: (0, 0)),
            pl.BlockSpec((1, C_CONV), lambda i: (0, 0)),
            pl.BlockSpec((1, C_CONV), lambda i: (0, 0)),
        ],
        out_specs=[
            pl.BlockSpec((rows, C_CONV), lambda i: (i, 0)),
            pl.BlockSpec((bb, hist, C_CONV), lambda i: (i, 0, 0)),
        ],
        out_shape=[
            jax.ShapeDtypeStruct((nb * t_len, C_CONV), F32),
            jax.ShapeDtypeStruct((nb, hist, C_CONV), F32),
        ],
        scratch_shapes=[pltpu.VMEM((bb, hist + 2 * t_len + 8, C_CONV), F32)],
        compiler_params=_params(("parallel",)),
        name="conv_sample",
    )(proj, proj, state, wdw, bdw, lng, lnb)


DN_PAD = 8


def _split3(x):
    hi = x.astype(BF16)
    r1 = x - hi.astype(F32)
    mid = r1.astype(BF16)
    lo = (r1 - mid.astype(F32)).astype(BF16)
    return hi, mid, lo


def _delta_chunk(ext_ref, z, ba, s_ref, wdc_ref, arow_ref, dtrow_ref, gdn_ref, store_o, *, c, n_valid):
    hist = DN_CONV - 1
    conv = wdc_ref[0:1, :] * ext_ref[DN_PAD - hist:DN_PAD - hist + c, :]
    for j in range(1, DN_CONV):
        s = DN_PAD - hist + j
        conv = conv + wdc_ref[j:j + 1, :] * ext_ref[s:s + c, :]
    qkv = _silu(conv)

    row = lax.broadcasted_iota(jnp.int32, (c, SMALL_W), 0)
    valid = row < n_valid
    beta_all = jnp.where(valid, _sigmoid(ba), 0.0)
    g_all = jnp.where(valid, arow_ref[...] * jax.nn.softplus(ba + dtrow_ref[...]), 0.0)

    ii = lax.broadcasted_iota(jnp.int32, (c, c), 0)
    jj = lax.broadcasted_iota(jnp.int32, (c, c), 1)
    causal = jj <= ii
    strict = jj < ii
    tri = jnp.where(causal, 1.0, 0.0).astype(BF16)
    ghi, gmid, glo = _split3(g_all)
    gam_all = _dot(tri, ghi) + _dot(tri, gmid) + _dot(tri, glo)
    gam_t = gam_all.T
    gdn = gdn_ref[...]

    n_steps = 0
    while (1 << (n_steps + 1)) < n_valid:
        n_steps += 1

    for h in range(DN_HEADS):
        q = qkv[:, h * DN_DK:(h + 1) * DN_DK]
        k = qkv[:, 1024 + h * DN_DK:1024 + (h + 1) * DN_DK]
        v = qkv[:, 2048 + h * DN_DV:2048 + (h + 1) * DN_DV]
        q = q * lax.rsqrt(jnp.sum(q * q, axis=-1, keepdims=True) + 1e-6) * (DN_DK ** -0.5)
        k = k * lax.rsqrt(jnp.sum(k * k, axis=-1, keepdims=True) + 1e-6)
        if n_valid < c:
            k = jnp.where(valid[:, 0:1], k, 0.0)
        beta = beta_all[:, h:h + 1]
        gam_c = gam_all[:, 8 + h:9 + h]
        gam_r = gam_t[8 + h:9 + h, :]
        g_last = gam_r[:, c - 1:c]

        dec = jnp.exp(jnp.where(causal, gam_c - gam_r, -1e30))
        kb = k * beta
        k16 = k.astype(BF16)
        gmat = _dot_nt(jnp.concatenate([kb, q], axis=0).astype(BF16), k16)
        lmat = jnp.where(strict, gmat[0:c] * dec, 0.0)
        qk = gmat[c:2 * c] * dec

        e = -lmat
        p = e
        for _ in range(n_steps):
            p16 = p.astype(BF16)
            p = _dot(p16, p16)
            e = e + p + _dot(e.astype(BF16), p.astype(BF16))

        eg = jnp.exp(gam_c)
        rhs = jnp.concatenate([v * beta, kb * eg], axis=1)
        sol = rhs + _dot(e.astype(BF16), rhs.astype(BF16))
        u = sol[:, 0:DN_DV]
        kcum = sol[:, DN_DV:DN_DV + DN_DK]

        s_old = s_ref[h]
        xs = _dot(jnp.concatenate([kcum, q * eg], axis=0).astype(BF16), s_old.astype(BF16))
        w = u - xs[0:c]
        w16 = w.astype(BF16)
        o = xs[c:2 * c] + _dot(qk.astype(BF16), w16)
        kg = k * jnp.exp(g_last - gam_c)
        s_ref[h] = s_old * jnp.exp(g_last) + _dot_tn(kg.astype(BF16), w16)

        zh = z[:, h * DN_DV:(h + 1) * DN_DV]
        store_o(h, _rms_rows(o, gdn) * _silu(zh))


def _delta_prompt_kernel(q_ref, k_ref, v_ref, z_ref, ba_ref, wdc_ref, arow_ref, dtrow_ref, gdn_ref,
                         o_ref, nd_ref, ns_ref, ext_ref, s_ref):
    t = pl.program_id(1)
    c = q_ref.shape[0]
    hist = DN_CONV - 1

    @pl.when(t == 0)
    def _():
        ext_ref[0:DN_PAD, :] = jnp.zeros((DN_PAD, QKV_W), F32)
        s_ref[...] = jnp.zeros(s_ref.shape, F32)

    ext_ref[DN_PAD:DN_PAD + c, 0:1024] = q_ref[...]
    ext_ref[DN_PAD:DN_PAD + c, 1024:2048] = k_ref[...]
    ext_ref[DN_PAD:DN_PAD + c, 2048:3072] = v_ref[...]

    def store_o(h, val):
        o_ref[:, h * DN_DV:(h + 1) * DN_DV] = val.astype(o_ref.dtype)

    _delta_chunk(ext_ref, z_ref[...], ba_ref[...], s_ref, wdc_ref, arow_ref, dtrow_ref, gdn_ref, store_o,
                 c=c, n_valid=c)

    @pl.when(t == pl.num_programs(1) - 1)
    def _():
        nd_ref[0] = ext_ref[DN_PAD + c - hist:DN_PAD + c, :]
        ns_ref[0] = s_ref[...]

    ext_ref[DN_PAD - hist:DN_PAD, :] = ext_ref[DN_PAD + c - hist:DN_PAD + c, :]


def _delta_prompt(proj, ba, wdc, arow, dtrow, gdn, nb, t_len):
    c = DN_CHUNK
    nt = t_len // c
    row_spec = lambda col: pl.BlockSpec((c, 1024), lambda b, t: (b * nt + t, col))
    small = lambda shape: pl.BlockSpec(shape, lambda b, t: (0,) * len(shape))
    return pl.pallas_call(
        _delta_prompt_kernel,
        grid=(nb, nt),
        in_specs=[
            row_spec(COL_Q), row_spec(COL_K), row_spec(COL_V), row_spec(COL_Z),
            pl.BlockSpec((c, SMALL_W), lambda b, t: (b * nt + t, 0)),
            small((DN_CONV, QKV_W)), small((1, SMALL_W)), small((1, SMALL_W)), small((1, DN_DV)),
        ],
        out_specs=[
            pl.BlockSpec((c, 1024), lambda b, t: (b * nt + t, 0)),
            pl.BlockSpec((1, DN_CONV - 1, QKV_W), lambda b, t: (b, 0, 0)),
            pl.BlockSpec((1, DN_HEADS, DN_DK, DN_DV), lambda b, t: (b, 0, 0, 0)),
        ],
        out_shape=[
            jax.ShapeDtypeStruct((nb * t_len, 1024), BF16),
            jax.ShapeDtypeStruct((nb, DN_CONV - 1, QKV_W), F32),
            jax.ShapeDtypeStruct((nb, DN_HEADS, DN_DK, DN_DV), F32),
        ],
        scratch_shapes=[pltpu.VMEM((DN_PAD + c, QKV_W), F32), pltpu.VMEM((DN_HEADS, DN_DK, DN_DV), F32)],
        compiler_params=_params(("parallel", "arbitrary")),
        name="delta_prompt",
    )(proj, proj, proj, proj, ba, wdc, arow, dtrow, gdn)


DS_ROWS = 8


def _delta_sample_kernel(q_ref, k_ref, v_ref, z_ref, ba_ref, dst_ref, s0_ref, wdc_ref, arow_ref, dtrow_ref,
                         gdn_ref, o_ref, nd_ref, ns_ref, ext_ref, *, t_len):
    hist = DN_CONV - 1
    z_all = z_ref[...]
    ba_all = ba_ref[...]
    ns_ref[...] = s0_ref[...]
    for i in range(DS_ROWS // t_len):
        r0 = i * t_len
        ext_ref[i] = jnp.zeros((DN_PAD + DS_ROWS, QKV_W), F32)
        ext_ref[i, DN_PAD - hist:DN_PAD, :] = dst_ref[i]
        ext_ref[i, DN_PAD:DN_PAD + t_len, 0:1024] = q_ref[r0:r0 + t_len, :]
        ext_ref[i, DN_PAD:DN_PAD + t_len, 1024:2048] = k_ref[r0:r0 + t_len, :]
        ext_ref[i, DN_PAD:DN_PAD + t_len, 2048:3072] = v_ref[r0:r0 + t_len, :]
        nd_ref[i] = ext_ref[i, DN_PAD + t_len - hist:DN_PAD + t_len, :]
        z = z_all if r0 == 0 else pltpu.roll(z_all, DS_ROWS - r0, 0)
        ba = ba_all if r0 == 0 else pltpu.roll(ba_all, DS_ROWS - r0, 0)

        def store_o(h, val, r0=r0):
            o_ref[r0:r0 + t_len, h * DN_DV:(h + 1) * DN_DV] = val[0:t_len].astype(o_ref.dtype)

        _delta_chunk(ext_ref.at[i], z, ba, ns_ref.at[i], wdc_ref, arow_ref, dtrow_ref, gdn_ref, store_o,
                     c=DS_ROWS, n_valid=t_len)


def _delta_sample(proj, ba, dstate, s0, wdc, arow, dtrow, gdn, nb, t_len):
    bb = DS_ROWS // t_len
    rows = DS_ROWS
    hist = DN_CONV - 1
    row_spec = lambda col: pl.BlockSpec((rows, 1024), lambda i: (i, col))
    small = lambda shape: pl.BlockSpec(shape, lambda i: (0,) * len(shape))
    return pl.pallas_call(
        functools.partial(_delta_sample_kernel, t_len=t_len),
        grid=(nb // bb,),
        in_specs=[
            row_spec(COL_Q), row_spec(COL_K), row_spec(COL_V), row_spec(COL_Z),
            pl.BlockSpec((rows, SMALL_W), lambda i: (i, 0)),
            pl.BlockSpec((bb, hist, QKV_W), lambda i: (i, 0, 0)),
            pl.BlockSpec((bb, DN_HEADS, DN_DK, DN_DV), lambda i: (i, 0, 0, 0)),
            small((DN_CONV, QKV_W)), small((1, SMALL_W)), small((1, SMALL_W)), small((1, DN_DV)),
        ],
        out_specs=[
            pl.BlockSpec((rows, 1024), lambda i: (i, 0)),
            pl.BlockSpec((bb, hist, QKV_W), lambda i: (i, 0, 0)),
            pl.BlockSpec((bb, DN_HEADS, DN_DK, DN_DV), lambda i: (i, 0, 0, 0)),
        ],
        out_shape=[
            jax.ShapeDtypeStruct((nb * t_len, 1024), F32),
            jax.ShapeDtypeStruct((nb, hist, QKV_W), F32),
            jax.ShapeDtypeStruct((nb, DN_HEADS, DN_DK, DN_DV), F32),
        ],
        scratch_shapes=[pltpu.VMEM((bb, DN_PAD + DS_ROWS, QKV_W), F32)],
        compiler_params=_params(("parallel",)),
        name="delta_sample",
    )(proj, proj, proj, proj, ba, dstate, s0, wdc, arow, dtrow, gdn)


def _softmax_rows(s):
    m = jnp.max(s, axis=-1, keepdims=True)
    p = jnp.exp(s - m)
    return p / jnp.sum(p, axis=-1, keepdims=True)


def _attn_head(q16, k, v):
    s = _dot_nt(q16, k.astype(BF16)) * (XA_DIM ** -0.5)
    return _dot(_softmax_rows(s).astype(BF16), v.astype(BF16))


def _attn_prompt_kernel(xq_ref, k_ref, v_ref, o_ref):
    for h in range(XA_HEADS):
        sl = slice(h * XA_DIM, (h + 1) * XA_DIM)
        o = _attn_head(xq_ref[:, sl].astype(BF16), k_ref[:, sl], v_ref[:, sl])
        o_ref[:, sl] = o.astype(o_ref.dtype)


def _attn_prompt(proj, mkv, nb, t_len, tq):
    nt = t_len // tq
    return pl.pallas_call(
        _attn_prompt_kernel,
        grid=(nb, nt),
        in_specs=[
            pl.BlockSpec((tq, XW), lambda b, t: (b * nt + t, COL_XQ)),
            pl.BlockSpec((N_MEM, XW), lambda b, t: (b, 0)),
            pl.BlockSpec((N_MEM, XW), lambda b, t: (b, 1)),
        ],
        out_specs=pl.BlockSpec((tq, XW), lambda b, t: (b * nt + t, 0)),
        out_shape=jax.ShapeDtypeStruct((nb * t_len, XW), BF16),
        compiler_params=_params(("parallel", "arbitrary")),
        name="attn_prompt",
    )(proj, mkv, mkv)


def _attn_sample_kernel(xq_ref, k_ref, v_ref, o_ref, *, bb, t_len):
    per = DS_ROWS // t_len
    rows = lax.broadcasted_iota(jnp.int32, (DS_ROWS, XA_DIM), 0)
    for grp in range(bb // per):
        r0 = grp * DS_ROWS
        for h in range(XA_HEADS):
            sl = slice(h * XA_DIM, (h + 1) * XA_DIM)
            q16 = xq_ref[r0:r0 + DS_ROWS, sl].astype(BF16)
            o = None
            for s in range(per):
                b = grp * per + s
                ob = _attn_head(q16, k_ref[b, :, sl], v_ref[b, :, sl])
                o = ob if o is None else jnp.where(rows < s * t_len, o, ob)
            o_ref[r0:r0 + DS_ROWS, sl] = o.astype(o_ref.dtype)


def _attn_sample(proj, ck, cv, nb, t_len, bb):
    rows = bb * t_len
    return pl.pallas_call(
        functools.partial(_attn_sample_kernel, bb=bb, t_len=t_len),
        grid=(nb // bb,),
        in_specs=[
            pl.BlockSpec((rows, XW), lambda i: (i, COL_XQ)),
            pl.BlockSpec((bb, N_MEM, XW), lambda i: (i, 0, 0)),
            pl.BlockSpec((bb, N_MEM, XW), lambda i: (i, 0, 0)),
        ],
        out_specs=pl.BlockSpec((rows, XW), lambda i: (i, 0)),
        out_shape=jax.ShapeDtypeStruct((nb * t_len, XW), F32),
        compiler_params=_params(("parallel",)),
        name="attn_sample",
    )(proj, ck, cv)


def _merge_kernel(c_ref, d_ref, m_ref, gc_ref, gd_ref, gx_ref, x_ref, wc_ref, bc_ref, wd_ref, wm_ref, wo_ref,
                  gpm_ref, h_ref):
    br_c = _dot(c_ref[...].astype(BF16), wc_ref[...]) + bc_ref[...]
    mix = _sigmoid(gc_ref[...]) * br_c
    mix = mix + _sigmoid(gd_ref[...]) * _dot(d_ref[...].astype(BF16), wd_ref[...])
    mix = mix + _sigmoid(gx_ref[...]) * _dot(m_ref[...].astype(BF16), wm_ref[...])
    out = _dot(mix.astype(BF16), wo_ref[...])
    h_ref[...] = x_ref[...] + _rms_rows(out, gpm_ref[...])


def _merge(c, d, m, proj, x2d, wc, bc, wd, wm, wo, gpm, tm):
    rows = x2d.shape[0]
    row = lambda width, col=0: pl.BlockSpec((tm, width), lambda i: (i, col))
    return pl.pallas_call(
        _merge_kernel,
        grid=(rows // tm,),
        in_specs=[
            row(1024), row(1024), row(1024),
            row(D_MODEL, 0), row(D_MODEL, 1), row(D_MODEL, 2),
            row(D_MODEL),
            _const_spec((C_CONV, D_MODEL)), _const_spec((1, D_MODEL)),
            _const_spec((1024, D_MODEL)), _const_spec((XW, D_MODEL)),
            _const_spec((D_MODEL, D_MODEL)), _const_spec((1, D_MODEL)),
        ],
        out_specs=row(D_MODEL),
        out_shape=jax.ShapeDtypeStruct((rows, D_MODEL), F32),
        compiler_params=_params(("parallel",)),
        name="merge",
    )(c, d, m, proj, proj, proj, x2d, wc, bc, wd, wm, wo, gpm)


def _ffn_kernel(h_ref, g1_ref, wg_ref, wu_ref, wd_ref, g2_ref, y_ref, fn_ref, acc_ref):
    f = pl.program_id(1)

    @pl.when(f == 0)
    def _():
        _norm_rows_to(h_ref, g1_ref, fn_ref, 64)

    fn = fn_ref[...]
    act = (_silu(_dot(fn, wg_ref[...])) * _dot(fn, wu_ref[...])).astype(BF16)
    part = _dot(act, wd_ref[...])

    @pl.when(f == 0)
    def _():
        acc_ref[...] = part

    @pl.when(f > 0)
    def _():
        acc_ref[...] += part

    @pl.when(f == pl.num_programs(1) - 1)
    def _():
        y_ref[...] = h_ref[...] + _rms_rows(acc_ref[...], g2_ref[...])


def _ffn(h2d, g1, wg, wu, wd, g2, tm, tf):
    rows = h2d.shape[0]
    d_ff = wg.shape[1]
    return pl.pallas_call(
        _ffn_kernel,
        grid=(rows // tm, d_ff // tf),
        in_specs=[
            pl.BlockSpec((tm, D_MODEL), lambda i, f: (i, 0)),
            pl.BlockSpec((1, D_MODEL), lambda i, f: (0, 0)),
            pl.BlockSpec((D_MODEL, tf), lambda i, f: (0, f)),
            pl.BlockSpec((D_MODEL, tf), lambda i, f: (0, f)),
            pl.BlockSpec((tf, D_MODEL), lambda i, f: (f, 0)),
            pl.BlockSpec((1, D_MODEL), lambda i, f: (0, 0)),
        ],
        out_specs=pl.BlockSpec((tm, D_MODEL), lambda i, f: (i, 0)),
        out_shape=jax.ShapeDtypeStruct((rows, D_MODEL), F32),
        scratch_shapes=[pltpu.VMEM((tm, D_MODEL), BF16), pltpu.VMEM((tm, D_MODEL), F32)],
        compiler_params=_params(("parallel", "arbitrary")),
        name="ffn",
    )(h2d, g1, wg, wu, wd, g2)


def _row(v):
    return v.reshape(1, -1).astype(F32)


def _layer_weights(w_in, w_dw, b_dw, ln_g, ln_b, w_p_conv, b_p_conv, w_dconv, a_log, dt_bias, g_dn_norm,
                   w_p_delta, w_p_mem, w_o, w_gate, w_up, w_down):
    glu0, qkv0, z0 = 0, 2 * C_CONV, 2 * C_CONV + QKV_W
    b0 = z0 + DN_HEADS * DN_DV
    xq0 = b0 + 2 * DN_HEADS
    gates0 = xq0 + XW
    w_main = jnp.concatenate(
        [w_in[:, gates0:gates0 + 3 * D_MODEL], w_in[:, glu0:qkv0], w_in[:, qkv0:z0], w_in[:, z0:b0],
         w_in[:, xq0:gates0]], axis=1).astype(BF16)
    w_small = jnp.pad(w_in[:, b0:xq0], ((0, 0), (0, SMALL_W - 2 * DN_HEADS))).astype(BF16)
    lane_pad = (0, SMALL_W - 2 * DN_HEADS)
    arow = jnp.pad(jnp.concatenate([jnp.zeros((DN_HEADS,), F32), -jnp.exp(a_log.astype(F32))]), lane_pad)
    dtrow = jnp.pad(jnp.concatenate([jnp.zeros((DN_HEADS,), F32), dt_bias.astype(F32)]), lane_pad)
    return dict(
        w_main=w_main, w_small=w_small, w_dw=w_dw, b_dw=_row(b_dw), ln_g=_row(ln_g), ln_b=_row(ln_b),
        w_p_conv=w_p_conv.astype(BF16), b_p_conv=_row(b_p_conv), w_dconv=w_dconv, arow=_row(arow),
        dtrow=_row(dtrow), g_dn=_row(g_dn_norm), w_p_delta=w_p_delta.astype(BF16),
        w_p_mem=w_p_mem.astype(BF16), w_o=w_o.astype(BF16), w_gate=w_gate.astype(BF16),
        w_up=w_up.astype(BF16), w_down=w_down.astype(BF16))


def _tile(n, pref):
    return pref if n % pref == 0 else n


def kernel(x_prompt, x_sample, mem_prompt, cache_mem_k, cache_mem_v, state_conv, state_delta_conv, state_delta, g_pre_mix, w_in, w_dw, b_dw, ln_g, ln_b, w_p_conv, b_p_conv, w_dconv, a_log, dt_bias, g_dn_norm, w_p_delta, g_mem, w_mem_kv, w_p_mem, w_o, g_post_mix, g_pre_ffn, w_gate, w_up, w_down, g_post_ffn):
    depth = w_in.shape[0]
    assert depth == 1, "single-layer stack"
    l = 0
    pb, pt, _ = x_prompt.shape
    sb, st, _ = x_sample.shape
    assert pt % DN_CHUNK == 0 and st <= DS_ROWS and DS_ROWS % st == 0

    lw = _layer_weights(w_in[l], w_dw[l], b_dw[l], ln_g[l], ln_b[l], w_p_conv[l], b_p_conv[l], w_dconv[l],
                        a_log[l], dt_bias[l], g_dn_norm[l], w_p_delta[l], w_p_mem[l], w_o[l], w_gate[l],
                        w_up[l], w_down[l])
    gpre, gpm, gff1, gff2 = _row(g_pre_mix[l]), _row(g_post_mix[l]), _row(g_pre_ffn[l]), _row(g_post_ffn[l])

    def tail(x2d, proj, c, d, m):
        rows = x2d.shape[0]
        h = _merge(c, d, m, proj, x2d, lw["w_p_conv"], lw["b_p_conv"], lw["w_p_delta"], lw["w_p_mem"],
                   lw["w_o"], gpm, _tile(rows, 256))
        return _ffn(h, gff1, lw["w_gate"], lw["w_up"], lw["w_down"], gff2, _tile(rows, 512), 512)

    xp = x_prompt.reshape(pb * pt, D_MODEL)
    mkv = _norm_proj(mem_prompt.reshape(pb * N_MEM, D_MODEL), _row(g_mem[l]), w_mem_kv[l].astype(BF16),
                     _tile(pb * N_MEM, 512), 1024)
    proj_p, ba_p = _in_proj(xp, gpre, lw["w_main"], lw["w_small"], _tile(pb * pt, 1024), 1024)
    c_p, conv_p = _conv_prompt(proj_p, lw["w_dw"], lw["b_dw"], lw["ln_g"], lw["ln_b"], pb, pt, _tile(pt, 256))
    d_p, dconv_p, s_p = _delta_prompt(proj_p, ba_p, lw["w_dconv"], lw["arow"], lw["dtrow"], lw["g_dn"], pb, pt)
    m_p = _attn_prompt(proj_p, mkv, pb, pt, _tile(pt, 512))
    y_p = tail(xp, proj_p, c_p, d_p, m_p).reshape(pb, pt, D_MODEL)
    mk = mkv[:, 0:XW].reshape(pb, N_MEM, XA_HEADS, XA_DIM)
    mv = mkv[:, XW:2 * XW].reshape(pb, N_MEM, XA_HEADS, XA_DIM)

    xs = x_sample.reshape(sb * st, D_MODEL)
    proj_s, ba_s = _in_proj(xs, gpre, lw["w_main"], lw["w_small"], _tile(sb * st, 512), 1024)
    c_s, conv_s = _conv_sample(proj_s, state_conv[l], lw["w_dw"], lw["b_dw"], lw["ln_g"], lw["ln_b"], sb, st,
                               _tile(sb, 8))
    d_s, dconv_s, s_s = _delta_sample(proj_s, ba_s, state_delta_conv[l], state_delta[l], lw["w_dconv"],
                                      lw["arow"], lw["dtrow"], lw["g_dn"], sb, st)
    m_s = _attn_sample(proj_s, cache_mem_k[l].reshape(sb, N_MEM, XW), cache_mem_v[l].reshape(sb, N_MEM, XW),
                       sb, st, 4)
    y_s = tail(xs, proj_s, c_s, d_s, m_s).reshape(sb, st, D_MODEL)

    return (y_p, y_s, mk[None], mv[None], conv_p[None], dconv_p[None], s_p[None],
            conv_s[None], dconv_s[None], s_s[None])
```

```python
import functools

import jax
import jax.numpy as jnp
from jax import lax
from jax.experimental import pallas as pl
from jax.experimental.pallas import tpu as pltpu

F32 = jnp.float32
BF16 = jnp.bfloat16

D_MODEL = 2048
N_MEM = 256
C_CONV = 1024
CONV_WIDTH = 31
DN_HEADS = 8
DN_DK = 128
DN_DV = 128
DN_CONV = 4
DN_CHUNK = 64
XA_HEADS = 4
XA_DIM = 256
QKV_W = DN_HEADS * (2 * DN_DK + DN_DV)
XW = XA_HEADS * XA_DIM

COL_GATES = 0
COL_GA, COL_GB = 6, 7
COL_Q, COL_K, COL_V = 8, 9, 10
COL_Z = 11
COL_XQ = 12
N_MAIN = 13 * 1024
SMALL_W = 128

VMEM_LIMIT = 52 * 1024 * 1024

_NT = (((1,), (1,)), ((), ()))
_TN = (((0,), (0,)), ((), ()))


def _dot(a, b):
    return jnp.dot(a, b, preferred_element_type=F32)


def _dot_nt(a, b):
    return lax.dot_general(a, b, _NT, preferred_element_type=F32)


def _dot_tn(a, b):
    return lax.dot_general(a, b, _TN, preferred_element_type=F32)


def _sigmoid(x):
    return jax.nn.sigmoid(x)


def _silu(x):
    return x * jax.nn.sigmoid(x)


def _rms_rows(x, g, eps=1e-6):
    ms = jnp.mean(x * x, axis=-1, keepdims=True)
    return x * lax.rsqrt(ms + eps) * g


def _params(sem):
    return pltpu.CompilerParams(dimension_semantics=sem, vmem_limit_bytes=VMEM_LIMIT)


def _const_spec(shape):
    nd = len(shape)
    return pl.BlockSpec(shape, lambda *_: (0,) * nd, pipeline_mode=pl.Buffered(1))


def _norm_rows_to(x_ref, g_ref, u_ref, rows):
    g = g_ref[...]
    tm = x_ref.shape[0]

    def body(i, c):
        r = pl.multiple_of(i * rows, rows)
        xx = x_ref[pl.ds(r, rows), :]
        u_ref[pl.ds(r, rows), :] = _rms_rows(xx, g).astype(BF16)
        return c

    lax.fori_loop(0, tm // rows, body, 0)


def _in_proj_kernel(x_ref, g_ref, w_ref, ws_ref, o_ref, ba_ref, u_ref):
    @pl.when(pl.program_id(1) == 0)
    def _():
        _norm_rows_to(x_ref, g_ref, u_ref, 64)
        ba_ref[...] = _dot(u_ref[...], ws_ref[...])

    o_ref[...] = _dot(u_ref[...], w_ref[...])


def _norm_proj_kernel(x_ref, g_ref, w_ref, o_ref, u_ref):
    @pl.when(pl.program_id(1) == 0)
    def _():
        _norm_rows_to(x_ref, g_ref, u_ref, 64)

    o_ref[...] = _dot(u_ref[...], w_ref[...])


def _in_proj(x2d, g, w, ws, tm, tn):
    m, k = x2d.shape
    n = w.shape[1]
    return pl.pallas_call(
        _in_proj_kernel,
        grid=(m // tm, n // tn),
        in_specs=[
            pl.BlockSpec((tm, k), lambda i, j: (i, 0)),
            pl.BlockSpec((1, k), lambda i, j: (0, 0)),
            pl.BlockSpec((k, tn), lambda i, j: (0, j)),
            pl.BlockSpec((k, SMALL_W), lambda i, j: (0, 0)),
        ],
        out_specs=[
            pl.BlockSpec((tm, tn), lambda i, j: (i, j)),
            pl.BlockSpec((tm, SMALL_W), lambda i, j: (i, 0)),
        ],
        out_shape=[jax.ShapeDtypeStruct((m, n), F32), jax.ShapeDtypeStruct((m, SMALL_W), F32)],
        scratch_shapes=[pltpu.VMEM((tm, k), BF16)],
        compiler_params=_params(("parallel", "arbitrary")),
        name="in_proj",
    )(x2d, g, w, ws)


def _norm_proj(x2d, g, w, tm, tn):
    m, k = x2d.shape
    n = w.shape[1]
    return pl.pallas_call(
        _norm_proj_kernel,
        grid=(m // tm, n // tn),
        in_specs=[
            pl.BlockSpec((tm, k), lambda i, j: (i, 0)),
            pl.BlockSpec((1, k), lambda i, j: (0, 0)),
            pl.BlockSpec((k, tn), lambda i, j: (0, j)),
        ],
        out_specs=pl.BlockSpec((tm, tn), lambda i, j: (i, j)),
        out_shape=jax.ShapeDtypeStruct((m, n), F32),
        scratch_shapes=[pltpu.VMEM((tm, k), BF16)],
        compiler_params=_params(("parallel", "arbitrary")),
        name="mem_kv_proj",
    )(x2d, g, w)


def _ln_silu(c, g, b, eps=1e-5):
    mu = jnp.mean(c, axis=-1, keepdims=True)
    xc = c - mu
    var = jnp.mean(xc * xc, axis=-1, keepdims=True)
    return _silu(xc * lax.rsqrt(var + eps) * g + b)


CONV_PAD = 32
CONV_ROWS = 32
SUBLANES = 8


def _conv_prompt_kernel(ga_ref, gb_ref, wdw_ref, bdw_ref, lng_ref, lnb_ref, c_ref, st_ref, ext_ref, sh_ref):
    t = pl.program_id(1)
    tt = ga_ref.shape[0]
    hist = CONV_WIDTH - 1
    span = CONV_PAD + tt - SUBLANES

    @pl.when(t == 0)
    def _():
        ext_ref[0:CONV_PAD, :] = jnp.zeros((CONV_PAD, C_CONV), F32)

    ext_ref[CONV_PAD:CONV_PAD + tt, :] = ga_ref[...] * _sigmoid(gb_ref[...])
    for s in range(1, SUBLANES):
        sh_ref[s - 1, 0:span, :] = ext_ref[s:s + span, :]
    bias = bdw_ref[...]
    lng = lng_ref[...]
    lnb = lnb_ref[...]
    for r0 in range(0, tt, CONV_ROWS):
        acc = jnp.broadcast_to(bias, (CONV_ROWS, C_CONV))
        for j in range(CONV_WIDTH):
            off = CONV_PAD - hist + r0 + j
            s = off % SUBLANES
            a = off - s
            rows = ext_ref[a:a + CONV_ROWS, :] if s == 0 else sh_ref[s - 1, a:a + CONV_ROWS, :]
            acc = acc + wdw_ref[j:j + 1, :] * rows
        c_ref[r0:r0 + CONV_ROWS, :] = _ln_silu(acc, lng, lnb).astype(c_ref.dtype)

    @pl.when(t == pl.num_programs(1) - 1)
    def _():
        st_ref[0] = ext_ref[CONV_PAD + tt - hist:CONV_PAD + tt, :]

    ext_ref[0:CONV_PAD, :] = ext_ref[tt:tt + CONV_PAD, :]


def _conv_prompt(proj, wdw, bdw, lng, lnb, nb, t_len, tt):
    nt = t_len // tt
    return pl.pallas_call(
        _conv_prompt_kernel,
        grid=(nb, nt),
        in_specs=[
            pl.BlockSpec((tt, C_CONV), lambda b, t: (b * nt + t, COL_GA)),
            pl.BlockSpec((tt, C_CONV), lambda b, t: (b * nt + t, COL_GB)),
            pl.BlockSpec((CONV_WIDTH, C_CONV), lambda b, t: (0, 0)),
            pl.BlockSpec((1, C_CONV), lambda b, t: (0, 0)),
            pl.BlockSpec((1, C_CONV), lambda b, t: (0, 0)),
            pl.BlockSpec((1, C_CONV), lambda b, t: (0, 0)),
        ],
        out_specs=[
            pl.BlockSpec((tt, C_CONV), lambda b, t: (b * nt + t, 0)),
            pl.BlockSpec((1, CONV_WIDTH - 1, C_CONV), lambda b, t: (b, 0, 0)),
        ],
        out_shape=[
            jax.ShapeDtypeStruct((nb * t_len, C_CONV), BF16),
            jax.ShapeDtypeStruct((nb, CONV_WIDTH - 1, C_CONV), F32),
        ],
        scratch_shapes=[pltpu.VMEM((CONV_PAD + tt, C_CONV), F32),
                        pltpu.VMEM((SUBLANES - 1, CONV_PAD + tt, C_CONV), F32)],
        compiler_params=_params(("parallel", "arbitrary")),
        name="conv_prompt",
    )(proj, proj, wdw, bdw, lng, lnb)


def _conv_sample_kernel(ga_ref, gb_ref, st_ref, wdw_ref, bdw_ref, lng_ref, lnb_ref, c_ref, nst_ref, ext_ref,
                        *, bb, t_len):
    hist = CONV_WIDTH - 1
    cin = ga_ref[...] * _sigmoid(gb_ref[...])
    bias = bdw_ref[...]
    rows = []
    for i in range(bb):
        ext_ref[i, 0:hist, :] = st_ref[i]
        ext_ref[i, hist:hist + t_len, :] = cin[i * t_len:(i + 1) * t_len, :]
        acc = jnp.broadcast_to(bias, (t_len, C_CONV))
        for j in range(CONV_WIDTH):
            acc = acc + wdw_ref[j:j + 1, :] * ext_ref[i, j:j + t_len, :]
        rows.append(acc)
        nst_ref[i] = ext_ref[i, t_len:t_len + hist, :]
    conv = jnp.concatenate(rows, axis=0)
    c_ref[...] = _ln_silu(conv, lng_ref[...], lnb_ref[...]).astype(c_ref.dtype)


def _conv_sample(proj, state, wdw, bdw, lng, lnb, nb, t_len, bb):
    hist = CONV_WIDTH - 1
    rows = bb * t_len
    return pl.pallas_call(
        functools.partial(_conv_sample_kernel, bb=bb, t_len=t_len),
        grid=(nb // bb,),
        in_specs=[
            pl.BlockSpec((rows, C_CONV), lambda i: (i, COL_GA)),
            pl.BlockSpec((rows, C_CONV), lambda i: (i, COL_GB)),
            pl.BlockSpec((bb, hist, C_CONV), lambda i: (i, 0, 0)),
            pl.BlockSpec((CONV_WIDTH, C_CONV), lambda i: (0, 0)),
            pl.BlockSpec((1, C_CONV), lambda i: (0, 0)),
            pl.BlockSpec((1, C_CONV), lambda i: (0, 0)),
            pl.BlockSpec((1, C_CONV), lambda i: (0, 0)),
        ],
        out_specs=[
            pl.BlockSpec((rows, C_CONV), lambda i: (i, 0)),
            pl.BlockSpec((bb, hist, C_CONV), lambda i: (i, 0, 0)),
        ],
        out_shape=[
            jax.ShapeDtypeStruct((nb * t_len, C_CONV), F32),
            jax.ShapeDtypeStruct((nb, hist, C_CONV), F32),
        ],
        scratch_shapes=[pltpu.VMEM((bb, hist + 2 * t_len + 8, C_CONV), F32)],
        compiler_params=_params(("parallel",)),
        name="conv_sample",
    )(proj, proj, state, wdw, bdw, lng, lnb)


DN_PAD = 8


def _split3(x):
    hi = x.astype(BF16)
    r1 = x - hi.astype(F32)
    mid = r1.astype(BF16)
    lo = (r1 - mid.astype(F32)).astype(BF16)
    return hi, mid, lo


def _delta_run(seqs, wdc_ref, arow_ref, dtrow_ref, gdn_ref, *, c, n_valid):
    hist = DN_CONV - 1
    n_steps = 0
    while (1 << (n_steps + 1)) < n_valid:
        n_steps += 1

    row = lax.broadcasted_iota(jnp.int32, (c, SMALL_W), 0)
    valid = row < n_valid
    ii = lax.broadcasted_iota(jnp.int32, (c, c), 0)
    jj = lax.broadcasted_iota(jnp.int32, (c, c), 1)
    causal = jj <= ii
    strict = jj < ii
    tri = jnp.where(causal, 1.0, 0.0).astype(BF16)
    arow = arow_ref[...]
    dtrow = dtrow_ref[...]
    gdn = gdn_ref[...]

    chunks = [ch for seq in seqs for ch in seq]
    for ch in chunks:
        conv = wdc_ref[0:1, :] * ch["win"](DN_PAD - hist, DN_PAD - hist + c)
        for j in range(1, DN_CONV):
            conv = conv + wdc_ref[j:j + 1, :] * ch["win"](DN_PAD - hist + j, DN_PAD - hist + j + c)
        ch["qkv"] = _silu(conv)
        ba = ch["ba"]
        ch["beta"] = jnp.where(valid, _sigmoid(ba), 0.0)
        ch["g3"] = _split3(jnp.where(valid, arow * jax.nn.softplus(ba + dtrow), 0.0))
    for ch in chunks:
        ghi, gmid, glo = ch["g3"]
        gam = _dot(tri, ghi) + _dot(tri, gmid) + _dot(tri, glo)
        ch["gam"] = gam
        ch["gam_t"] = gam.T

    units = []
    for ch in chunks:
        ch["units"] = []
        for h in range(DN_HEADS):
            qkv = ch["qkv"]
            q = qkv[:, h * DN_DK:(h + 1) * DN_DK]
            k = qkv[:, 1024 + h * DN_DK:1024 + (h + 1) * DN_DK]
            v = qkv[:, 2048 + h * DN_DV:2048 + (h + 1) * DN_DV]
            q = q * lax.rsqrt(jnp.sum(q * q, axis=-1, keepdims=True) + 1e-6) * (DN_DK ** -0.5)
            k = k * lax.rsqrt(jnp.sum(k * k, axis=-1, keepdims=True) + 1e-6)
            if n_valid < c:
                k = jnp.where(valid[:, 0:1], k, 0.0)
            beta = ch["beta"][:, h:h + 1]
            gam_c = ch["gam"][:, 8 + h:9 + h]
            gam_r = ch["gam_t"][8 + h:9 + h, :]
            g_last = gam_r[:, c - 1:c]
            kb = k * beta
            u = dict(ch=ch, h=h, q=q, k=k, kb=kb, vb=v * beta, gam_c=gam_c, g_last=g_last,
                     dec=jnp.exp(jnp.where(causal, gam_c - gam_r, -1e30)))
            ch["units"].append(u)
            units.append(u)

    for u in units:
        u["gmat"] = _dot_nt(jnp.concatenate([u["kb"], u["q"]], axis=0).astype(BF16), u["k"].astype(BF16))
    for u in units:
        gmat = u.pop("gmat")
        u["e"] = -jnp.where(strict, gmat[0:c] * u["dec"], 0.0)
        u["p"] = u["e"]
        u["qk16"] = (gmat[c:2 * c] * u.pop("dec")).astype(BF16)
    for _ in range(n_steps):
        for u in units:
            p16 = u["p"].astype(BF16)
            u["p"] = _dot(p16, p16)
        for u in units:
            u["ep"] = _dot(u["e"].astype(BF16), u["p"].astype(BF16))
        for u in units:
            u["e"] = u["e"] + u["p"] + u.pop("ep")
    for u in units:
        eg = jnp.exp(u["gam_c"])
        rhs = jnp.concatenate([u.pop("vb"), u.pop("kb") * eg], axis=1)
        u["rhs"] = rhs
        u["qg"] = u.pop("q") * eg
        u["sol"] = _dot(u.pop("e").astype(BF16), rhs.astype(BF16))
    for u in units:
        sol = u.pop("rhs") + u.pop("sol")
        u["u"] = sol[:, 0:DN_DV]
        u["lhs16"] = jnp.concatenate([sol[:, DN_DV:DN_DV + DN_DK], u.pop("qg")], axis=0).astype(BF16)
        kg = u.pop("k") * jnp.exp(u["g_last"] - u["gam_c"])
        u["kgt16"] = kg.T.astype(BF16)
        u["decay"] = jnp.exp(u["g_last"])

    for j in range(len(seqs[0])):
        cur = [u for seq in seqs for u in seq[j]["units"]]
        for u in cur:
            u["s_old"] = u["ch"]["s"][u["h"]]
            u["xs"] = _dot(u["lhs16"], u["s_old"].astype(BF16))
        for u in cur:
            xs = u.pop("xs")
            u["w16"] = (u["u"] - xs[0:c]).astype(BF16)
            u["o"] = xs[c:2 * c]
        for u in cur:
            u["o2"] = _dot(u["qk16"], u["w16"])
            u["ds"] = _dot(u["kgt16"], u["w16"])
        for u in cur:
            ch, h = u["ch"], u["h"]
            ch["s"][h] = u["s_old"] * u["decay"] + u["ds"]
            zh = ch["z"][:, h * DN_DV:(h + 1) * DN_DV]
            ch["store_o"](h, _rms_rows(u["o"] + u["o2"], gdn) * _silu(zh))


DP_CHUNKS = 2


def _delta_prompt_kernel(q_ref, k_ref, v_ref, z_ref, ba_ref, wdc_ref, arow_ref, dtrow_ref, gdn_ref,
                         o_ref, nd_ref, ns_ref, ext_ref, s_ref):
    t = pl.program_id(1)
    rows = q_ref.shape[0]
    c = DN_CHUNK
    hist = DN_CONV - 1

    @pl.when(t == 0)
    def _():
        ext_ref[0:DN_PAD, :] = jnp.zeros((DN_PAD, QKV_W), F32)
        s_ref[...] = jnp.zeros(s_ref.shape, F32)

    ext_ref[DN_PAD:DN_PAD + rows, 0:1024] = q_ref[...]
    ext_ref[DN_PAD:DN_PAD + rows, 1024:2048] = k_ref[...]
    ext_ref[DN_PAD:DN_PAD + rows, 2048:3072] = v_ref[...]

    seq = []
    for g in range(rows // c):
        def win(j0, j1, g=g):
            return ext_ref[g * c + j0:g * c + j1, :]

        def store_o(h, val, g=g):
            o_ref[g * c:(g + 1) * c, h * DN_DV:(h + 1) * DN_DV] = val.astype(o_ref.dtype)

        seq.append(dict(win=win, z=z_ref[g * c:(g + 1) * c, :], ba=ba_ref[g * c:(g + 1) * c, :], s=s_ref,
                        store_o=store_o))
    _delta_run([seq], wdc_ref, arow_ref, dtrow_ref, gdn_ref, c=c, n_valid=c)

    @pl.when(t == pl.num_programs(1) - 1)
    def _():
        nd_ref[0] = ext_ref[DN_PAD + rows - hist:DN_PAD + rows, :]
        ns_ref[0] = s_ref[...]

    ext_ref[DN_PAD - hist:DN_PAD, :] = ext_ref[DN_PAD + rows - hist:DN_PAD + rows, :]


def _delta_prompt(proj, ba, wdc, arow, dtrow, gdn, nb, t_len):
    rows = DN_CHUNK * (DP_CHUNKS if t_len % (DN_CHUNK * DP_CHUNKS) == 0 else 1)
    nt = t_len // rows
    row_spec = lambda col: pl.BlockSpec((rows, 1024), lambda b, t: (b * nt + t, col))
    small = lambda shape: pl.BlockSpec(shape, lambda b, t: (0,) * len(shape))
    return pl.pallas_call(
        _delta_prompt_kernel,
        grid=(nb, nt),
        in_specs=[
            row_spec(COL_Q), row_spec(COL_K), row_spec(COL_V), row_spec(COL_Z),
            pl.BlockSpec((rows, SMALL_W), lambda b, t: (b * nt + t, 0)),
            small((DN_CONV, QKV_W)), small((1, SMALL_W)), small((1, SMALL_W)), small((1, DN_DV)),
        ],
        out_specs=[
            pl.BlockSpec((rows, 1024), lambda b, t: (b * nt + t, 0)),
            pl.BlockSpec((1, DN_CONV - 1, QKV_W), lambda b, t: (b, 0, 0)),
            pl.BlockSpec((1, DN_HEADS, DN_DK, DN_DV), lambda b, t: (b, 0, 0, 0)),
        ],
        out_shape=[
            jax.ShapeDtypeStruct((nb * t_len, 1024), BF16),
            jax.ShapeDtypeStruct((nb, DN_CONV - 1, QKV_W), F32),
            jax.ShapeDtypeStruct((nb, DN_HEADS, DN_DK, DN_DV), F32),
        ],
        scratch_shapes=[pltpu.VMEM((DN_PAD + rows, QKV_W), F32), pltpu.VMEM((DN_HEADS, DN_DK, DN_DV), F32)],
        compiler_params=_params(("parallel", "arbitrary")),
        name="delta_prompt",
    )(proj, proj, proj, proj, ba, wdc, arow, dtrow, gdn)


DS_ROWS = 8
DS_GROUPS = 2


def _delta_sample_kernel(q_ref, k_ref, v_ref, z_ref, ba_ref, dst_ref, s0_ref, wdc_ref, arow_ref, dtrow_ref,
                         gdn_ref, o_ref, nd_ref, ns_ref, ext_ref, *, t_len):
    hist = DN_CONV - 1
    per = DS_ROWS // t_len
    ns_ref[...] = s0_ref[...]
    seqs = []
    for i in range(dst_ref.shape[0]):
        g0 = (i // per) * DS_ROWS
        r0 = (i % per) * t_len
        ext_ref[i] = jnp.zeros((DN_PAD + DS_ROWS, QKV_W), F32)
        ext_ref[i, DN_PAD - hist:DN_PAD, :] = dst_ref[i]
        ext_ref[i, DN_PAD:DN_PAD + t_len, 0:1024] = q_ref[g0 + r0:g0 + r0 + t_len, :]
        ext_ref[i, DN_PAD:DN_PAD + t_len, 1024:2048] = k_ref[g0 + r0:g0 + r0 + t_len, :]
        ext_ref[i, DN_PAD:DN_PAD + t_len, 2048:3072] = v_ref[g0 + r0:g0 + r0 + t_len, :]
        nd_ref[i] = ext_ref[i, DN_PAD + t_len - hist:DN_PAD + t_len, :]
        z = z_ref[g0:g0 + DS_ROWS, :]
        ba = ba_ref[g0:g0 + DS_ROWS, :]
        if r0:
            z = pltpu.roll(z, DS_ROWS - r0, 0)
            ba = pltpu.roll(ba, DS_ROWS - r0, 0)

        def win(j0, j1, i=i):
            return ext_ref[i, j0:j1, :]

        def store_o(h, val, g0=g0, r0=r0):
            o_ref[g0 + r0:g0 + r0 + t_len, h * DN_DV:(h + 1) * DN_DV] = val[0:t_len].astype(o_ref.dtype)

        seqs.append([dict(win=win, z=z, ba=ba, s=ns_ref.at[i], store_o=store_o)])
    _delta_run(seqs, wdc_ref, arow_ref, dtrow_ref, gdn_ref, c=DS_ROWS, n_valid=t_len)


def _delta_sample(proj, ba, dstate, s0, wdc, arow, dtrow, gdn, nb, t_len):
    per = DS_ROWS // t_len
    groups = DS_GROUPS if nb % (per * DS_GROUPS) == 0 else 1
    bb = per * groups
    rows = DS_ROWS * groups
    hist = DN_CONV - 1
    row_spec = lambda col: pl.BlockSpec((rows, 1024), lambda i: (i, col))
    small = lambda shape: pl.BlockSpec(shape, lambda i: (0,) * len(shape))
    return pl.pallas_call(
        functools.partial(_delta_sample_kernel, t_len=t_len),
        grid=(nb // bb,),
        in_specs=[
            row_spec(COL_Q), row_spec(COL_K), row_spec(COL_V), row_spec(COL_Z),
            pl.BlockSpec((rows, SMALL_W), lambda i: (i, 0)),
            pl.BlockSpec((bb, hist, QKV_W), lambda i: (i, 0, 0)),
            pl.BlockSpec((bb, DN_HEADS, DN_DK, DN_DV), lambda i: (i, 0, 0, 0)),
            small((DN_CONV, QKV_W)), small((1, SMALL_W)), small((1, SMALL_W)), small((1, DN_DV)),
        ],
        out_specs=[
            pl.BlockSpec((rows, 1024), lambda i: (i, 0)),
            pl.BlockSpec((bb, hist, QKV_W), lambda i: (i, 0, 0)),
            pl.BlockSpec((bb, DN_HEADS, DN_DK, DN_DV), lambda i: (i, 0, 0, 0)),
        ],
        out_shape=[
            jax.ShapeDtypeStruct((nb * t_len, 1024), F32),
            jax.ShapeDtypeStruct((nb, hist, QKV_W), F32),
            jax.ShapeDtypeStruct((nb, DN_HEADS, DN_DK, DN_DV), F32),
        ],
        scratch_shapes=[pltpu.VMEM((bb, DN_PAD + DS_ROWS, QKV_W), F32)],
        compiler_params=_params(("parallel",)),
        name="delta_sample",
    )(proj, proj, proj, proj, ba, dstate, s0, wdc, arow, dtrow, gdn)


def _softmax_rows(s):
    m = jnp.max(s, axis=-1, keepdims=True)
    p = jnp.exp(s - m)
    return p / jnp.sum(p, axis=-1, keepdims=True)


def _attn_pairs(pairs):
    scores = [_dot_nt(q16, k.astype(BF16)) * (XA_DIM ** -0.5) for q16, k, _ in pairs]
    probs = [_softmax_rows(s).astype(BF16) for s in scores]
    return [_dot(p, v.astype(BF16)) for p, (_, _, v) in zip(probs, pairs)]


def _attn_prompt_kernel(xq_ref, k_ref, v_ref, o_ref):
    sls = [slice(h * XA_DIM, (h + 1) * XA_DIM) for h in range(XA_HEADS)]
    outs = _attn_pairs([(xq_ref[:, sl].astype(BF16), k_ref[:, sl], v_ref[:, sl]) for sl in sls])
    for sl, o in zip(sls, outs):
        o_ref[:, sl] = o.astype(o_ref.dtype)


def _attn_prompt(proj, mkv, nb, t_len, tq):
    nt = t_len // tq
    return pl.pallas_call(
        _attn_prompt_kernel,
        grid=(nb, nt),
        in_specs=[
            pl.BlockSpec((tq, XW), lambda b, t: (b * nt + t, COL_XQ)),
            pl.BlockSpec((N_MEM, XW), lambda b, t: (b, 0)),
            pl.BlockSpec((N_MEM, XW), lambda b, t: (b, 1)),
        ],
        out_specs=pl.BlockSpec((tq, XW), lambda b, t: (b * nt + t, 0)),
        out_shape=jax.ShapeDtypeStruct((nb * t_len, XW), BF16),
        compiler_params=_params(("parallel", "arbitrary")),
        name="attn_prompt",
    )(proj, mkv, mkv)


def _attn_sample_kernel(xq_ref, k_hbm, v_hbm, o_ref, kbuf, vbuf, sem, *, layer, bb, t_len):
    i = pl.program_id(0)
    slot = i % 2

    def copies(step, slot_):
        out = []
        for h in range(XA_HEADS):
            for kv, (hbm, buf) in enumerate(((k_hbm, kbuf), (v_hbm, vbuf))):
                out.append(pltpu.make_async_copy(hbm.at[layer, pl.ds(step * bb, bb), :, h, :], buf.at[slot_, h],
                                                 sem.at[kv, slot_, h]))
        return out

    @pl.when(i == 0)
    def _():
        for cp in copies(0, 0):
            cp.start()

    @pl.when(i + 1 < pl.num_programs(0))
    def _():
        for cp in copies(i + 1, 1 - slot):
            cp.start()

    for cp in copies(i, slot):
        cp.wait()

    per = DS_ROWS // t_len
    rows = lax.broadcasted_iota(jnp.int32, (DS_ROWS, XA_DIM), 0)
    pairs = []
    for b in range(bb):
        r0 = (b // per) * DS_ROWS
        for h in range(XA_HEADS):
            q16 = xq_ref[r0:r0 + DS_ROWS, h * XA_DIM:(h + 1) * XA_DIM].astype(BF16)
            pairs.append((q16, kbuf[slot, h, b], vbuf[slot, h, b]))
    outs = _attn_pairs(pairs)
    for grp in range(bb // per):
        r0 = grp * DS_ROWS
        for h in range(XA_HEADS):
            o = outs[(grp * per) * XA_HEADS + h]
            for s in range(1, per):
                o = jnp.where(rows < s * t_len, o, outs[(grp * per + s) * XA_HEADS + h])
            o_ref[r0:r0 + DS_ROWS, h * XA_DIM:(h + 1) * XA_DIM] = o.astype(o_ref.dtype)


def _attn_sample(proj, ck, cv, layer, nb, t_len, bb):
    rows = bb * t_len
    buf = pltpu.VMEM((2, XA_HEADS, bb, N_MEM, XA_DIM), F32)
    return pl.pallas_call(
        functools.partial(_attn_sample_kernel, layer=layer, bb=bb, t_len=t_len),
        grid=(nb // bb,),
        in_specs=[pl.BlockSpec((rows, XW), lambda i: (i, COL_XQ)), pl.BlockSpec(memory_space=pl.ANY),
                  pl.BlockSpec(memory_space=pl.ANY)],
        out_specs=pl.BlockSpec((rows, XW), lambda i: (i, 0)),
        out_shape=jax.ShapeDtypeStruct((nb * t_len, XW), F32),
        scratch_shapes=[buf, buf, pltpu.SemaphoreType.DMA((2, 2, XA_HEADS))],
        compiler_params=_params(("arbitrary",)),
        name="attn_sample",
    )(proj, ck, cv)


def _merge_kernel(c_ref, d_ref, m_ref, gc_ref, gd_ref, gx_ref, x_ref, wc_ref, bc_ref, wd_ref, wm_ref, wo_ref,
                  gpm_ref, h_ref):
    br_c = _dot(c_ref[...].astype(BF16), wc_ref[...]) + bc_ref[...]
    mix = _sigmoid(gc_ref[...]) * br_c
    mix = mix + _sigmoid(gd_ref[...]) * _dot(d_ref[...].astype(BF16), wd_ref[...])
    mix = mix + _sigmoid(gx_ref[...]) * _dot(m_ref[...].astype(BF16), wm_ref[...])
    out = _dot(mix.astype(BF16), wo_ref[...])
    h_ref[...] = x_ref[...] + _rms_rows(out, gpm_ref[...])


def _merge(c, d, m, proj, x2d, wc, bc, wd, wm, wo, gpm, tm):
    rows = x2d.shape[0]
    row = lambda width, col=0: pl.BlockSpec((tm, width), lambda i: (i, col))
    return pl.pallas_call(
        _merge_kernel,
        grid=(rows // tm,),
        in_specs=[
            row(1024), row(1024), row(1024),
            row(D_MODEL, 0), row(D_MODEL, 1), row(D_MODEL, 2),
            row(D_MODEL),
            _const_spec((C_CONV, D_MODEL)), _const_spec((1, D_MODEL)),
            _const_spec((1024, D_MODEL)), _const_spec((XW, D_MODEL)),
            _const_spec((D_MODEL, D_MODEL)), _const_spec((1, D_MODEL)),
        ],
        out_specs=row(D_MODEL),
        out_shape=jax.ShapeDtypeStruct((rows, D_MODEL), F32),
        compiler_params=_params(("parallel",)),
        name="merge",
    )(c, d, m, proj, proj, proj, x2d, wc, bc, wd, wm, wo, gpm)


def _ffn_kernel(h_ref, g1_ref, wg_ref, wu_ref, wd_ref, g2_ref, y_ref, fn_ref, acc_ref):
    f = pl.program_id(1)

    @pl.when(f == 0)
    def _():
        _norm_rows_to(h_ref, g1_ref, fn_ref, 64)

    fn = fn_ref[...]
    act = (_silu(_dot(fn, wg_ref[...])) * _dot(fn, wu_ref[...])).astype(BF16)
    part = _dot(act, wd_ref[...])

    @pl.when(f == 0)
    def _():
        acc_ref[...] = part

    @pl.when(f > 0)
    def _():
        acc_ref[...] += part

    @pl.when(f == pl.num_programs(1) - 1)
    def _():
        y_ref[...] = h_ref[...] + _rms_rows(acc_ref[...], g2_ref[...])


def _ffn(h2d, g1, wg, wu, wd, g2, tm, tf):
    rows = h2d.shape[0]
    d_ff = wg.shape[1]
    return pl.pallas_call(
        _ffn_kernel,
        grid=(rows // tm, d_ff // tf),
        in_specs=[
            pl.BlockSpec((tm, D_MODEL), lambda i, f: (i, 0)),
            pl.BlockSpec((1, D_MODEL), lambda i, f: (0, 0)),
            pl.BlockSpec((D_MODEL, tf), lambda i, f: (0, f)),
            pl.BlockSpec((D_MODEL, tf), lambda i, f: (0, f)),
            pl.BlockSpec((tf, D_MODEL), lambda i, f: (f, 0)),
            pl.BlockSpec((1, D_MODEL), lambda i, f: (0, 0)),
        ],
        out_specs=pl.BlockSpec((tm, D_MODEL), lambda i, f: (i, 0)),
        out_shape=jax.ShapeDtypeStruct((rows, D_MODEL), F32),
        scratch_shapes=[pltpu.VMEM((tm, D_MODEL), BF16), pltpu.VMEM((tm, D_MODEL), F32)],
        compiler_params=_params(("parallel", "arbitrary")),
        name="ffn",
    )(h2d, g1, wg, wu, wd, g2)


def _row(v):
    return v.reshape(1, -1).astype(F32)


def _layer_weights(w_in, w_dw, b_dw, ln_g, ln_b, w_p_conv, b_p_conv, w_dconv, a_log, dt_bias, g_dn_norm,
                   w_p_delta, w_p_mem, w_o, w_gate, w_up, w_down):
    glu0, qkv0, z0 = 0, 2 * C_CONV, 2 * C_CONV + QKV_W
    b0 = z0 + DN_HEADS * DN_DV
    xq0 = b0 + 2 * DN_HEADS
    gates0 = xq0 + XW
    w_main = jnp.concatenate(
        [w_in[:, gates0:gates0 + 3 * D_MODEL], w_in[:, glu0:qkv0], w_in[:, qkv0:z0], w_in[:, z0:b0],
         w_in[:, xq0:gates0]], axis=1).astype(BF16)
    w_small = jnp.pad(w_in[:, b0:xq0], ((0, 0), (0, SMALL_W - 2 * DN_HEADS))).astype(BF16)
    lane_pad = (0, SMALL_W - 2 * DN_HEADS)
    arow = jnp.pad(jnp.concatenate([jnp.zeros((DN_HEADS,), F32), -jnp.exp(a_log.astype(F32))]), lane_pad)
    dtrow = jnp.pad(jnp.concatenate([jnp.zeros((DN_HEADS,), F32), dt_bias.astype(F32)]), lane_pad)
    return dict(
        w_main=w_main, w_small=w_small, w_dw=w_dw, b_dw=_row(b_dw), ln_g=_row(ln_g), ln_b=_row(ln_b),
        w_p_conv=w_p_conv.astype(BF16), b_p_conv=_row(b_p_conv), w_dconv=w_dconv, arow=_row(arow),
        dtrow=_row(dtrow), g_dn=_row(g_dn_norm), w_p_delta=w_p_delta.astype(BF16),
        w_p_mem=w_p_mem.astype(BF16), w_o=w_o.astype(BF16), w_gate=w_gate.astype(BF16),
        w_up=w_up.astype(BF16), w_down=w_down.astype(BF16))


def _tile(n, pref):
    return pref if n % pref == 0 else n


def kernel(x_prompt, x_sample, mem_prompt, cache_mem_k, cache_mem_v, state_conv, state_delta_conv, state_delta, g_pre_mix, w_in, w_dw, b_dw, ln_g, ln_b, w_p_conv, b_p_conv, w_dconv, a_log, dt_bias, g_dn_norm, w_p_delta, g_mem, w_mem_kv, w_p_mem, w_o, g_post_mix, g_pre_ffn, w_gate, w_up, w_down, g_post_ffn):
    depth = w_in.shape[0]
    assert depth == 1, "single-layer stack"
    l = 0
    pb, pt, _ = x_prompt.shape
    sb, st, _ = x_sample.shape
    assert pt % DN_CHUNK == 0 and st <= DS_ROWS and DS_ROWS % st == 0

    lw = _layer_weights(w_in[l], w_dw[l], b_dw[l], ln_g[l], ln_b[l], w_p_conv[l], b_p_conv[l], w_dconv[l],
                        a_log[l], dt_bias[l], g_dn_norm[l], w_p_delta[l], w_p_mem[l], w_o[l], w_gate[l],
                        w_up[l], w_down[l])
    gpre, gpm, gff1, gff2 = _row(g_pre_mix[l]), _row(g_post_mix[l]), _row(g_pre_ffn[l]), _row(g_post_ffn[l])

    def tail(x2d, proj, c, d, m):
        rows = x2d.shape[0]
        h = _merge(c, d, m, proj, x2d, lw["w_p_conv"], lw["b_p_conv"], lw["w_p_delta"], lw["w_p_mem"],
                   lw["w_o"], gpm, _tile(rows, 256))
        return _ffn(h, gff1, lw["w_gate"], lw["w_up"], lw["w_down"], gff2, _tile(rows, 512), 512)

    xp = x_prompt.reshape(pb * pt, D_MODEL)
    mkv = _norm_proj(mem_prompt.reshape(pb * N_MEM, D_MODEL), _row(g_mem[l]), w_mem_kv[l].astype(BF16),
                     _tile(pb * N_MEM, 512), 1024)
    proj_p, ba_p = _in_proj(xp, gpre, lw["w_main"], lw["w_small"], _tile(pb * pt, 1024), 1024)
    c_p, conv_p = _conv_prompt(proj_p, lw["w_dw"], lw["b_dw"], lw["ln_g"], lw["ln_b"], pb, pt, _tile(pt, 256))
    d_p, dconv_p, s_p = _delta_prompt(proj_p, ba_p, lw["w_dconv"], lw["arow"], lw["dtrow"], lw["g_dn"], pb, pt)
    m_p = _attn_prompt(proj_p, mkv, pb, pt, _tile(pt, 512))
    y_p = tail(xp, proj_p, c_p, d_p, m_p).reshape(pb, pt, D_MODEL)
    mk = mkv[:, 0:XW].reshape(pb, N_MEM, XA_HEADS, XA_DIM)
    mv = mkv[:, XW:2 * XW].reshape(pb, N_MEM, XA_HEADS, XA_DIM)

    xs = x_sample.reshape(sb * st, D_MODEL)
    proj_s, ba_s = _in_proj(xs, gpre, lw["w_main"], lw["w_small"], _tile(sb * st, 512), 1024)
    c_s, conv_s = _conv_sample(proj_s, state_conv[l], lw["w_dw"], lw["b_dw"], lw["ln_g"], lw["ln_b"], sb, st,
                               _tile(sb, 8))
    d_s, dconv_s, s_s = _delta_sample(proj_s, ba_s, state_delta_conv[l], state_delta[l], lw["w_dconv"],
                                      lw["arow"], lw["dtrow"], lw["g_dn"], sb, st)
    m_s = _attn_sample(proj_s, cache_mem_k, cache_mem_v, l, sb, st, 4)
    y_s = tail(xs, proj_s, c_s, d_s, m_s).reshape(sb, st, D_MODEL)

    return (y_p, y_s, mk[None], mv[None], conv_p[None], dconv_p[None], s_p[None],
            conv_s[None], dconv_s[None], s_s[None])
```

```python
import functools

import jax
import jax.numpy as jnp
from jax import lax
from jax.experimental import pallas as pl
from jax.experimental.pallas import tpu as pltpu

F32 = jnp.float32
BF16 = jnp.bfloat16

D_MODEL = 2048
N_MEM = 256
C_CONV = 1024
CONV_WIDTH = 31
DN_HEADS = 8
DN_DK = 128
DN_DV = 128
DN_CONV = 4
DN_CHUNK = 64
XA_HEADS = 4
XA_DIM = 256
QKV_W = DN_HEADS * (2 * DN_DK + DN_DV)
XW = XA_HEADS * XA_DIM

COL_GA, COL_GB = 6, 7
COL_Q, COL_K, COL_V = 8, 9, 10
COL_Z = 11
COL_XQ = 12
N_MAIN = 13 * 1024
SMALL_W = 128

VMEM_LIMIT = 52 * 1024 * 1024

_NT = (((1,), (1,)), ((), ()))


def _dot(a, b):
    return jnp.dot(a, b, preferred_element_type=F32)


def _dot_nt(a, b):
    return lax.dot_general(a, b, _NT, preferred_element_type=F32)


def _sigmoid(x):
    return jax.nn.sigmoid(x)


def _silu(x):
    return x * jax.nn.sigmoid(x)


def _rms_rows(x, g, eps=1e-6):
    ms = jnp.mean(x * x, axis=-1, keepdims=True)
    return x * lax.rsqrt(ms + eps) * g


def _params(sem):
    return pltpu.CompilerParams(dimension_semantics=sem, vmem_limit_bytes=VMEM_LIMIT)


def _const_spec(shape):
    nd = len(shape)
    return pl.BlockSpec(shape, lambda *_: (0,) * nd, pipeline_mode=pl.Buffered(1))


def _norm_rows_to(x_ref, g_ref, u_ref, rows):
    g = g_ref[...]
    tm = x_ref.shape[0]

    def body(i, c):
        r = pl.multiple_of(i * rows, rows)
        xx = x_ref[pl.ds(r, rows), :]
        u_ref[pl.ds(r, rows), :] = _rms_rows(xx, g).astype(BF16)
        return c

    lax.fori_loop(0, tm // rows, body, 0)


def _norm_cast_kernel(x_ref, g_ref, u_ref):
    _norm_rows_to(x_ref, g_ref, u_ref, 64)


def _norm_cast(x2d, g, tm):
    m, k = x2d.shape
    return pl.pallas_call(
        _norm_cast_kernel,
        grid=(m // tm,),
        in_specs=[pl.BlockSpec((tm, k), lambda i: (i, 0)), pl.BlockSpec((1, k), lambda i: (0, 0))],
        out_specs=pl.BlockSpec((tm, k), lambda i: (i, 0)),
        out_shape=jax.ShapeDtypeStruct((m, k), BF16),
        compiler_params=_params(("parallel",)),
        name="norm_cast",
    )(x2d, g)


def _proj_wt_kernel(u_ref, w_ref, o_ref, wb_ref):
    @pl.when(pl.program_id(1) == 0)
    def _():
        rows = 64

        def body(i, c):
            r = pl.multiple_of(i * rows, rows)
            wb_ref[pl.ds(r, rows), :] = w_ref[pl.ds(r, rows), :].astype(BF16)
            return c

        lax.fori_loop(0, w_ref.shape[0] // rows, body, 0)

    o_ref[...] = _dot_nt(u_ref[...], wb_ref[...])


def _proj_wt(u, w_t, row_offset, n_blocks, tn, tm):
    m, k = u.shape
    return pl.pallas_call(
        _proj_wt_kernel,
        grid=(n_blocks, m // tm),
        in_specs=[
            pl.BlockSpec((tm, k), lambda j, i: (i, 0)),
            pl.BlockSpec((pl.Element(tn), pl.Element(k)), lambda j, i: (row_offset(j), 0)),
        ],
        out_specs=pl.BlockSpec((tm, tn), lambda j, i: (i, j)),
        out_shape=jax.ShapeDtypeStruct((m, n_blocks * tn), F32),
        scratch_shapes=[pltpu.VMEM((tn, k), BF16)],
        compiler_params=_params(("parallel", "arbitrary")),
        name="in_proj",
    )(u, w_t)


IN_GLU0 = 0
IN_B0 = 2 * C_CONV + QKV_W + DN_HEADS * DN_DV
IN_XQ0 = IN_B0 + 2 * DN_HEADS
IN_GATES0 = IN_XQ0 + XW


ROW_ALIGN = 16


def _main_row_offset(j):
    unit = 1024 // ROW_ALIGN
    units = jnp.where(j < 6, IN_GATES0 // ROW_ALIGN + unit * j,
                      jnp.where(j < 12, IN_GLU0 // ROW_ALIGN + unit * (j - 6), IN_XQ0 // ROW_ALIGN))
    return units * ROW_ALIGN


def _in_proj(x2d, g, w_t, tm):
    u = _norm_cast(x2d, g, _tile(x2d.shape[0], 512))
    proj = _proj_wt(u, w_t, _main_row_offset, N_MAIN // 1024, 1024, tm)
    ba = _proj_wt(u, w_t, lambda j: j * 0 + IN_B0, 1, SMALL_W, tm)
    return proj, ba


def _norm_proj_kernel(x_ref, g_ref, w_ref, o_ref, u_ref):
    @pl.when(pl.program_id(1) == 0)
    def _():
        _norm_rows_to(x_ref, g_ref, u_ref, 64)

    o_ref[...] = _dot(u_ref[...], w_ref[...])


def _norm_proj(x2d, g, w, tm, tn):
    m, k = x2d.shape
    n = w.shape[1]
    return pl.pallas_call(
        _norm_proj_kernel,
        grid=(m // tm, n // tn),
        in_specs=[
            pl.BlockSpec((tm, k), lambda i, j: (i, 0)),
            pl.BlockSpec((1, k), lambda i, j: (0, 0)),
            pl.BlockSpec((k, tn), lambda i, j: (0, j)),
        ],
        out_specs=pl.BlockSpec((tm, tn), lambda i, j: (i, j)),
        out_shape=jax.ShapeDtypeStruct((m, n), F32),
        scratch_shapes=[pltpu.VMEM((tm, k), BF16)],
        compiler_params=_params(("parallel", "arbitrary")),
        name="mem_kv_proj",
    )(x2d, g, w)


def _ln_silu(c, g, b, eps=1e-5):
    mu = jnp.mean(c, axis=-1, keepdims=True)
    xc = c - mu
    var = jnp.mean(xc * xc, axis=-1, keepdims=True)
    return _silu(xc * lax.rsqrt(var + eps) * g + b)


CONV_PAD = 32
CONV_ROWS = 32
LANES = 128


def _conv_prompt_kernel(ga_ref, gb_ref, wdw_ref, bdw_ref, lng_ref, lnb_ref, c_ref, st_ref, ext_ref, conv_ref):
    t = pl.program_id(1)
    tt = ga_ref.shape[0]
    hist = CONV_WIDTH - 1
    slabs = [slice(s * LANES, (s + 1) * LANES) for s in range(C_CONV // LANES)]

    @pl.when(t == 0)
    def _():
        ext_ref[:, 0:CONV_PAD, :] = jnp.zeros((len(slabs), CONV_PAD, LANES), F32)

    cin = ga_ref[...] * _sigmoid(gb_ref[...])
    for s, sl in enumerate(slabs):
        ext_ref[s, CONV_PAD:CONV_PAD + tt, :] = cin[:, sl]
    blocks = range(0, tt, CONV_ROWS)
    for s, sl in enumerate(slabs):
        w_all = wdw_ref[:, sl]
        accs = [jnp.broadcast_to(bdw_ref[:, sl], (CONV_ROWS, LANES)) for _ in blocks]
        for j in range(CONV_WIDTH):
            w = jnp.broadcast_to(w_all[j:j + 1, :], (CONV_ROWS, LANES))
            for i, r0 in enumerate(blocks):
                off = CONV_PAD - hist + r0 + j
                accs[i] = accs[i] + w * ext_ref[s, off:off + CONV_ROWS, :]
        for i, r0 in enumerate(blocks):
            conv_ref[r0:r0 + CONV_ROWS, sl] = accs[i]
    lng = lng_ref[...]
    lnb = lnb_ref[...]
    for r0 in blocks:
        c_ref[r0:r0 + CONV_ROWS, :] = _ln_silu(conv_ref[r0:r0 + CONV_ROWS, :], lng, lnb).astype(c_ref.dtype)

    @pl.when(t == pl.num_programs(1) - 1)
    def _():
        for s, sl in enumerate(slabs):
            st_ref[0, :, sl] = ext_ref[s, CONV_PAD + tt - hist:CONV_PAD + tt, :]

    ext_ref[:, 0:CONV_PAD, :] = ext_ref[:, tt:tt + CONV_PAD, :]


def _conv_prompt(proj, wdw, bdw, lng, lnb, nb, t_len, tt):
    nt = t_len // tt
    return pl.pallas_call(
        _conv_prompt_kernel,
        grid=(nb, nt),
        in_specs=[
            pl.BlockSpec((tt, C_CONV), lambda b, t: (b * nt + t, COL_GA)),
            pl.BlockSpec((tt, C_CONV), lambda b, t: (b * nt + t, COL_GB)),
            pl.BlockSpec((CONV_WIDTH, C_CONV), lambda b, t: (0, 0)),
            pl.BlockSpec((1, C_CONV), lambda b, t: (0, 0)),
            pl.BlockSpec((1, C_CONV), lambda b, t: (0, 0)),
            pl.BlockSpec((1, C_CONV), lambda b, t: (0, 0)),
        ],
        out_specs=[
            pl.BlockSpec((tt, C_CONV), lambda b, t: (b * nt + t, 0)),
            pl.BlockSpec((1, CONV_WIDTH - 1, C_CONV), lambda b, t: (b, 0, 0)),
        ],
        out_shape=[
            jax.ShapeDtypeStruct((nb * t_len, C_CONV), BF16),
            jax.ShapeDtypeStruct((nb, CONV_WIDTH - 1, C_CONV), F32),
        ],
        scratch_shapes=[pltpu.VMEM((C_CONV // LANES, CONV_PAD + tt, LANES), F32), pltpu.VMEM((tt, C_CONV), F32)],
        compiler_params=_params(("parallel", "arbitrary")),
        name="conv_prompt",
    )(proj, proj, wdw, bdw, lng, lnb)


def _conv_sample_kernel(ga_ref, gb_ref, st_ref, wdw_ref, bdw_ref, lng_ref, lnb_ref, c_ref, nst_ref, ext_ref,
                        *, bb, t_len):
    hist = CONV_WIDTH - 1
    cin = ga_ref[...] * _sigmoid(gb_ref[...])
    bias = bdw_ref[...]
    rows = []
    for i in range(bb):
        ext_ref[i, 0:hist, :] = st_ref[i]
        ext_ref[i, hist:hist + t_len, :] = cin[i * t_len:(i + 1) * t_len, :]
        acc = jnp.broadcast_to(bias, (t_len, C_CONV))
        for j in range(CONV_WIDTH):
            acc = acc + wdw_ref[j:j + 1, :] * ext_ref[i, j:j + t_len, :]
        rows.append(acc)
        nst_ref[i] = ext_ref[i, t_len:t_len + hist, :]
    conv = jnp.concatenate(rows, axis=0)
    c_ref[...] = _ln_silu(conv, lng_ref[...], lnb_ref[...]).astype(c_ref.dtype)


def _conv_sample(proj, state, wdw, bdw, lng, lnb, nb, t_len, bb):
    hist = CONV_WIDTH - 1
    rows = bb * t_len
    return pl.pallas_call(
        functools.partial(_conv_sample_kernel, bb=bb, t_len=t_len),
        grid=(nb // bb,),
        in_specs=[
            pl.BlockSpec((rows, C_CONV), lambda i: (i, COL_GA)),
            pl.BlockSpec((rows, C_CONV), lambda i: (i, COL_GB)),
            pl.BlockSpec((bb, hist, C_CONV), lambda i: (i, 0, 0)),
            pl.BlockSpec((CONV_WIDTH, C_CONV), lambda i: (0, 0)),
            pl.BlockSpec((1, C_CONV), lambda i: (0, 0)),
            pl.BlockSpec((1, C_CONV), lambda i: (0, 0)),
            pl.BlockSpec((1, C_CONV), lambda i: (0, 0)),
        ],
        out_specs=[
            pl.BlockSpec((rows, C_CONV), lambda i: (i, 0)),
            pl.BlockSpec((bb, hist, C_CONV), lambda i: (i, 0, 0)),
        ],
        out_shape=[
            jax.ShapeDtypeStruct((nb * t_len, C_CONV), F32),
            jax.ShapeDtypeStruct((nb, hist, C_CONV), F32),
        ],
        scratch_shapes=[pltpu.VMEM((bb, hist + 2 * t_len + 8, C_CONV), F32)],
        compiler_params=_params(("parallel",)),
        name="conv_sample",
    )(proj, proj, state, wdw, bdw, lng, lnb)


DN_PAD = 8


def _split3(x):
    hi = x.astype(BF16)
    r1 = x - hi.astype(F32)
    mid = r1.astype(BF16)
    lo = (r1 - mid.astype(F32)).astype(BF16)
    return hi, mid, lo


def _delta_run(seqs, wdc_ref, arow_ref, dtrow_ref, gdn_ref, *, c, n_valid):
    hist = DN_CONV - 1
    n_steps = 0
    while (1 << (n_steps + 1)) < n_valid:
        n_steps += 1

    row = lax.broadcasted_iota(jnp.int32, (c, SMALL_W), 0)
    valid = row < n_valid
    ii = lax.broadcasted_iota(jnp.int32, (c, c), 0)
    jj = lax.broadcasted_iota(jnp.int32, (c, c), 1)
    causal_f = jnp.where(jj <= ii, 1.0, 0.0)
    strict_f = jnp.where(jj < ii, 1.0, 0.0)
    above = (causal_f - 1.0) * 1e30
    tri = causal_f.astype(BF16)
    ones = jnp.ones((DN_DK, DN_DK), BF16)
    arow = arow_ref[...]
    dtrow = dtrow_ref[...]
    gdn = gdn_ref[...]

    def conv_slab(ch, s):
        sl = slice(s * LANES, (s + 1) * LANES)
        conv = wdc_ref[0:1, sl] * ch["win"](s, DN_PAD - hist, DN_PAD - hist + c)
        for j in range(1, DN_CONV):
            conv = conv + wdc_ref[j:j + 1, sl] * ch["win"](s, DN_PAD - hist + j, DN_PAD - hist + j + c)
        return _silu(conv)

    def unit_norm(x):
        sq = x * x
        hi = sq.astype(BF16)
        lo = (sq - hi.astype(F32)).astype(BF16)
        return x * lax.rsqrt(_dot(hi, ones) + _dot(lo, ones) + 1e-6)

    chunks = [ch for seq in seqs for ch in seq]
    for ch in chunks:
        ba = ch["ba"]
        beta = _sigmoid(ba)
        g = arow * jax.nn.softplus(ba + dtrow)
        if n_valid < c:
            beta = jnp.where(valid, beta, 0.0)
            g = jnp.where(valid, g, 0.0)
        ch["beta"] = beta
        ch["g3"] = _split3(g)
    for ch in chunks:
        ghi, gmid, glo = ch["g3"]
        gam = _dot(tri, ghi) + _dot(tri, gmid) + _dot(tri, glo)
        ch["gam"] = gam
        ch["gam_t"] = gam.T

    units = []
    heads = DN_HEADS
    for ch in chunks:
        ch["units"] = []
        for h in range(heads):
            q = unit_norm(conv_slab(ch, h)) * (DN_DK ** -0.5)
            k = unit_norm(conv_slab(ch, heads + h))
            v = conv_slab(ch, 2 * heads + h)
            if n_valid < c:
                k = jnp.where(valid[:, 0:1], k, 0.0)
            beta = ch["beta"][:, h:h + 1]
            gam_c = ch["gam"][:, 8 + h:9 + h]
            gam_r = ch["gam_t"][8 + h:9 + h, :]
            g_last = gam_r[:, c - 1:c]
            kb = k * beta
            u = dict(ch=ch, h=h, q=q, k=k, kb=kb, vb=v * beta, gam_c=gam_c, g_last=g_last,
                     dec=jnp.exp((gam_c - gam_r) * causal_f + above))
            ch["units"].append(u)
            units.append(u)

    for u in units:
        u["gmat"] = _dot_nt(jnp.concatenate([u["kb"], u["q"]], axis=0).astype(BF16), u["k"].astype(BF16))
    for u in units:
        gmat = u.pop("gmat")
        u["e"] = -(gmat[0:c] * (u["dec"] * strict_f))
        u["p"] = u["e"]
        u["qk16"] = (gmat[c:2 * c] * u.pop("dec")).astype(BF16)
    for _ in range(n_steps):
        for u in units:
            p16 = u["p"].astype(BF16)
            u["p"] = _dot(p16, p16)
        for u in units:
            u["ep"] = _dot(u["e"].astype(BF16), u["p"].astype(BF16))
        for u in units:
            u["e"] = u["e"] + u["p"] + u.pop("ep")
    for u in units:
        eg = jnp.exp(u["gam_c"])
        rhs = jnp.concatenate([u.pop("vb"), u.pop("kb") * eg], axis=1)
        u["rhs"] = rhs
        u["qg"] = u.pop("q") * eg
        u["sol"] = _dot(u.pop("e").astype(BF16), rhs.astype(BF16))
    for u in units:
        sol = u.pop("rhs") + u.pop("sol")
        u["u"] = sol[:, 0:DN_DV]
        u["lhs16"] = jnp.concatenate([sol[:, DN_DV:DN_DV + DN_DK], u.pop("qg")], axis=0).astype(BF16)
        kg = u.pop("k") * jnp.exp(u["g_last"] - u["gam_c"])
        u["kgt16"] = kg.T.astype(BF16)
        u["decay"] = jnp.exp(u["g_last"])

    for j in range(len(seqs[0])):
        cur = [u for seq in seqs for u in seq[j]["units"]]
        for u in cur:
            u["s_old"] = u["ch"]["s"][u["h"]]
            u["xs"] = _dot(u["lhs16"], u["s_old"].astype(BF16))
        for u in cur:
            xs = u.pop("xs")
            u["w16"] = (u["u"] - xs[0:c]).astype(BF16)
            u["o"] = xs[c:2 * c]
        for u in cur:
            u["o2"] = _dot(u["qk16"], u["w16"])
            u["ds"] = _dot(u["kgt16"], u["w16"])
        for u in cur:
            ch, h = u["ch"], u["h"]
            ch["s"][h] = u["s_old"] * u["decay"] + u["ds"]
            zh = ch["z"][:, h * DN_DV:(h + 1) * DN_DV]
            ch["store_o"](h, _rms_rows(u["o"] + u["o2"], gdn) * _silu(zh))


DP_CHUNKS = 2


def _delta_prompt_kernel(q_ref, k_ref, v_ref, z_ref, ba_ref, wdc_ref, arow_ref, dtrow_ref, gdn_ref,
                         o_ref, nd_ref, ns_ref, ext_ref, s_ref):
    t = pl.program_id(1)
    rows = q_ref.shape[0]
    c = DN_CHUNK
    hist = DN_CONV - 1

    @pl.when(t == 0)
    def _():
        ext_ref[:, 0:DN_PAD, :] = jnp.zeros((QKV_W // LANES, DN_PAD, LANES), F32)
        s_ref[...] = jnp.zeros(s_ref.shape, F32)

    n_slab = QKV_W // LANES
    for i, ref in enumerate((q_ref, k_ref, v_ref)):
        for s in range(1024 // LANES):
            ext_ref[i * (1024 // LANES) + s, DN_PAD:DN_PAD + rows, :] = ref[:, s * LANES:(s + 1) * LANES]

    seq = []
    for g in range(rows // c):
        def win(s, j0, j1, g=g):
            return ext_ref[s, g * c + j0:g * c + j1, :]

        def store_o(h, val, g=g):
            o_ref[g * c:(g + 1) * c, h * DN_DV:(h + 1) * DN_DV] = val.astype(o_ref.dtype)

        seq.append(dict(win=win, z=z_ref[g * c:(g + 1) * c, :], ba=ba_ref[g * c:(g + 1) * c, :], s=s_ref,
                        store_o=store_o))
    _delta_run([seq], wdc_ref, arow_ref, dtrow_ref, gdn_ref, c=c, n_valid=c)

    @pl.when(t == pl.num_programs(1) - 1)
    def _():
        for s in range(n_slab):
            nd_ref[0, :, s * LANES:(s + 1) * LANES] = ext_ref[s, DN_PAD + rows - hist:DN_PAD + rows, :]
        ns_ref[0] = s_ref[...]

    ext_ref[:, DN_PAD - hist:DN_PAD, :] = ext_ref[:, DN_PAD + rows - hist:DN_PAD + rows, :]


def _delta_prompt(proj, ba, wdc, arow, dtrow, gdn, nb, t_len):
    rows = DN_CHUNK * (DP_CHUNKS if t_len % (DN_CHUNK * DP_CHUNKS) == 0 else 1)
    nt = t_len // rows
    row_spec = lambda col: pl.BlockSpec((rows, 1024), lambda b, t: (b * nt + t, col))
    small = lambda shape: pl.BlockSpec(shape, lambda b, t: (0,) * len(shape))
    return pl.pallas_call(
        _delta_prompt_kernel,
        grid=(nb, nt),
        in_specs=[
            row_spec(COL_Q), row_spec(COL_K), row_spec(COL_V), row_spec(COL_Z),
            pl.BlockSpec((rows, SMALL_W), lambda b, t: (b * nt + t, 0)),
            small((DN_CONV, QKV_W)), small((1, SMALL_W)), small((1, SMALL_W)), small((1, DN_DV)),
        ],
        out_specs=[
            pl.BlockSpec((rows, 1024), lambda b, t: (b * nt + t, 0)),
            pl.BlockSpec((1, DN_CONV - 1, QKV_W), lambda b, t: (b, 0, 0)),
            pl.BlockSpec((1, DN_HEADS, DN_DK, DN_DV), lambda b, t: (b, 0, 0, 0)),
        ],
        out_shape=[
            jax.ShapeDtypeStruct((nb * t_len, 1024), BF16),
            jax.ShapeDtypeStruct((nb, DN_CONV - 1, QKV_W), F32),
            jax.ShapeDtypeStruct((nb, DN_HEADS, DN_DK, DN_DV), F32),
        ],
        scratch_shapes=[pltpu.VMEM((QKV_W // LANES, DN_PAD + rows, LANES), F32),
                        pltpu.VMEM((DN_HEADS, DN_DK, DN_DV), F32)],
        compiler_params=_params(("parallel", "arbitrary")),
        name="delta_prompt",
    )(proj, proj, proj, proj, ba, wdc, arow, dtrow, gdn)


DS_ROWS = 8
DS_GROUPS = 2


def _delta_sample_kernel(q_ref, k_ref, v_ref, z_ref, ba_ref, dst_ref, s0_ref, wdc_ref, arow_ref, dtrow_ref,
                         gdn_ref, o_ref, nd_ref, ns_ref, ext_ref, *, t_len):
    hist = DN_CONV - 1
    per = DS_ROWS // t_len
    n_slab = QKV_W // LANES
    ns_ref[...] = s0_ref[...]
    seqs = []
    for i in range(dst_ref.shape[0]):
        g0 = (i // per) * DS_ROWS
        r0 = (i % per) * t_len
        ext_ref[i] = jnp.zeros((n_slab, DN_PAD + DS_ROWS, LANES), F32)
        for s in range(n_slab):
            sl = slice(s * LANES, (s + 1) * LANES)
            ext_ref[i, s, DN_PAD - hist:DN_PAD, :] = dst_ref[i, :, sl]
            src = (q_ref, k_ref, v_ref)[s // (1024 // LANES)]
            col = (s % (1024 // LANES)) * LANES
            ext_ref[i, s, DN_PAD:DN_PAD + t_len, :] = src[g0 + r0:g0 + r0 + t_len, col:col + LANES]
            nd_ref[i, :, sl] = ext_ref[i, s, DN_PAD + t_len - hist:DN_PAD + t_len, :]
        z = z_ref[g0:g0 + DS_ROWS, :]
        ba = ba_ref[g0:g0 + DS_ROWS, :]
        if r0:
            z = pltpu.roll(z, DS_ROWS - r0, 0)
            ba = pltpu.roll(ba, DS_ROWS - r0, 0)

        def win(s, j0, j1, i=i):
            return ext_ref[i, s, j0:j1, :]

        def store_o(h, val, g0=g0, r0=r0):
            o_ref[g0 + r0:g0 + r0 + t_len, h * DN_DV:(h + 1) * DN_DV] = val[0:t_len].astype(o_ref.dtype)

        seqs.append([dict(win=win, z=z, ba=ba, s=ns_ref.at[i], store_o=store_o)])
    _delta_run(seqs, wdc_ref, arow_ref, dtrow_ref, gdn_ref, c=DS_ROWS, n_valid=t_len)


def _delta_sample(proj, ba, dstate, s0, wdc, arow, dtrow, gdn, nb, t_len):
    per = DS_ROWS // t_len
    groups = DS_GROUPS if nb % (per * DS_GROUPS) == 0 else 1
    bb = per * groups
    rows = DS_ROWS * groups
    hist = DN_CONV - 1
    row_spec = lambda col: pl.BlockSpec((rows, 1024), lambda i: (i, col))
    small = lambda shape: pl.BlockSpec(shape, lambda i: (0,) * len(shape))
    return pl.pallas_call(
        functools.partial(_delta_sample_kernel, t_len=t_len),
        grid=(nb // bb,),
        in_specs=[
            row_spec(COL_Q), row_spec(COL_K), row_spec(COL_V), row_spec(COL_Z),
            pl.BlockSpec((rows, SMALL_W), lambda i: (i, 0)),
            pl.BlockSpec((bb, hist, QKV_W), lambda i: (i, 0, 0)),
            pl.BlockSpec((bb, DN_HEADS, DN_DK, DN_DV), lambda i: (i, 0, 0, 0)),
            small((DN_CONV, QKV_W)), small((1, SMALL_W)), small((1, SMALL_W)), small((1, DN_DV)),
        ],
        out_specs=[
            pl.BlockSpec((rows, 1024), lambda i: (i, 0)),
            pl.BlockSpec((bb, hist, QKV_W), lambda i: (i, 0, 0)),
            pl.BlockSpec((bb, DN_HEADS, DN_DK, DN_DV), lambda i: (i, 0, 0, 0)),
        ],
        out_shape=[
            jax.ShapeDtypeStruct((nb * t_len, 1024), F32),
            jax.ShapeDtypeStruct((nb, hist, QKV_W), F32),
            jax.ShapeDtypeStruct((nb, DN_HEADS, DN_DK, DN_DV), F32),
        ],
        scratch_shapes=[pltpu.VMEM((bb, QKV_W // LANES, DN_PAD + DS_ROWS, LANES), F32)],
        compiler_params=_params(("parallel",)),
        name="delta_sample",
    )(proj, proj, proj, proj, ba, dstate, s0, wdc, arow, dtrow, gdn)


def _softmax_rows(s):
    m = jnp.max(s, axis=-1, keepdims=True)
    p = jnp.exp(s - m)
    return p / jnp.sum(p, axis=-1, keepdims=True)


def _attn_pairs(pairs):
    scores = [_dot_nt(q16, k.astype(BF16)) * (XA_DIM ** -0.5) for q16, k, _ in pairs]
    probs = [_softmax_rows(s).astype(BF16) for s in scores]
    return [_dot(p, v.astype(BF16)) for p, (_, _, v) in zip(probs, pairs)]


def _attn_prompt_kernel(xq_ref, k_ref, v_ref, o_ref):
    sls = [slice(h * XA_DIM, (h + 1) * XA_DIM) for h in range(XA_HEADS)]
    outs = _attn_pairs([(xq_ref[:, sl].astype(BF16), k_ref[:, sl], v_ref[:, sl]) for sl in sls])
    for sl, o in zip(sls, outs):
        o_ref[:, sl] = o.astype(o_ref.dtype)


def _attn_prompt(proj, mkv, nb, t_len, tq):
    nt = t_len // tq
    return pl.pallas_call(
        _attn_prompt_kernel,
        grid=(nb, nt),
        in_specs=[
            pl.BlockSpec((tq, XW), lambda b, t: (b * nt + t, COL_XQ)),
            pl.BlockSpec((N_MEM, XW), lambda b, t: (b, 0)),
            pl.BlockSpec((N_MEM, XW), lambda b, t: (b, 1)),
        ],
        out_specs=pl.BlockSpec((tq, XW), lambda b, t: (b * nt + t, 0)),
        out_shape=jax.ShapeDtypeStruct((nb * t_len, XW), BF16),
        compiler_params=_params(("parallel", "arbitrary")),
        name="attn_prompt",
    )(proj, mkv, mkv)


def _attn_sample_kernel(xq_ref, k_hbm, v_hbm, o_ref, kbuf, vbuf, sem, *, layer, bb, t_len):
    i = pl.program_id(0)
    slot = i % 2

    def copies(step, slot_):
        out = []
        for h in range(XA_HEADS):
            for kv, (hbm, buf) in enumerate(((k_hbm, kbuf), (v_hbm, vbuf))):
                out.append(pltpu.make_async_copy(hbm.at[layer, pl.ds(step * bb, bb), :, h, :], buf.at[slot_, h],
                                                 sem.at[kv, slot_, h]))
        return out

    @pl.when(i == 0)
    def _():
        for cp in copies(0, 0):
            cp.start()

    @pl.when(i + 1 < pl.num_programs(0))
    def _():
        for cp in copies(i + 1, 1 - slot):
            cp.start()

    for cp in copies(i, slot):
        cp.wait()

    per = DS_ROWS // t_len
    rows = lax.broadcasted_iota(jnp.int32, (DS_ROWS, XA_DIM), 0)
    pairs = []
    for b in range(bb):
        r0 = (b // per) * DS_ROWS
        for h in range(XA_HEADS):
            q16 = xq_ref[r0:r0 + DS_ROWS, h * XA_DIM:(h + 1) * XA_DIM].astype(BF16)
            pairs.append((q16, kbuf[slot, h, b], vbuf[slot, h, b]))
    outs = _attn_pairs(pairs)
    for grp in range(bb // per):
        r0 = grp * DS_ROWS
        for h in range(XA_HEADS):
            o = outs[(grp * per) * XA_HEADS + h]
            for s in range(1, per):
                o = jnp.where(rows < s * t_len, o, outs[(grp * per + s) * XA_HEADS + h])
            o_ref[r0:r0 + DS_ROWS, h * XA_DIM:(h + 1) * XA_DIM] = o.astype(o_ref.dtype)


def _attn_sample(proj, ck, cv, layer, nb, t_len, bb):
    rows = bb * t_len
    buf = pltpu.VMEM((2, XA_HEADS, bb, N_MEM, XA_DIM), F32)
    return pl.pallas_call(
        functools.partial(_attn_sample_kernel, layer=layer, bb=bb, t_len=t_len),
        grid=(nb // bb,),
        in_specs=[pl.BlockSpec((rows, XW), lambda i: (i, COL_XQ)), pl.BlockSpec(memory_space=pl.ANY),
                  pl.BlockSpec(memory_space=pl.ANY)],
        out_specs=pl.BlockSpec((rows, XW), lambda i: (i, 0)),
        out_shape=jax.ShapeDtypeStruct((nb * t_len, XW), F32),
        scratch_shapes=[buf, buf, pltpu.SemaphoreType.DMA((2, 2, XA_HEADS))],
        compiler_params=_params(("arbitrary",)),
        name="attn_sample",
    )(proj, ck, cv)


def _merge_kernel(c_ref, d_ref, m_ref, gc_ref, gd_ref, gx_ref, x_ref, wc_ref, bc_ref, wd_ref, wm_ref, wo_ref,
                  gpm_ref, h_ref):
    br_c = _dot(c_ref[...].astype(BF16), wc_ref[...]) + bc_ref[...]
    mix = _sigmoid(gc_ref[...]) * br_c
    mix = mix + _sigmoid(gd_ref[...]) * _dot(d_ref[...].astype(BF16), wd_ref[...])
    mix = mix + _sigmoid(gx_ref[...]) * _dot(m_ref[...].astype(BF16), wm_ref[...])
    out = _dot(mix.astype(BF16), wo_ref[...])
    h_ref[...] = x_ref[...] + _rms_rows(out, gpm_ref[...])


def _merge(c, d, m, proj, x2d, wc, bc, wd, wm, wo, gpm, tm):
    rows = x2d.shape[0]
    row = lambda width, col=0: pl.BlockSpec((tm, width), lambda i: (i, col))
    return pl.pallas_call(
        _merge_kernel,
        grid=(rows // tm,),
        in_specs=[
            row(1024), row(1024), row(1024),
            row(D_MODEL, 0), row(D_MODEL, 1), row(D_MODEL, 2),
            row(D_MODEL),
            _const_spec((C_CONV, D_MODEL)), _const_spec((1, D_MODEL)),
            _const_spec((1024, D_MODEL)), _const_spec((XW, D_MODEL)),
            _const_spec((D_MODEL, D_MODEL)), _const_spec((1, D_MODEL)),
        ],
        out_specs=row(D_MODEL),
        out_shape=jax.ShapeDtypeStruct((rows, D_MODEL), F32),
        compiler_params=_params(("parallel",)),
        name="merge",
    )(c, d, m, proj, proj, proj, x2d, wc, bc, wd, wm, wo, gpm)


FFN_ROWS = 256


def _ffn_kernel(h_ref, g1_ref, wg_ref, wu_ref, wd_ref, g2_ref, y_ref, fn_ref, acc_ref):
    f = pl.program_id(1)
    tm = h_ref.shape[0]

    @pl.when(f == 0)
    def _():
        _norm_rows_to(h_ref, g1_ref, fn_ref, 64)
        acc_ref[...] = jnp.zeros(acc_ref.shape, F32)

    for r0 in range(0, tm, FFN_ROWS):
        fn = fn_ref[r0:r0 + FFN_ROWS, :]
        act = (_silu(_dot(fn, wg_ref[...])) * _dot(fn, wu_ref[...])).astype(BF16)
        acc_ref[r0:r0 + FFN_ROWS, :] += _dot(act, wd_ref[...])

    @pl.when(f == pl.num_programs(1) - 1)
    def _():
        y_ref[...] = h_ref[...] + _rms_rows(acc_ref[...], g2_ref[...])


def _ffn(h2d, g1, wg, wu, wd, g2, tm, tf):
    rows = h2d.shape[0]
    d_ff = wg.shape[1]
    return pl.pallas_call(
        _ffn_kernel,
        grid=(rows // tm, d_ff // tf),
        in_specs=[
            pl.BlockSpec((tm, D_MODEL), lambda i, f: (i, 0)),
            pl.BlockSpec((1, D_MODEL), lambda i, f: (0, 0)),
            pl.BlockSpec((D_MODEL, tf), lambda i, f: (0, f)),
            pl.BlockSpec((D_MODEL, tf), lambda i, f: (0, f)),
            pl.BlockSpec((tf, D_MODEL), lambda i, f: (f, 0)),
            pl.BlockSpec((1, D_MODEL), lambda i, f: (0, 0)),
        ],
        out_specs=pl.BlockSpec((tm, D_MODEL), lambda i, f: (i, 0)),
        out_shape=jax.ShapeDtypeStruct((rows, D_MODEL), F32),
        scratch_shapes=[pltpu.VMEM((tm, D_MODEL), BF16), pltpu.VMEM((tm, D_MODEL), F32)],
        compiler_params=_params(("parallel", "arbitrary")),
        name="ffn",
    )(h2d, g1, wg, wu, wd, g2)


def _row(v):
    return v.reshape(1, -1).astype(F32)


def _layer_weights(w_in, w_dw, b_dw, ln_g, ln_b, w_p_conv, b_p_conv, w_dconv, a_log, dt_bias, g_dn_norm,
                   w_p_delta, w_p_mem, w_o, w_gate, w_up, w_down):
    lane_pad = (0, SMALL_W - 2 * DN_HEADS)
    arow = jnp.pad(jnp.concatenate([jnp.zeros((DN_HEADS,), F32), -jnp.exp(a_log.astype(F32))]), lane_pad)
    dtrow = jnp.pad(jnp.concatenate([jnp.zeros((DN_HEADS,), F32), dt_bias.astype(F32)]), lane_pad)
    return dict(
        w_t=jnp.swapaxes(w_in, 0, 1), w_dw=w_dw, b_dw=_row(b_dw), ln_g=_row(ln_g), ln_b=_row(ln_b),
        w_p_conv=w_p_conv.astype(BF16), b_p_conv=_row(b_p_conv), w_dconv=w_dconv, arow=_row(arow),
        dtrow=_row(dtrow), g_dn=_row(g_dn_norm), w_p_delta=w_p_delta.astype(BF16),
        w_p_mem=w_p_mem.astype(BF16), w_o=w_o.astype(BF16), w_gate=w_gate.astype(BF16),
        w_up=w_up.astype(BF16), w_down=w_down.astype(BF16))


def _tile(n, pref):
    return pref if n % pref == 0 else n


def kernel(x_prompt, x_sample, mem_prompt, cache_mem_k, cache_mem_v, state_conv, state_delta_conv, state_delta, g_pre_mix, w_in, w_dw, b_dw, ln_g, ln_b, w_p_conv, b_p_conv, w_dconv, a_log, dt_bias, g_dn_norm, w_p_delta, g_mem, w_mem_kv, w_p_mem, w_o, g_post_mix, g_pre_ffn, w_gate, w_up, w_down, g_post_ffn):
    depth = w_in.shape[0]
    assert depth == 1, "single-layer stack"
    l = 0
    pb, pt, _ = x_prompt.shape
    sb, st, _ = x_sample.shape
    assert pt % DN_CHUNK == 0 and st <= DS_ROWS and DS_ROWS % st == 0

    lw = _layer_weights(w_in[l], w_dw[l], b_dw[l], ln_g[l], ln_b[l], w_p_conv[l], b_p_conv[l], w_dconv[l],
                        a_log[l], dt_bias[l], g_dn_norm[l], w_p_delta[l], w_p_mem[l], w_o[l], w_gate[l],
                        w_up[l], w_down[l])
    gpre, gpm, gff1, gff2 = _row(g_pre_mix[l]), _row(g_post_mix[l]), _row(g_pre_ffn[l]), _row(g_post_ffn[l])

    def tail(x2d, proj, c, d, m):
        rows = x2d.shape[0]
        h = _merge(c, d, m, proj, x2d, lw["w_p_conv"], lw["b_p_conv"], lw["w_p_delta"], lw["w_p_mem"],
                   lw["w_o"], gpm, _tile(rows, 256))
        return _ffn(h, gff1, lw["w_gate"], lw["w_up"], lw["w_down"], gff2, _tile(rows, 512), 512)

    xp = x_prompt.reshape(pb * pt, D_MODEL)
    mkv = _norm_proj(mem_prompt.reshape(pb * N_MEM, D_MODEL), _row(g_mem[l]), w_mem_kv[l].astype(BF16),
                     _tile(pb * N_MEM, 512), 1024)
    proj_p, ba_p = _in_proj(xp, gpre, lw["w_t"], _tile(pb * pt, 1024))
    c_p, conv_p = _conv_prompt(proj_p, lw["w_dw"], lw["b_dw"], lw["ln_g"], lw["ln_b"], pb, pt, _tile(pt, 256))
    d_p, dconv_p, s_p = _delta_prompt(proj_p, ba_p, lw["w_dconv"], lw["arow"], lw["dtrow"], lw["g_dn"], pb, pt)
    m_p = _attn_prompt(proj_p, mkv, pb, pt, _tile(pt, 512))
    y_p = tail(xp, proj_p, c_p, d_p, m_p).reshape(pb, pt, D_MODEL)
    mk = mkv[:, 0:XW].reshape(pb, N_MEM, XA_HEADS, XA_DIM)
    mv = mkv[:, XW:2 * XW].reshape(pb, N_MEM, XA_HEADS, XA_DIM)

    xs = x_sample.reshape(sb * st, D_MODEL)
    proj_s, ba_s = _in_proj(xs, gpre, lw["w_t"], _tile(sb * st, 512))
    c_s, conv_s = _conv_sample(proj_s, state_conv[l], lw["w_dw"], lw["b_dw"], lw["ln_g"], lw["ln_b"], sb, st,
                               _tile(sb, 8))
    d_s, dconv_s, s_s = _delta_sample(proj_s, ba_s, state_delta_conv[l], state_delta[l], lw["w_dconv"],
                                      lw["arow"], lw["dtrow"], lw["g_dn"], sb, st)
    m_s = _attn_sample(proj_s, cache_mem_k, cache_mem_v, l, sb, st, 4)
    y_s = tail(xs, proj_s, c_s, d_s, m_s).reshape(sb, st, D_MODEL)

    return (y_p, y_s, mk[None], mv[None], conv_p[None], dconv_p[None], s_p[None],
            conv_s[None], dconv_s[None], s_s[None])
```

```python
import functools

import jax
import jax.numpy as jnp
from jax import lax
from jax.experimental import pallas as pl
from jax.experimental.pallas import tpu as pltpu

F32 = jnp.float32
BF16 = jnp.bfloat16

D_MODEL = 2048
N_MEM = 256
C_CONV = 1024
CONV_WIDTH = 31
DN_HEADS = 8
DN_DK = 128
DN_DV = 128
DN_CONV = 4
DN_CHUNK = 64
XA_HEADS = 4
XA_DIM = 256
QKV_W = DN_HEADS * (2 * DN_DK + DN_DV)
XW = XA_HEADS * XA_DIM

COL_GA, COL_GB = 6, 7
COL_Q, COL_K, COL_V = 8, 9, 10
COL_Z = 11
COL_XQ = 12
N_MAIN = 13 * 1024
SMALL_W = 128

VMEM_LIMIT = 52 * 1024 * 1024

_NT = (((1,), (1,)), ((), ()))


def _dot(a, b):
    return jnp.dot(a, b, preferred_element_type=F32)


def _dot_nt(a, b):
    return lax.dot_general(a, b, _NT, preferred_element_type=F32)


def _sigmoid(x):
    return jax.nn.sigmoid(x)


def _silu(x):
    return x * jax.nn.sigmoid(x)


def _rms_rows(x, g, eps=1e-6):
    ms = jnp.mean(x * x, axis=-1, keepdims=True)
    return x * lax.rsqrt(ms + eps) * g


def _params(sem):
    return pltpu.CompilerParams(dimension_semantics=sem, vmem_limit_bytes=VMEM_LIMIT)


def _const_spec(shape):
    nd = len(shape)
    return pl.BlockSpec(shape, lambda *_: (0,) * nd, pipeline_mode=pl.Buffered(1))


def _norm_rows_to(x_ref, g_ref, u_ref, rows):
    g = g_ref[...]
    tm = x_ref.shape[0]

    def body(i, c):
        r = pl.multiple_of(i * rows, rows)
        xx = x_ref[pl.ds(r, rows), :]
        u_ref[pl.ds(r, rows), :] = _rms_rows(xx, g).astype(BF16)
        return c

    lax.fori_loop(0, tm // rows, body, 0)


def _norm_cast_kernel(x_ref, g_ref, u_ref):
    _norm_rows_to(x_ref, g_ref, u_ref, 64)


def _norm_cast(x2d, g, tm):
    m, k = x2d.shape
    return pl.pallas_call(
        _norm_cast_kernel,
        grid=(m // tm,),
        in_specs=[pl.BlockSpec((tm, k), lambda i: (i, 0)), pl.BlockSpec((1, k), lambda i: (0, 0))],
        out_specs=pl.BlockSpec((tm, k), lambda i: (i, 0)),
        out_shape=jax.ShapeDtypeStruct((m, k), BF16),
        compiler_params=_params(("parallel",)),
        name="norm_cast",
    )(x2d, g)


def _proj_wt_kernel(u_ref, w_ref, o_ref, wb_ref):
    @pl.when(pl.program_id(1) == 0)
    def _():
        rows = 64

        def body(i, c):
            r = pl.multiple_of(i * rows, rows)
            wb_ref[pl.ds(r, rows), :] = w_ref[pl.ds(r, rows), :].astype(BF16)
            return c

        lax.fori_loop(0, w_ref.shape[0] // rows, body, 0)

    o_ref[...] = _dot_nt(u_ref[...], wb_ref[...])


def _proj_wt(u, w_t, row_offset, n_blocks, tn, tm):
    m, k = u.shape
    return pl.pallas_call(
        _proj_wt_kernel,
        grid=(n_blocks, m // tm),
        in_specs=[
            pl.BlockSpec((tm, k), lambda j, i: (i, 0)),
            pl.BlockSpec((pl.Element(tn), pl.Element(k)), lambda j, i: (row_offset(j), 0)),
        ],
        out_specs=pl.BlockSpec((tm, tn), lambda j, i: (i, j)),
        out_shape=jax.ShapeDtypeStruct((m, n_blocks * tn), F32),
        scratch_shapes=[pltpu.VMEM((tn, k), BF16)],
        compiler_params=_params(("parallel", "arbitrary")),
        name="in_proj",
    )(u, w_t)


IN_GLU0 = 0
IN_B0 = 2 * C_CONV + QKV_W + DN_HEADS * DN_DV
IN_XQ0 = IN_B0 + 2 * DN_HEADS
IN_GATES0 = IN_XQ0 + XW


ROW_ALIGN = 16


def _main_row_offset(j):
    unit = 1024 // ROW_ALIGN
    units = jnp.where(j < 6, IN_GATES0 // ROW_ALIGN + unit * j,
                      jnp.where(j < 12, IN_GLU0 // ROW_ALIGN + unit * (j - 6), IN_XQ0 // ROW_ALIGN))
    return units * ROW_ALIGN


def _in_proj(x2d, g, w_t, tm):
    u = _norm_cast(x2d, g, _tile(x2d.shape[0], 512))
    proj = _proj_wt(u, w_t, _main_row_offset, N_MAIN // 1024, 1024, tm)
    ba = _proj_wt(u, w_t, lambda j: j * 0 + IN_B0, 1, SMALL_W, tm)
    return proj, ba


def _norm_proj_kernel(x_ref, g_ref, w_ref, o_ref, u_ref):
    @pl.when(pl.program_id(1) == 0)
    def _():
        _norm_rows_to(x_ref, g_ref, u_ref, 64)

    o_ref[...] = _dot(u_ref[...], w_ref[...])


def _norm_proj(x2d, g, w, tm, tn):
    m, k = x2d.shape
    n = w.shape[1]
    return pl.pallas_call(
        _norm_proj_kernel,
        grid=(m // tm, n // tn),
        in_specs=[
            pl.BlockSpec((tm, k), lambda i, j: (i, 0)),
            pl.BlockSpec((1, k), lambda i, j: (0, 0)),
            pl.BlockSpec((k, tn), lambda i, j: (0, j)),
        ],
        out_specs=pl.BlockSpec((tm, tn), lambda i, j: (i, j)),
        out_shape=jax.ShapeDtypeStruct((m, n), F32),
        scratch_shapes=[pltpu.VMEM((tm, k), BF16)],
        compiler_params=_params(("parallel", "arbitrary")),
        name="mem_kv_proj",
    )(x2d, g, w)


def _ln_silu(c, g, b, eps=1e-5):
    mu = jnp.mean(c, axis=-1, keepdims=True)
    xc = c - mu
    var = jnp.mean(xc * xc, axis=-1, keepdims=True)
    return _silu(xc * lax.rsqrt(var + eps) * g + b)


CONV_PAD = 32
CONV_ROWS = 32
LANES = 128


def _conv_prompt_kernel(ga_ref, gb_ref, wdw_ref, bdw_ref, lng_ref, lnb_ref, c_ref, st_ref, ext_ref, conv_ref):
    t = pl.program_id(1)
    tt = ga_ref.shape[0]
    hist = CONV_WIDTH - 1
    slabs = [slice(s * LANES, (s + 1) * LANES) for s in range(C_CONV // LANES)]

    @pl.when(t == 0)
    def _():
        ext_ref[:, 0:CONV_PAD, :] = jnp.zeros((len(slabs), CONV_PAD, LANES), F32)

    cin = ga_ref[...] * _sigmoid(gb_ref[...])
    for s, sl in enumerate(slabs):
        ext_ref[s, CONV_PAD:CONV_PAD + tt, :] = cin[:, sl]
    blocks = range(0, tt, CONV_ROWS)
    for s, sl in enumerate(slabs):
        w_all = wdw_ref[:, sl]
        accs = [jnp.broadcast_to(bdw_ref[:, sl], (CONV_ROWS, LANES)) for _ in blocks]
        for j in range(CONV_WIDTH):
            w = jnp.broadcast_to(w_all[j:j + 1, :], (CONV_ROWS, LANES))
            for i, r0 in enumerate(blocks):
                off = CONV_PAD - hist + r0 + j
                accs[i] = accs[i] + w * ext_ref[s, off:off + CONV_ROWS, :]
        for i, r0 in enumerate(blocks):
            conv_ref[r0:r0 + CONV_ROWS, sl] = accs[i]
    lng = lng_ref[...]
    lnb = lnb_ref[...]
    for r0 in blocks:
        c_ref[r0:r0 + CONV_ROWS, :] = _ln_silu(conv_ref[r0:r0 + CONV_ROWS, :], lng, lnb).astype(c_ref.dtype)

    @pl.when(t == pl.num_programs(1) - 1)
    def _():
        for s, sl in enumerate(slabs):
            st_ref[0, :, sl] = ext_ref[s, CONV_PAD + tt - hist:CONV_PAD + tt, :]

    ext_ref[:, 0:CONV_PAD, :] = ext_ref[:, tt:tt + CONV_PAD, :]


def _conv_prompt(proj, wdw, bdw, lng, lnb, nb, t_len, tt):
    nt = t_len // tt
    return pl.pallas_call(
        _conv_prompt_kernel,
        grid=(nb, nt),
        in_specs=[
            pl.BlockSpec((tt, C_CONV), lambda b, t: (b * nt + t, COL_GA)),
            pl.BlockSpec((tt, C_CONV), lambda b, t: (b * nt + t, COL_GB)),
            pl.BlockSpec((CONV_WIDTH, C_CONV), lambda b, t: (0, 0)),
            pl.BlockSpec((1, C_CONV), lambda b, t: (0, 0)),
            pl.BlockSpec((1, C_CONV), lambda b, t: (0, 0)),
            pl.BlockSpec((1, C_CONV), lambda b, t: (0, 0)),
        ],
        out_specs=[
            pl.BlockSpec((tt, C_CONV), lambda b, t: (b * nt + t, 0)),
            pl.BlockSpec((1, CONV_WIDTH - 1, C_CONV), lambda b, t: (b, 0, 0)),
        ],
        out_shape=[
            jax.ShapeDtypeStruct((nb * t_len, C_CONV), BF16),
            jax.ShapeDtypeStruct((nb, CONV_WIDTH - 1, C_CONV), F32),
        ],
        scratch_shapes=[pltpu.VMEM((C_CONV // LANES, CONV_PAD + tt, LANES), F32), pltpu.VMEM((tt, C_CONV), F32)],
        compiler_params=_params(("parallel", "arbitrary")),
        name="conv_prompt",
    )(proj, proj, wdw, bdw, lng, lnb)


def _conv_sample_kernel(ga_ref, gb_ref, st_ref, wdw_ref, bdw_ref, lng_ref, lnb_ref, c_ref, nst_ref, ext_ref,
                        *, bb, t_len):
    hist = CONV_WIDTH - 1
    cin = ga_ref[...] * _sigmoid(gb_ref[...])
    bias = bdw_ref[...]
    rows = []
    for i in range(bb):
        ext_ref[i, 0:hist, :] = st_ref[i]
        ext_ref[i, hist:hist + t_len, :] = cin[i * t_len:(i + 1) * t_len, :]
        acc = jnp.broadcast_to(bias, (t_len, C_CONV))
        for j in range(CONV_WIDTH):
            acc = acc + wdw_ref[j:j + 1, :] * ext_ref[i, j:j + t_len, :]
        rows.append(acc)
        nst_ref[i] = ext_ref[i, t_len:t_len + hist, :]
    conv = jnp.concatenate(rows, axis=0)
    c_ref[...] = _ln_silu(conv, lng_ref[...], lnb_ref[...]).astype(c_ref.dtype)


def _conv_sample(proj, state, wdw, bdw, lng, lnb, nb, t_len, bb):
    hist = CONV_WIDTH - 1
    rows = bb * t_len
    return pl.pallas_call(
        functools.partial(_conv_sample_kernel, bb=bb, t_len=t_len),
        grid=(nb // bb,),
        in_specs=[
            pl.BlockSpec((rows, C_CONV), lambda i: (i, COL_GA)),
            pl.BlockSpec((rows, C_CONV), lambda i: (i, COL_GB)),
            pl.BlockSpec((bb, hist, C_CONV), lambda i: (i, 0, 0)),
            pl.BlockSpec((CONV_WIDTH, C_CONV), lambda i: (0, 0)),
            pl.BlockSpec((1, C_CONV), lambda i: (0, 0)),
            pl.BlockSpec((1, C_CONV), lambda i: (0, 0)),
            pl.BlockSpec((1, C_CONV), lambda i: (0, 0)),
        ],
        out_specs=[
            pl.BlockSpec((rows, C_CONV), lambda i: (i, 0)),
            pl.BlockSpec((bb, hist, C_CONV), lambda i: (i, 0, 0)),
        ],
        out_shape=[
            jax.ShapeDtypeStruct((nb * t_len, C_CONV), F32),
            jax.ShapeDtypeStruct((nb, hist, C_CONV), F32),
        ],
        scratch_shapes=[pltpu.VMEM((bb, hist + 2 * t_len + 8, C_CONV), F32)],
        compiler_params=_params(("parallel",)),
        name="conv_sample",
    )(proj, proj, state, wdw, bdw, lng, lnb)


DN_PAD = 8


def _split3(x):
    hi = x.astype(BF16)
    r1 = x - hi.astype(F32)
    mid = r1.astype(BF16)
    lo = (r1 - mid.astype(F32)).astype(BF16)
    return hi, mid, lo


def _delta_run(seqs, wdc_ref, arow_ref, dtrow_ref, gdn_ref, *, c, n_valid):
    hist = DN_CONV - 1
    n_steps = 0
    while (1 << (n_steps + 1)) < n_valid:
        n_steps += 1

    row = lax.broadcasted_iota(jnp.int32, (c, SMALL_W), 0)
    valid = row < n_valid
    ii = lax.broadcasted_iota(jnp.int32, (c, c), 0)
    jj = lax.broadcasted_iota(jnp.int32, (c, c), 1)
    causal_f = jnp.where(jj <= ii, 1.0, 0.0)
    strict_f = jnp.where(jj < ii, 1.0, 0.0)
    above = (causal_f - 1.0) * 1e30
    tri = causal_f.astype(BF16)
    ones = jnp.ones((DN_DK, DN_DK), BF16)
    arow = arow_ref[...]
    dtrow = dtrow_ref[...]
    gdn = gdn_ref[...]

    def conv_slab(ch, s):
        sl = slice(s * LANES, (s + 1) * LANES)
        conv = wdc_ref[0:1, sl] * ch["win"](s, DN_PAD - hist, DN_PAD - hist + c)
        for j in range(1, DN_CONV):
            conv = conv + wdc_ref[j:j + 1, sl] * ch["win"](s, DN_PAD - hist + j, DN_PAD - hist + j + c)
        return _silu(conv)

    def unit_norm_pair(x, y):
        sq = jnp.concatenate([x * x, y * y], axis=0)
        hi = sq.astype(BF16)
        lo = (sq - hi.astype(F32)).astype(BF16)
        inv = lax.rsqrt(_dot(hi, ones) + _dot(lo, ones) + 1e-6)
        return x * inv[0:c], y * inv[c:2 * c]

    chunks = [ch for seq in seqs for ch in seq]
    for ch in chunks:
        ba = ch["ba"]
        beta = _sigmoid(ba)
        g = arow * jax.nn.softplus(ba + dtrow)
        if n_valid < c:
            beta = jnp.where(valid, beta, 0.0)
            g = jnp.where(valid, g, 0.0)
        ch["beta"] = beta
        ch["g3"] = _split3(g)
    for ch in chunks:
        ghi, gmid, glo = ch["g3"]
        gam = _dot(tri, ghi) + _dot(tri, gmid) + _dot(tri, glo)
        ch["gam"] = gam
        ch["gam_t"] = gam.T

    units = []
    heads = DN_HEADS
    for ch in chunks:
        ch["units"] = []
        for h in range(heads):
            q, k = unit_norm_pair(conv_slab(ch, h), conv_slab(ch, heads + h))
            q = q * (DN_DK ** -0.5)
            v = conv_slab(ch, 2 * heads + h)
            if n_valid < c:
                k = jnp.where(valid[:, 0:1], k, 0.0)
            beta = ch["beta"][:, h:h + 1]
            gam_c = ch["gam"][:, 8 + h:9 + h]
            gam_r = ch["gam_t"][8 + h:9 + h, :]
            g_last = gam_r[:, c - 1:c]
            kb = k * beta
            u = dict(ch=ch, h=h, q=q, k=k, kb=kb, vb=v * beta, gam_c=gam_c, g_last=g_last,
                     dec=jnp.exp((gam_c - gam_r) * causal_f + above))
            ch["units"].append(u)
            units.append(u)

    for u in units:
        u["gmat"] = _dot_nt(jnp.concatenate([u["kb"], u["q"]], axis=0).astype(BF16), u["k"].astype(BF16))
    for u in units:
        gmat = u.pop("gmat")
        u["e"] = -(gmat[0:c] * (u["dec"] * strict_f))
        u["p"] = u["e"]
        u["qk"] = gmat[c:2 * c] * u.pop("dec")
    if n_steps:
        for u in units:
            p16 = u["p"].astype(BF16)
            u["p"] = _dot(p16, p16)
        for _ in range(n_steps - 1):
            for u in units:
                p16 = u["p"].astype(BF16)
                u["pe"] = _dot(jnp.concatenate([u["p"], u["e"]], axis=0).astype(BF16), p16)
            for u in units:
                pe = u.pop("pe")
                u["e"] = u["e"] + u["p"] + pe[c:2 * c]
                u["p"] = pe[0:c]
        for u in units:
            u["ep"] = _dot(u["e"].astype(BF16), u["p"].astype(BF16))
        for u in units:
            u["e"] = u["e"] + u.pop("p") + u.pop("ep")
    for u in units:
        eg = jnp.exp(u["gam_c"])
        rhs = jnp.concatenate([u.pop("vb"), u.pop("kb") * eg], axis=1)
        u["rhs"] = rhs
        u["qg"] = u.pop("q") * eg
        u["sol"] = _dot(u.pop("e").astype(BF16), rhs.astype(BF16))
    for u in units:
        sol = u.pop("rhs") + u.pop("sol")
        u["u"] = sol[:, 0:DN_DV]
        u["lhs16"] = jnp.concatenate([sol[:, DN_DV:DN_DV + DN_DK], u.pop("qg")], axis=0).astype(BF16)
        kg = u.pop("k") * jnp.exp(u["g_last"] - u["gam_c"])
        u["qk_kgt16"] = jnp.concatenate([u.pop("qk"), kg.T], axis=0).astype(BF16)
        u["decay"] = jnp.exp(u["g_last"])

    for j in range(len(seqs[0])):
        cur = [u for seq in seqs for u in seq[j]["units"]]
        for u in cur:
            u["s_old"] = u["ch"]["s"][u["h"]]
            u["xs"] = _dot(u["lhs16"], u["s_old"].astype(BF16))
        for u in cur:
            xs = u.pop("xs")
            u["w16"] = (u["u"] - xs[0:c]).astype(BF16)
            u["o"] = xs[c:2 * c]
        for u in cur:
            u["o2ds"] = _dot(u["qk_kgt16"], u["w16"])
        for u in cur:
            ch, h = u["ch"], u["h"]
            o2ds = u.pop("o2ds")
            ch["s"][h] = u["s_old"] * u["decay"] + o2ds[c:c + DN_DK]
            zh = ch["z"][:, h * DN_DV:(h + 1) * DN_DV]
            ch["store_o"](h, _rms_rows(u["o"] + o2ds[0:c], gdn) * _silu(zh))


DP_CHUNKS = 2


def _delta_prompt_kernel(q_ref, k_ref, v_ref, z_ref, ba_ref, wdc_ref, arow_ref, dtrow_ref, gdn_ref,
                         o_ref, nd_ref, ns_ref, ext_ref, s_ref):
    t = pl.program_id(1)
    rows = q_ref.shape[0]
    c = DN_CHUNK
    hist = DN_CONV - 1

    @pl.when(t == 0)
    def _():
        ext_ref[:, 0:DN_PAD, :] = jnp.zeros((QKV_W // LANES, DN_PAD, LANES), F32)
        s_ref[...] = jnp.zeros(s_ref.shape, F32)

    n_slab = QKV_W // LANES
    for i, ref in enumerate((q_ref, k_ref, v_ref)):
        for s in range(1024 // LANES):
            ext_ref[i * (1024 // LANES) + s, DN_PAD:DN_PAD + rows, :] = ref[:, s * LANES:(s + 1) * LANES]

    seq = []
    for g in range(rows // c):
        def win(s, j0, j1, g=g):
            return ext_ref[s, g * c + j0:g * c + j1, :]

        def store_o(h, val, g=g):
            o_ref[g * c:(g + 1) * c, h * DN_DV:(h + 1) * DN_DV] = val.astype(o_ref.dtype)

        seq.append(dict(win=win, z=z_ref[g * c:(g + 1) * c, :], ba=ba_ref[g * c:(g + 1) * c, :], s=s_ref,
                        store_o=store_o))
    _delta_run([seq], wdc_ref, arow_ref, dtrow_ref, gdn_ref, c=c, n_valid=c)

    @pl.when(t == pl.num_programs(1) - 1)
    def _():
        for s in range(n_slab):
            nd_ref[0, :, s * LANES:(s + 1) * LANES] = ext_ref[s, DN_PAD + rows - hist:DN_PAD + rows, :]
        ns_ref[0] = s_ref[...]

    ext_ref[:, DN_PAD - hist:DN_PAD, :] = ext_ref[:, DN_PAD + rows - hist:DN_PAD + rows, :]


def _delta_prompt(proj, ba, wdc, arow, dtrow, gdn, nb, t_len):
    rows = DN_CHUNK * (DP_CHUNKS if t_len % (DN_CHUNK * DP_CHUNKS) == 0 else 1)
    nt = t_len // rows
    row_spec = lambda col: pl.BlockSpec((rows, 1024), lambda b, t: (b * nt + t, col))
    small = lambda shape: pl.BlockSpec(shape, lambda b, t: (0,) * len(shape))
    return pl.pallas_call(
        _delta_prompt_kernel,
        grid=(nb, nt),
        in_specs=[
            row_spec(COL_Q), row_spec(COL_K), row_spec(COL_V), row_spec(COL_Z),
            pl.BlockSpec((rows, SMALL_W), lambda b, t: (b * nt + t, 0)),
            small((DN_CONV, QKV_W)), small((1, SMALL_W)), small((1, SMALL_W)), small((1, DN_DV)),
        ],
        out_specs=[
            pl.BlockSpec((rows, 1024), lambda b, t: (b * nt + t, 0)),
            pl.BlockSpec((1, DN_CONV - 1, QKV_W), lambda b, t: (b, 0, 0)),
            pl.BlockSpec((1, DN_HEADS, DN_DK, DN_DV), lambda b, t: (b, 0, 0, 0)),
        ],
        out_shape=[
            jax.ShapeDtypeStruct((nb * t_len, 1024), BF16),
            jax.ShapeDtypeStruct((nb, DN_CONV - 1, QKV_W), F32),
            jax.ShapeDtypeStruct((nb, DN_HEADS, DN_DK, DN_DV), F32),
        ],
        scratch_shapes=[pltpu.VMEM((QKV_W // LANES, DN_PAD + rows, LANES), F32),
                        pltpu.VMEM((DN_HEADS, DN_DK, DN_DV), F32)],
        compiler_params=_params(("parallel", "arbitrary")),
        name="delta_prompt",
    )(proj, proj, proj, proj, ba, wdc, arow, dtrow, gdn)


DS_ROWS = 8
DS_GROUPS = 2


def _delta_sample_kernel(q_ref, k_ref, v_ref, z_ref, ba_ref, dst_ref, s0_ref, wdc_ref, arow_ref, dtrow_ref,
                         gdn_ref, o_ref, nd_ref, ns_ref, ext_ref, *, t_len):
    hist = DN_CONV - 1
    per = DS_ROWS // t_len
    n_slab = QKV_W // LANES
    ns_ref[...] = s0_ref[...]
    seqs = []
    for i in range(dst_ref.shape[0]):
        g0 = (i // per) * DS_ROWS
        r0 = (i % per) * t_len
        ext_ref[i] = jnp.zeros((n_slab, DN_PAD + DS_ROWS, LANES), F32)
        for s in range(n_slab):
            sl = slice(s * LANES, (s + 1) * LANES)
            ext_ref[i, s, DN_PAD - hist:DN_PAD, :] = dst_ref[i, :, sl]
            src = (q_ref, k_ref, v_ref)[s // (1024 // LANES)]
            col = (s % (1024 // LANES)) * LANES
            ext_ref[i, s, DN_PAD:DN_PAD + t_len, :] = src[g0 + r0:g0 + r0 + t_len, col:col + LANES]
            nd_ref[i, :, sl] = ext_ref[i, s, DN_PAD + t_len - hist:DN_PAD + t_len, :]
        z = z_ref[g0:g0 + DS_ROWS, :]
        ba = ba_ref[g0:g0 + DS_ROWS, :]
        if r0:
            z = pltpu.roll(z, DS_ROWS - r0, 0)
            ba = pltpu.roll(ba, DS_ROWS - r0, 0)

        def win(s, j0, j1, i=i):
            return ext_ref[i, s, j0:j1, :]

        def store_o(h, val, g0=g0, r0=r0):
            o_ref[g0 + r0:g0 + r0 + t_len, h * DN_DV:(h + 1) * DN_DV] = val[0:t_len].astype(o_ref.dtype)

        seqs.append([dict(win=win, z=z, ba=ba, s=ns_ref.at[i], store_o=store_o)])
    _delta_run(seqs, wdc_ref, arow_ref, dtrow_ref, gdn_ref, c=DS_ROWS, n_valid=t_len)


def _delta_sample(proj, ba, dstate, s0, wdc, arow, dtrow, gdn, nb, t_len):
    per = DS_ROWS // t_len
    groups = DS_GROUPS if nb % (per * DS_GROUPS) == 0 else 1
    bb = per * groups
    rows = DS_ROWS * groups
    hist = DN_CONV - 1
    row_spec = lambda col: pl.BlockSpec((rows, 1024), lambda i: (i, col))
    small = lambda shape: pl.BlockSpec(shape, lambda i: (0,) * len(shape))
    return pl.pallas_call(
        functools.partial(_delta_sample_kernel, t_len=t_len),
        grid=(nb // bb,),
        in_specs=[
            row_spec(COL_Q), row_spec(COL_K), row_spec(COL_V), row_spec(COL_Z),
            pl.BlockSpec((rows, SMALL_W), lambda i: (i, 0)),
            pl.BlockSpec((bb, hist, QKV_W), lambda i: (i, 0, 0)),
            pl.BlockSpec((bb, DN_HEADS, DN_DK, DN_DV), lambda i: (i, 0, 0, 0)),
            small((DN_CONV, QKV_W)), small((1, SMALL_W)), small((1, SMALL_W)), small((1, DN_DV)),
        ],
        out_specs=[
            pl.BlockSpec((rows, 1024), lambda i: (i, 0)),
            pl.BlockSpec((bb, hist, QKV_W), lambda i: (i, 0, 0)),
            pl.BlockSpec((bb, DN_HEADS, DN_DK, DN_DV), lambda i: (i, 0, 0, 0)),
        ],
        out_shape=[
            jax.ShapeDtypeStruct((nb * t_len, 1024), F32),
            jax.ShapeDtypeStruct((nb, hist, QKV_W), F32),
            jax.ShapeDtypeStruct((nb, DN_HEADS, DN_DK, DN_DV), F32),
        ],
        scratch_shapes=[pltpu.VMEM((bb, QKV_W // LANES, DN_PAD + DS_ROWS, LANES), F32)],
        compiler_params=_params(("parallel",)),
        name="delta_sample",
    )(proj, proj, proj, proj, ba, dstate, s0, wdc, arow, dtrow, gdn)


def _softmax_rows(s):
    m = jnp.max(s, axis=-1, keepdims=True)
    p = jnp.exp(s - m)
    return p / jnp.sum(p, axis=-1, keepdims=True)


def _attn_pairs(pairs):
    scores = [_dot_nt(q16, k.astype(BF16)) * (XA_DIM ** -0.5) for q16, k, _ in pairs]
    probs = [_softmax_rows(s).astype(BF16) for s in scores]
    return [_dot(p, v.astype(BF16)) for p, (_, _, v) in zip(probs, pairs)]


def _attn_prompt_kernel(xq_ref, k_ref, v_ref, o_ref):
    sls = [slice(h * XA_DIM, (h + 1) * XA_DIM) for h in range(XA_HEADS)]
    outs = _attn_pairs([(xq_ref[:, sl].astype(BF16), k_ref[:, sl], v_ref[:, sl]) for sl in sls])
    for sl, o in zip(sls, outs):
        o_ref[:, sl] = o.astype(o_ref.dtype)


def _attn_prompt(proj, mkv, nb, t_len, tq):
    nt = t_len // tq
    return pl.pallas_call(
        _attn_prompt_kernel,
        grid=(nb, nt),
        in_specs=[
            pl.BlockSpec((tq, XW), lambda b, t: (b * nt + t, COL_XQ)),
            pl.BlockSpec((N_MEM, XW), lambda b, t: (b, 0)),
            pl.BlockSpec((N_MEM, XW), lambda b, t: (b, 1)),
        ],
        out_specs=pl.BlockSpec((tq, XW), lambda b, t: (b * nt + t, 0)),
        out_shape=jax.ShapeDtypeStruct((nb * t_len, XW), BF16),
        compiler_params=_params(("parallel", "arbitrary")),
        name="attn_prompt",
    )(proj, mkv, mkv)


def _attn_sample_kernel(xq_ref, k_hbm, v_hbm, o_ref, kbuf, vbuf, sem, *, layer, bb, t_len):
    i = pl.program_id(0)
    slot = i % 2

    def copies(step, slot_):
        out = []
        for h in range(XA_HEADS):
            for kv, (hbm, buf) in enumerate(((k_hbm, kbuf), (v_hbm, vbuf))):
                out.append(pltpu.make_async_copy(hbm.at[layer, pl.ds(step * bb, bb), :, h, :], buf.at[slot_, h],
                                                 sem.at[kv, slot_, h]))
        return out

    @pl.when(i == 0)
    def _():
        for cp in copies(0, 0):
            cp.start()

    @pl.when(i + 1 < pl.num_programs(0))
    def _():
        for cp in copies(i + 1, 1 - slot):
            cp.start()

    for cp in copies(i, slot):
        cp.wait()

    per = DS_ROWS // t_len
    rows = lax.broadcasted_iota(jnp.int32, (DS_ROWS, XA_DIM), 0)
    pairs = []
    for b in range(bb):
        r0 = (b // per) * DS_ROWS
        for h in range(XA_HEADS):
            q16 = xq_ref[r0:r0 + DS_ROWS, h * XA_DIM:(h + 1) * XA_DIM].astype(BF16)
            pairs.append((q16, kbuf[slot, h, b], vbuf[slot, h, b]))
    outs = _attn_pairs(pairs)
    for grp in range(bb // per):
        r0 = grp * DS_ROWS
        for h in range(XA_HEADS):
            o = outs[(grp * per) * XA_HEADS + h]
            for s in range(1, per):
                o = jnp.where(rows < s * t_len, o, outs[(grp * per + s) * XA_HEADS + h])
            o_ref[r0:r0 + DS_ROWS, h * XA_DIM:(h + 1) * XA_DIM] = o.astype(o_ref.dtype)


def _attn_sample(proj, ck, cv, layer, nb, t_len, bb):
    rows = bb * t_len
    buf = pltpu.VMEM((2, XA_HEADS, bb, N_MEM, XA_DIM), F32)
    return pl.pallas_call(
        functools.partial(_attn_sample_kernel, layer=layer, bb=bb, t_len=t_len),
        grid=(nb // bb,),
        in_specs=[pl.BlockSpec((rows, XW), lambda i: (i, COL_XQ)), pl.BlockSpec(memory_space=pl.ANY),
                  pl.BlockSpec(memory_space=pl.ANY)],
        out_specs=pl.BlockSpec((rows, XW), lambda i: (i, 0)),
        out_shape=jax.ShapeDtypeStruct((nb * t_len, XW), F32),
        scratch_shapes=[buf, buf, pltpu.SemaphoreType.DMA((2, 2, XA_HEADS))],
        compiler_params=_params(("arbitrary",)),
        name="attn_sample",
    )(proj, ck, cv)


def _merge_kernel(c_ref, d_ref, m_ref, gc_ref, gd_ref, gx_ref, x_ref, wc_ref, bc_ref, wd_ref, wm_ref, wo_ref,
                  gpm_ref, h_ref):
    br_c = _dot(c_ref[...].astype(BF16), wc_ref[...]) + bc_ref[...]
    mix = _sigmoid(gc_ref[...]) * br_c
    mix = mix + _sigmoid(gd_ref[...]) * _dot(d_ref[...].astype(BF16), wd_ref[...])
    mix = mix + _sigmoid(gx_ref[...]) * _dot(m_ref[...].astype(BF16), wm_ref[...])
    out = _dot(mix.astype(BF16), wo_ref[...])
    h_ref[...] = x_ref[...] + _rms_rows(out, gpm_ref[...])


def _merge(c, d, m, proj, x2d, wc, bc, wd, wm, wo, gpm, tm):
    rows = x2d.shape[0]
    row = lambda width, col=0: pl.BlockSpec((tm, width), lambda i: (i, col))
    return pl.pallas_call(
        _merge_kernel,
        grid=(rows // tm,),
        in_specs=[
            row(1024), row(1024), row(1024),
            row(D_MODEL, 0), row(D_MODEL, 1), row(D_MODEL, 2),
            row(D_MODEL),
            _const_spec((C_CONV, D_MODEL)), _const_spec((1, D_MODEL)),
            _const_spec((1024, D_MODEL)), _const_spec((XW, D_MODEL)),
            _const_spec((D_MODEL, D_MODEL)), _const_spec((1, D_MODEL)),
        ],
        out_specs=row(D_MODEL),
        out_shape=jax.ShapeDtypeStruct((rows, D_MODEL), F32),
        compiler_params=_params(("parallel",)),
        name="merge",
    )(c, d, m, proj, proj, proj, x2d, wc, bc, wd, wm, wo, gpm)


FFN_ROWS = 256


def _ffn_kernel(h_ref, g1_ref, wg_ref, wu_ref, wd_ref, g2_ref, y_ref, fn_ref):
    f = pl.program_id(1)
    tm = h_ref.shape[0]

    @pl.when(f == 0)
    def _():
        _norm_rows_to(h_ref, g1_ref, fn_ref, 64)
        y_ref[...] = jnp.zeros(y_ref.shape, F32)

    for r0 in range(0, tm, FFN_ROWS):
        fn = fn_ref[r0:r0 + FFN_ROWS, :]
        act = (_silu(_dot(fn, wg_ref[...])) * _dot(fn, wu_ref[...])).astype(BF16)
        y_ref[r0:r0 + FFN_ROWS, :] += _dot(act, wd_ref[...])

    @pl.when(f == pl.num_programs(1) - 1)
    def _():
        g2 = g2_ref[...]
        rows = 64

        def body(i, c):
            r = pl.multiple_of(i * rows, rows)
            y_ref[pl.ds(r, rows), :] = h_ref[pl.ds(r, rows), :] + _rms_rows(y_ref[pl.ds(r, rows), :], g2)
            return c

        lax.fori_loop(0, tm // rows, body, 0)


def _ffn(h2d, g1, wg, wu, wd, g2, tm, tf):
    rows = h2d.shape[0]
    d_ff = wg.shape[1]
    return pl.pallas_call(
        _ffn_kernel,
        grid=(rows // tm, d_ff // tf),
        in_specs=[
            pl.BlockSpec((tm, D_MODEL), lambda i, f: (i, 0), pipeline_mode=pl.Buffered(1)),
            pl.BlockSpec((1, D_MODEL), lambda i, f: (0, 0)),
            pl.BlockSpec((D_MODEL, tf), lambda i, f: (0, f)),
            pl.BlockSpec((D_MODEL, tf), lambda i, f: (0, f)),
            pl.BlockSpec((tf, D_MODEL), lambda i, f: (f, 0)),
            pl.BlockSpec((1, D_MODEL), lambda i, f: (0, 0)),
        ],
        out_specs=pl.BlockSpec((tm, D_MODEL), lambda i, f: (i, 0)),
        out_shape=jax.ShapeDtypeStruct((rows, D_MODEL), F32),
        scratch_shapes=[pltpu.VMEM((tm, D_MODEL), BF16)],
        compiler_params=_params(("parallel", "arbitrary")),
        name="ffn",
    )(h2d, g1, wg, wu, wd, g2)


def _row(v):
    return v.reshape(1, -1).astype(F32)


def _layer_weights(w_in, w_dw, b_dw, ln_g, ln_b, w_p_conv, b_p_conv, w_dconv, a_log, dt_bias, g_dn_norm,
                   w_p_delta, w_p_mem, w_o, w_gate, w_up, w_down):
    lane_pad = (0, SMALL_W - 2 * DN_HEADS)
    arow = jnp.pad(jnp.concatenate([jnp.zeros((DN_HEADS,), F32), -jnp.exp(a_log.astype(F32))]), lane_pad)
    dtrow = jnp.pad(jnp.concatenate([jnp.zeros((DN_HEADS,), F32), dt_bias.astype(F32)]), lane_pad)
    return dict(
        w_t=jnp.swapaxes(w_in, 0, 1), w_dw=w_dw, b_dw=_row(b_dw), ln_g=_row(ln_g), ln_b=_row(ln_b),
        w_p_conv=w_p_conv.astype(BF16), b_p_conv=_row(b_p_conv), w_dconv=w_dconv, arow=_row(arow),
        dtrow=_row(dtrow), g_dn=_row(g_dn_norm), w_p_delta=w_p_delta.astype(BF16),
        w_p_mem=w_p_mem.astype(BF16), w_o=w_o.astype(BF16), w_gate=w_gate.astype(BF16),
        w_up=w_up.astype(BF16), w_down=w_down.astype(BF16))


def _tile(n, pref):
    return pref if n % pref == 0 else n


def kernel(x_prompt, x_sample, mem_prompt, cache_mem_k, cache_mem_v, state_conv, state_delta_conv, state_delta, g_pre_mix, w_in, w_dw, b_dw, ln_g, ln_b, w_p_conv, b_p_conv, w_dconv, a_log, dt_bias, g_dn_norm, w_p_delta, g_mem, w_mem_kv, w_p_mem, w_o, g_post_mix, g_pre_ffn, w_gate, w_up, w_down, g_post_ffn):
    depth = w_in.shape[0]
    assert depth == 1, "single-layer stack"
    l = 0
    pb, pt, _ = x_prompt.shape
    sb, st, _ = x_sample.shape
    assert pt % DN_CHUNK == 0 and st <= DS_ROWS and DS_ROWS % st == 0

    lw = _layer_weights(w_in[l], w_dw[l], b_dw[l], ln_g[l], ln_b[l], w_p_conv[l], b_p_conv[l], w_dconv[l],
                        a_log[l], dt_bias[l], g_dn_norm[l], w_p_delta[l], w_p_mem[l], w_o[l], w_gate[l],
                        w_up[l], w_down[l])
    gpre, gpm, gff1, gff2 = _row(g_pre_mix[l]), _row(g_post_mix[l]), _row(g_pre_ffn[l]), _row(g_post_ffn[l])

    def tail(x2d, proj, c, d, m):
        rows = x2d.shape[0]
        h = _merge(c, d, m, proj, x2d, lw["w_p_conv"], lw["b_p_conv"], lw["w_p_delta"], lw["w_p_mem"],
                   lw["w_o"], gpm, _tile(rows, 256))
        return _ffn(h, gff1, lw["w_gate"], lw["w_up"], lw["w_down"], gff2, _tile(rows, 1024), 512)

    xp = x_prompt.reshape(pb * pt, D_MODEL)
    mkv = _norm_proj(mem_prompt.reshape(pb * N_MEM, D_MODEL), _row(g_mem[l]), w_mem_kv[l].astype(BF16),
                     _tile(pb * N_MEM, 512), 1024)
    proj_p, ba_p = _in_proj(xp, gpre, lw["w_t"], _tile(pb * pt, 1024))
    c_p, conv_p = _conv_prompt(proj_p, lw["w_dw"], lw["b_dw"], lw["ln_g"], lw["ln_b"], pb, pt, _tile(pt, 256))
    d_p, dconv_p, s_p = _delta_prompt(proj_p, ba_p, lw["w_dconv"], lw["arow"], lw["dtrow"], lw["g_dn"], pb, pt)
    m_p = _attn_prompt(proj_p, mkv, pb, pt, _tile(pt, 512))
    y_p = tail(xp, proj_p, c_p, d_p, m_p).reshape(pb, pt, D_MODEL)
    mk = mkv[:, 0:XW].reshape(pb, N_MEM, XA_HEADS, XA_DIM)
    mv = mkv[:, XW:2 * XW].reshape(pb, N_MEM, XA_HEADS, XA_DIM)

    xs = x_sample.reshape(sb * st, D_MODEL)
    proj_s, ba_s = _in_proj(xs, gpre, lw["w_t"], _tile(sb * st, 512))
    c_s, conv_s = _conv_sample(proj_s, state_conv[l], lw["w_dw"], lw["b_dw"], lw["ln_g"], lw["ln_b"], sb, st,
                               _tile(sb, 8))
    d_s, dconv_s, s_s = _delta_sample(proj_s, ba_s, state_delta_conv[l], state_delta[l], lw["w_dconv"],
                                      lw["arow"], lw["dtrow"], lw["g_dn"], sb, st)
    m_s = _attn_sample(proj_s, cache_mem_k, cache_mem_v, l, sb, st, 4)
    y_s = tail(xs, proj_s, c_s, d_s, m_s).reshape(sb, st, D_MODEL)

    return (y_p, y_s, mk[None], mv[None], conv_p[None], dconv_p[None], s_p[None],
            conv_s[None], dconv_s[None], s_s[None])
```

```python
import functools

import jax
import jax.numpy as jnp
from jax import lax
from jax.experimental import pallas as pl
from jax.experimental.pallas import tpu as pltpu

F32 = jnp.float32
BF16 = jnp.bfloat16

D_MODEL = 2048
N_MEM = 256
C_CONV = 1024
CONV_WIDTH = 31
DN_HEADS = 8
DN_DK = 128
DN_DV = 128
DN_CONV = 4
DN_CHUNK = 64
XA_HEADS = 4
XA_DIM = 256
QKV_W = DN_HEADS * (2 * DN_DK + DN_DV)
XW = XA_HEADS * XA_DIM

COL_GA, COL_GB = 6, 7
COL_Q, COL_K, COL_V = 8, 9, 10
COL_Z = 11
COL_XQ = 12
N_MAIN = 13 * 1024
SMALL_W = 128

VMEM_LIMIT = 52 * 1024 * 1024

_NT = (((1,), (1,)), ((), ()))


def _dot(a, b):
    return jnp.dot(a, b, preferred_element_type=F32)


def _dot_nt(a, b):
    return lax.dot_general(a, b, _NT, preferred_element_type=F32)


def _sigmoid(x):
    return jax.nn.sigmoid(x)


def _silu(x):
    return x * jax.nn.sigmoid(x)


def _rms_rows(x, g, eps=1e-6):
    ms = jnp.mean(x * x, axis=-1, keepdims=True)
    return x * lax.rsqrt(ms + eps) * g


def _params(sem):
    return pltpu.CompilerParams(dimension_semantics=sem, vmem_limit_bytes=VMEM_LIMIT)


def _const_spec(shape):
    nd = len(shape)
    return pl.BlockSpec(shape, lambda *_: (0,) * nd, pipeline_mode=pl.Buffered(1))


def _norm_rows_to(x_ref, g_ref, u_ref, rows):
    g = g_ref[...]
    tm = x_ref.shape[0]

    def body(i, c):
        r = pl.multiple_of(i * rows, rows)
        xx = x_ref[pl.ds(r, rows), :]
        u_ref[pl.ds(r, rows), :] = _rms_rows(xx, g).astype(BF16)
        return c

    lax.fori_loop(0, tm // rows, body, 0)


def _norm_cast_kernel(x_ref, g_ref, wba_ref, u_ref, ba_ref):
    _norm_rows_to(x_ref, g_ref, u_ref, 64)
    ba_ref[...] = _dot_nt(u_ref[...], wba_ref[...].astype(BF16))


def _norm_cast(x2d, g, w_t, tm):
    m, k = x2d.shape
    return pl.pallas_call(
        _norm_cast_kernel,
        grid=(m // tm,),
        in_specs=[
            pl.BlockSpec((tm, k), lambda i: (i, 0)),
            pl.BlockSpec((1, k), lambda i: (0, 0)),
            pl.BlockSpec((pl.Element(SMALL_W), pl.Element(k)), lambda i: (i * 0 + IN_B0, 0)),
        ],
        out_specs=[pl.BlockSpec((tm, k), lambda i: (i, 0)), pl.BlockSpec((tm, SMALL_W), lambda i: (i, 0))],
        out_shape=[jax.ShapeDtypeStruct((m, k), BF16), jax.ShapeDtypeStruct((m, SMALL_W), F32)],
        compiler_params=_params(("parallel",)),
        name="norm_cast",
    )(x2d, g, w_t)


def _proj_wt_kernel(u_ref, w_ref, o_ref, wb_ref):
    @pl.when(pl.program_id(1) == 0)
    def _():
        rows = 64

        def body(i, c):
            r = pl.multiple_of(i * rows, rows)
            wb_ref[pl.ds(r, rows), :] = w_ref[pl.ds(r, rows), :].astype(BF16)
            return c

        lax.fori_loop(0, w_ref.shape[0] // rows, body, 0)

    o_ref[...] = _dot_nt(u_ref[...], wb_ref[...])


def _proj_wt(u, w_t, row_offset, n_blocks, tn, tm):
    m, k = u.shape
    return pl.pallas_call(
        _proj_wt_kernel,
        grid=(n_blocks, m // tm),
        in_specs=[
            pl.BlockSpec((tm, k), lambda j, i: (i, 0)),
            pl.BlockSpec((pl.Element(tn), pl.Element(k)), lambda j, i: (row_offset(j), 0)),
        ],
        out_specs=pl.BlockSpec((tm, tn), lambda j, i: (i, j)),
        out_shape=jax.ShapeDtypeStruct((m, n_blocks * tn), F32),
        scratch_shapes=[pltpu.VMEM((tn, k), BF16)],
        compiler_params=_params(("parallel", "arbitrary")),
        name="in_proj",
    )(u, w_t)


IN_GLU0 = 0
IN_B0 = 2 * C_CONV + QKV_W + DN_HEADS * DN_DV
IN_XQ0 = IN_B0 + 2 * DN_HEADS
IN_GATES0 = IN_XQ0 + XW


ROW_ALIGN = 16


def _main_row_offset(j):
    unit = 1024 // ROW_ALIGN
    units = jnp.where(j < 6, IN_GATES0 // ROW_ALIGN + unit * j,
                      jnp.where(j < 12, IN_GLU0 // ROW_ALIGN + unit * (j - 6), IN_XQ0 // ROW_ALIGN))
    return units * ROW_ALIGN


def _in_proj(x2d, g, w_t, tm):
    u, ba = _norm_cast(x2d, g, w_t, _tile(x2d.shape[0], 512))
    return _proj_wt(u, w_t, _main_row_offset, N_MAIN // 1024, 1024, tm), ba


def _norm_proj_kernel(x_ref, g_ref, w_ref, o_ref, u_ref):
    @pl.when(pl.program_id(1) == 0)
    def _():
        _norm_rows_to(x_ref, g_ref, u_ref, 64)

    o_ref[...] = _dot(u_ref[...], w_ref[...])


def _norm_proj(x2d, g, w, tm, tn):
    m, k = x2d.shape
    n = w.shape[1]
    return pl.pallas_call(
        _norm_proj_kernel,
        grid=(m // tm, n // tn),
        in_specs=[
            pl.BlockSpec((tm, k), lambda i, j: (i, 0)),
            pl.BlockSpec((1, k), lambda i, j: (0, 0)),
            pl.BlockSpec((k, tn), lambda i, j: (0, j)),
        ],
        out_specs=pl.BlockSpec((tm, tn), lambda i, j: (i, j)),
        out_shape=jax.ShapeDtypeStruct((m, n), F32),
        scratch_shapes=[pltpu.VMEM((tm, k), BF16)],
        compiler_params=_params(("parallel", "arbitrary")),
        name="mem_kv_proj",
    )(x2d, g, w)


def _ln_silu(c, g, b, eps=1e-5):
    mu = jnp.mean(c, axis=-1, keepdims=True)
    xc = c - mu
    var = jnp.mean(xc * xc, axis=-1, keepdims=True)
    return _silu(xc * lax.rsqrt(var + eps) * g + b)


CONV_PAD = 32
CONV_ROWS = 32
LANES = 128


def _conv_prompt_kernel(ga_ref, gb_ref, wdw_ref, bdw_ref, lng_ref, lnb_ref, c_ref, st_ref, ext_ref, conv_ref):
    t = pl.program_id(1)
    tt = ga_ref.shape[0]
    hist = CONV_WIDTH - 1
    slabs = [slice(s * LANES, (s + 1) * LANES) for s in range(C_CONV // LANES)]

    @pl.when(t == 0)
    def _():
        ext_ref[:, 0:CONV_PAD, :] = jnp.zeros((len(slabs), CONV_PAD, LANES), F32)

    cin = ga_ref[...] * _sigmoid(gb_ref[...])
    for s, sl in enumerate(slabs):
        ext_ref[s, CONV_PAD:CONV_PAD + tt, :] = cin[:, sl]
    blocks = range(0, tt, CONV_ROWS)
    for s, sl in enumerate(slabs):
        w_all = wdw_ref[:, sl]
        accs = [jnp.broadcast_to(bdw_ref[:, sl], (CONV_ROWS, LANES)) for _ in blocks]
        for j in range(CONV_WIDTH):
            w = jnp.broadcast_to(w_all[j:j + 1, :], (CONV_ROWS, LANES))
            for i, r0 in enumerate(blocks):
                off = CONV_PAD - hist + r0 + j
                accs[i] = accs[i] + w * ext_ref[s, off:off + CONV_ROWS, :]
        for i, r0 in enumerate(blocks):
            conv_ref[r0:r0 + CONV_ROWS, sl] = accs[i]
    lng = lng_ref[...]
    lnb = lnb_ref[...]
    for r0 in blocks:
        c_ref[r0:r0 + CONV_ROWS, :] = _ln_silu(conv_ref[r0:r0 + CONV_ROWS, :], lng, lnb).astype(c_ref.dtype)

    @pl.when(t == pl.num_programs(1) - 1)
    def _():
        for s, sl in enumerate(slabs):
            st_ref[0, :, sl] = ext_ref[s, CONV_PAD + tt - hist:CONV_PAD + tt, :]

    ext_ref[:, 0:CONV_PAD, :] = ext_ref[:, tt:tt + CONV_PAD, :]


def _conv_prompt(proj, wdw, bdw, lng, lnb, nb, t_len, tt):
    nt = t_len // tt
    return pl.pallas_call(
        _conv_prompt_kernel,
        grid=(nb, nt),
        in_specs=[
            pl.BlockSpec((tt, C_CONV), lambda b, t: (b * nt + t, COL_GA)),
            pl.BlockSpec((tt, C_CONV), lambda b, t: (b * nt + t, COL_GB)),
            pl.BlockSpec((CONV_WIDTH, C_CONV), lambda b, t: (0, 0)),
            pl.BlockSpec((1, C_CONV), lambda b, t: (0, 0)),
            pl.BlockSpec((1, C_CONV), lambda b, t: (0, 0)),
            pl.BlockSpec((1, C_CONV), lambda b, t: (0, 0)),
        ],
        out_specs=[
            pl.BlockSpec((tt, C_CONV), lambda b, t: (b * nt + t, 0)),
            pl.BlockSpec((1, CONV_WIDTH - 1, C_CONV), lambda b, t: (b, 0, 0)),
        ],
        out_shape=[
            jax.ShapeDtypeStruct((nb * t_len, C_CONV), BF16),
            jax.ShapeDtypeStruct((nb, CONV_WIDTH - 1, C_CONV), F32),
        ],
        scratch_shapes=[pltpu.VMEM((C_CONV // LANES, CONV_PAD + tt, LANES), F32), pltpu.VMEM((tt, C_CONV), F32)],
        compiler_params=_params(("parallel", "arbitrary")),
        name="conv_prompt",
    )(proj, proj, wdw, bdw, lng, lnb)


def _conv_sample_kernel(ga_ref, gb_ref, st_ref, wdw_ref, bdw_ref, lng_ref, lnb_ref, c_ref, nst_ref, ext_ref,
                        *, bb, t_len):
    hist = CONV_WIDTH - 1
    cin = ga_ref[...] * _sigmoid(gb_ref[...])
    bias = bdw_ref[...]
    rows = []
    for i in range(bb):
        ext_ref[i, 0:hist, :] = st_ref[i]
        ext_ref[i, hist:hist + t_len, :] = cin[i * t_len:(i + 1) * t_len, :]
        acc = jnp.broadcast_to(bias, (t_len, C_CONV))
        for j in range(CONV_WIDTH):
            acc = acc + wdw_ref[j:j + 1, :] * ext_ref[i, j:j + t_len, :]
        rows.append(acc)
        nst_ref[i] = ext_ref[i, t_len:t_len + hist, :]
    conv = jnp.concatenate(rows, axis=0)
    c_ref[...] = _ln_silu(conv, lng_ref[...], lnb_ref[...]).astype(c_ref.dtype)


def _conv_sample(proj, state, wdw, bdw, lng, lnb, nb, t_len, bb):
    hist = CONV_WIDTH - 1
    rows = bb * t_len
    return pl.pallas_call(
        functools.partial(_conv_sample_kernel, bb=bb, t_len=t_len),
        grid=(nb // bb,),
        in_specs=[
            pl.BlockSpec((rows, C_CONV), lambda i: (i, COL_GA)),
            pl.BlockSpec((rows, C_CONV), lambda i: (i, COL_GB)),
            pl.BlockSpec((bb, hist, C_CONV), lambda i: (i, 0, 0)),
            pl.BlockSpec((CONV_WIDTH, C_CONV), lambda i: (0, 0)),
            pl.BlockSpec((1, C_CONV), lambda i: (0, 0)),
            pl.BlockSpec((1, C_CONV), lambda i: (0, 0)),
            pl.BlockSpec((1, C_CONV), lambda i: (0, 0)),
        ],
        out_specs=[
            pl.BlockSpec((rows, C_CONV), lambda i: (i, 0)),
            pl.BlockSpec((bb, hist, C_CONV), lambda i: (i, 0, 0)),
        ],
        out_shape=[
            jax.ShapeDtypeStruct((nb * t_len, C_CONV), F32),
            jax.ShapeDtypeStruct((nb, hist, C_CONV), F32),
        ],
        scratch_shapes=[pltpu.VMEM((bb, hist + 2 * t_len + 8, C_CONV), F32)],
        compiler_params=_params(("parallel",)),
        name="conv_sample",
    )(proj, proj, state, wdw, bdw, lng, lnb)


DN_PAD = 8


def _split3(x):
    hi = x.astype(BF16)
    r1 = x - hi.astype(F32)
    mid = r1.astype(BF16)
    lo = (r1 - mid.astype(F32)).astype(BF16)
    return hi, mid, lo


def _delta_run(seqs, wdc_ref, arow_ref, dtrow_ref, gdn_ref, *, c, n_valid):
    hist = DN_CONV - 1
    n_steps = 0
    while (1 << (n_steps + 1)) < n_valid:
        n_steps += 1

    row = lax.broadcasted_iota(jnp.int32, (c, SMALL_W), 0)
    valid = row < n_valid
    ii = lax.broadcasted_iota(jnp.int32, (c, c), 0)
    jj = lax.broadcasted_iota(jnp.int32, (c, c), 1)
    causal_f = jnp.where(jj <= ii, 1.0, 0.0)
    strict_f = jnp.where(jj < ii, 1.0, 0.0)
    above = (causal_f - 1.0) * 1e30
    tri = causal_f.astype(BF16)
    ones = jnp.ones((DN_DK, DN_DK), BF16)
    arow = arow_ref[...]
    dtrow = dtrow_ref[...]
    gdn = gdn_ref[...]

    def conv_slab(ch, s):
        sl = slice(s * LANES, (s + 1) * LANES)
        conv = wdc_ref[0:1, sl] * ch["win"](s, DN_PAD - hist, DN_PAD - hist + c)
        for j in range(1, DN_CONV):
            conv = conv + wdc_ref[j:j + 1, sl] * ch["win"](s, DN_PAD - hist + j, DN_PAD - hist + j + c)
        return _silu(conv)

    def unit_norm_pair(x, y):
        sq = jnp.concatenate([x * x, y * y], axis=0)
        hi = sq.astype(BF16)
        lo = (sq - hi.astype(F32)).astype(BF16)
        inv = lax.rsqrt(_dot(hi, ones) + _dot(lo, ones) + 1e-6)
        return x * inv[0:c], y * inv[c:2 * c]

    chunks = [ch for seq in seqs for ch in seq]
    for ch in chunks:
        ba = ch["ba"]
        beta = _sigmoid(ba)
        g = arow * jax.nn.softplus(ba + dtrow)
        if n_valid < c:
            beta = jnp.where(valid, beta, 0.0)
            g = jnp.where(valid, g, 0.0)
        ch["beta"] = beta
        ch["g3"] = _split3(g)
    for ch in chunks:
        ghi, gmid, glo = ch["g3"]
        gam = _dot(tri, ghi) + _dot(tri, gmid) + _dot(tri, glo)
        ch["gam"] = gam
        ch["gam_t"] = gam.T

    units = []
    heads = DN_HEADS
    for ch in chunks:
        ch["units"] = []
        for h in range(heads):
            q, k = unit_norm_pair(conv_slab(ch, h), conv_slab(ch, heads + h))
            q = q * (DN_DK ** -0.5)
            v = conv_slab(ch, 2 * heads + h)
            if n_valid < c:
                k = jnp.where(valid[:, 0:1], k, 0.0)
            beta = ch["beta"][:, h:h + 1]
            gam_c = ch["gam"][:, 8 + h:9 + h]
            gam_r = ch["gam_t"][8 + h:9 + h, :]
            g_last = gam_r[:, c - 1:c]
            kb = k * beta
            u = dict(ch=ch, h=h, q=q, k=k, kb=kb, vb=v * beta, gam_c=gam_c, g_last=g_last,
                     dec=jnp.exp((gam_c - gam_r) * causal_f + above))
            ch["units"].append(u)
            units.append(u)

    for u in units:
        u["gmat"] = _dot_nt(jnp.concatenate([u["kb"], u["q"]], axis=0).astype(BF16), u["k"].astype(BF16))
    for u in units:
        gmat = u.pop("gmat")
        u["e"] = -(gmat[0:c] * (u["dec"] * strict_f))
        u["p"] = u["e"]
        u["qk"] = gmat[c:2 * c] * u.pop("dec")
    if n_steps:
        for u in units:
            p16 = u["p"].astype(BF16)
            u["p"] = _dot(p16, p16)
        for _ in range(n_steps - 1):
            for u in units:
                p16 = u["p"].astype(BF16)
                u["pe"] = _dot(jnp.concatenate([u["p"], u["e"]], axis=0).astype(BF16), p16)
            for u in units:
                pe = u.pop("pe")
                u["e"] = u["e"] + u["p"] + pe[c:2 * c]
                u["p"] = pe[0:c]
        for u in units:
            u["ep"] = _dot(u["e"].astype(BF16), u["p"].astype(BF16))
        for u in units:
            u["e"] = u["e"] + u.pop("p") + u.pop("ep")
    for u in units:
        eg = jnp.exp(u["gam_c"])
        rhs = jnp.concatenate([u.pop("vb"), u.pop("kb") * eg], axis=1)
        u["rhs"] = rhs
        u["qg"] = u.pop("q") * eg
        u["sol"] = _dot(u.pop("e").astype(BF16), rhs.astype(BF16))
    for u in units:
        sol = u.pop("rhs") + u.pop("sol")
        u["u"] = sol[:, 0:DN_DV]
        u["lhs16"] = jnp.concatenate([sol[:, DN_DV:DN_DV + DN_DK], u.pop("qg")], axis=0).astype(BF16)
        kg = u.pop("k") * jnp.exp(u["g_last"] - u["gam_c"])
        u["qk_kgt16"] = jnp.concatenate([u.pop("qk"), kg.T], axis=0).astype(BF16)
        u["decay"] = jnp.exp(u["g_last"])

    for j in range(len(seqs[0])):
        cur = [u for seq in seqs for u in seq[j]["units"]]
        for u in cur:
            u["s_old"] = u["ch"]["s"][u["h"]]
            u["xs"] = _dot(u["lhs16"], u["s_old"].astype(BF16))
        for u in cur:
            xs = u.pop("xs")
            u["w16"] = (u["u"] - xs[0:c]).astype(BF16)
            u["o"] = xs[c:2 * c]
        for u in cur:
            u["o2ds"] = _dot(u["qk_kgt16"], u["w16"])
        for u in cur:
            ch, h = u["ch"], u["h"]
            o2ds = u.pop("o2ds")
            ch["s"][h] = u["s_old"] * u["decay"] + o2ds[c:c + DN_DK]
            zh = ch["z"][:, h * DN_DV:(h + 1) * DN_DV]
            ch["store_o"](h, _rms_rows(u["o"] + o2ds[0:c], gdn) * _silu(zh))


DP_CHUNKS = 2


def _delta_prompt_kernel(q_ref, k_ref, v_ref, z_ref, ba_ref, wdc_ref, arow_ref, dtrow_ref, gdn_ref,
                         o_ref, nd_ref, ns_ref, ext_ref, s_ref):
    t = pl.program_id(1)
    rows = q_ref.shape[0]
    c = DN_CHUNK
    hist = DN_CONV - 1

    @pl.when(t == 0)
    def _():
        ext_ref[:, 0:DN_PAD, :] = jnp.zeros((QKV_W // LANES, DN_PAD, LANES), F32)
        s_ref[...] = jnp.zeros(s_ref.shape, F32)

    n_slab = QKV_W // LANES
    for i, ref in enumerate((q_ref, k_ref, v_ref)):
        for s in range(1024 // LANES):
            ext_ref[i * (1024 // LANES) + s, DN_PAD:DN_PAD + rows, :] = ref[:, s * LANES:(s + 1) * LANES]

    seq = []
    for g in range(rows // c):
        def win(s, j0, j1, g=g):
            return ext_ref[s, g * c + j0:g * c + j1, :]

        def store_o(h, val, g=g):
            o_ref[g * c:(g + 1) * c, h * DN_DV:(h + 1) * DN_DV] = val.astype(o_ref.dtype)

        seq.append(dict(win=win, z=z_ref[g * c:(g + 1) * c, :], ba=ba_ref[g * c:(g + 1) * c, :], s=s_ref,
                        store_o=store_o))
    _delta_run([seq], wdc_ref, arow_ref, dtrow_ref, gdn_ref, c=c, n_valid=c)

    @pl.when(t == pl.num_programs(1) - 1)
    def _():
        for s in range(n_slab):
            nd_ref[0, :, s * LANES:(s + 1) * LANES] = ext_ref[s, DN_PAD + rows - hist:DN_PAD + rows, :]
        ns_ref[0] = s_ref[...]

    ext_ref[:, DN_PAD - hist:DN_PAD, :] = ext_ref[:, DN_PAD + rows - hist:DN_PAD + rows, :]


def _delta_prompt(proj, ba, wdc, arow, dtrow, gdn, nb, t_len):
    rows = DN_CHUNK * (DP_CHUNKS if t_len % (DN_CHUNK * DP_CHUNKS) == 0 else 1)
    nt = t_len // rows
    row_spec = lambda col: pl.BlockSpec((rows, 1024), lambda b, t: (b * nt + t, col))
    small = lambda shape: pl.BlockSpec(shape, lambda b, t: (0,) * len(shape))
    return pl.pallas_call(
        _delta_prompt_kernel,
        grid=(nb, nt),
        in_specs=[
            row_spec(COL_Q), row_spec(COL_K), row_spec(COL_V), row_spec(COL_Z),
            pl.BlockSpec((rows, SMALL_W), lambda b, t: (b * nt + t, 0)),
            small((DN_CONV, QKV_W)), small((1, SMALL_W)), small((1, SMALL_W)), small((1, DN_DV)),
        ],
        out_specs=[
            pl.BlockSpec((rows, 1024), lambda b, t: (b * nt + t, 0)),
            pl.BlockSpec((1, DN_CONV - 1, QKV_W), lambda b, t: (b, 0, 0)),
            pl.BlockSpec((1, DN_HEADS, DN_DK, DN_DV), lambda b, t: (b, 0, 0, 0)),
        ],
        out_shape=[
            jax.ShapeDtypeStruct((nb * t_len, 1024), BF16),
            jax.ShapeDtypeStruct((nb, DN_CONV - 1, QKV_W), F32),
            jax.ShapeDtypeStruct((nb, DN_HEADS, DN_DK, DN_DV), F32),
        ],
        scratch_shapes=[pltpu.VMEM((QKV_W // LANES, DN_PAD + rows, LANES), F32),
                        pltpu.VMEM((DN_HEADS, DN_DK, DN_DV), F32)],
        compiler_params=_params(("parallel", "arbitrary")),
        name="delta_prompt",
    )(proj, proj, proj, proj, ba, wdc, arow, dtrow, gdn)


DS_ROWS = 8
DS_GROUPS = 2


def _delta_sample_kernel(q_ref, k_ref, v_ref, z_ref, ba_ref, dst_ref, s0_ref, wdc_ref, arow_ref, dtrow_ref,
                         gdn_ref, o_ref, nd_ref, ns_ref, ext_ref, *, t_len):
    hist = DN_CONV - 1
    per = DS_ROWS // t_len
    n_slab = QKV_W // LANES
    ns_ref[...] = s0_ref[...]
    seqs = []
    for i in range(dst_ref.shape[0]):
        g0 = (i // per) * DS_ROWS
        r0 = (i % per) * t_len
        ext_ref[i] = jnp.zeros((n_slab, DN_PAD + DS_ROWS, LANES), F32)
        for s in range(n_slab):
            sl = slice(s * LANES, (s + 1) * LANES)
            ext_ref[i, s, DN_PAD - hist:DN_PAD, :] = dst_ref[i, :, sl]
            src = (q_ref, k_ref, v_ref)[s // (1024 // LANES)]
            col = (s % (1024 // LANES)) * LANES
            ext_ref[i, s, DN_PAD:DN_PAD + t_len, :] = src[g0 + r0:g0 + r0 + t_len, col:col + LANES]
            nd_ref[i, :, sl] = ext_ref[i, s, DN_PAD + t_len - hist:DN_PAD + t_len, :]
        z = z_ref[g0:g0 + DS_ROWS, :]
        ba = ba_ref[g0:g0 + DS_ROWS, :]
        if r0:
            z = pltpu.roll(z, DS_ROWS - r0, 0)
            ba = pltpu.roll(ba, DS_ROWS - r0, 0)

        def win(s, j0, j1, i=i):
            return ext_ref[i, s, j0:j1, :]

        def store_o(h, val, g0=g0, r0=r0):
            o_ref[g0 + r0:g0 + r0 + t_len, h * DN_DV:(h + 1) * DN_DV] = val[0:t_len].astype(o_ref.dtype)

        seqs.append([dict(win=win, z=z, ba=ba, s=ns_ref.at[i], store_o=store_o)])
    _delta_run(seqs, wdc_ref, arow_ref, dtrow_ref, gdn_ref, c=DS_ROWS, n_valid=t_len)


def _delta_sample(proj, ba, dstate, s0, wdc, arow, dtrow, gdn, nb, t_len):
    per = DS_ROWS // t_len
    groups = DS_GROUPS if nb % (per * DS_GROUPS) == 0 else 1
    bb = per * groups
    rows = DS_ROWS * groups
    hist = DN_CONV - 1
    row_spec = lambda col: pl.BlockSpec((rows, 1024), lambda i: (i, col))
    small = lambda shape: pl.BlockSpec(shape, lambda i: (0,) * len(shape))
    return pl.pallas_call(
        functools.partial(_delta_sample_kernel, t_len=t_len),
        grid=(nb // bb,),
        in_specs=[
            row_spec(COL_Q), row_spec(COL_K), row_spec(COL_V), row_spec(COL_Z),
            pl.BlockSpec((rows, SMALL_W), lambda i: (i, 0)),
            pl.BlockSpec((bb, hist, QKV_W), lambda i: (i, 0, 0)),
            pl.BlockSpec((bb, DN_HEADS, DN_DK, DN_DV), lambda i: (i, 0, 0, 0)),
            small((DN_CONV, QKV_W)), small((1, SMALL_W)), small((1, SMALL_W)), small((1, DN_DV)),
        ],
        out_specs=[
            pl.BlockSpec((rows, 1024), lambda i: (i, 0)),
            pl.BlockSpec((bb, hist, QKV_W), lambda i: (i, 0, 0)),
            pl.BlockSpec((bb, DN_HEADS, DN_DK, DN_DV), lambda i: (i, 0, 0, 0)),
        ],
        out_shape=[
            jax.ShapeDtypeStruct((nb * t_len, 1024), F32),
            jax.ShapeDtypeStruct((nb, hist, QKV_W), F32),
            jax.ShapeDtypeStruct((nb, DN_HEADS, DN_DK, DN_DV), F32),
        ],
        scratch_shapes=[pltpu.VMEM((bb, QKV_W // LANES, DN_PAD + DS_ROWS, LANES), F32)],
        compiler_params=_params(("parallel",)),
        name="delta_sample",
    )(proj, proj, proj, proj, ba, dstate, s0, wdc, arow, dtrow, gdn)


def _softmax_rows(s):
    m = jnp.max(s, axis=-1, keepdims=True)
    p = jnp.exp(s - m)
    return p / jnp.sum(p, axis=-1, keepdims=True)


def _attn_pairs(pairs):
    scores = [_dot_nt(q16, k.astype(BF16)) * (XA_DIM ** -0.5) for q16, k, _ in pairs]
    probs = [_softmax_rows(s).astype(BF16) for s in scores]
    return [_dot(p, v.astype(BF16)) for p, (_, _, v) in zip(probs, pairs)]


def _attn_prompt_kernel(xq_ref, k_ref, v_ref, o_ref):
    sls = [slice(h * XA_DIM, (h + 1) * XA_DIM) for h in range(XA_HEADS)]
    outs = _attn_pairs([(xq_ref[:, sl].astype(BF16), k_ref[:, sl], v_ref[:, sl]) for sl in sls])
    for sl, o in zip(sls, outs):
        o_ref[:, sl] = o.astype(o_ref.dtype)


def _attn_prompt(proj, mkv, nb, t_len, tq):
    nt = t_len // tq
    return pl.pallas_call(
        _attn_prompt_kernel,
        grid=(nb, nt),
        in_specs=[
            pl.BlockSpec((tq, XW), lambda b, t: (b * nt + t, COL_XQ)),
            pl.BlockSpec((N_MEM, XW), lambda b, t: (b, 0)),
            pl.BlockSpec((N_MEM, XW), lambda b, t: (b, 1)),
        ],
        out_specs=pl.BlockSpec((tq, XW), lambda b, t: (b * nt + t, 0)),
        out_shape=jax.ShapeDtypeStruct((nb * t_len, XW), BF16),
        compiler_params=_params(("parallel", "arbitrary")),
        name="attn_prompt",
    )(proj, mkv, mkv)


def _attn_sample_kernel(xq_ref, k_hbm, v_hbm, o_ref, kbuf, vbuf, sem, *, layer, bb, t_len):
    i = pl.program_id(0)
    slot = i % 2

    def copies(step, slot_):
        out = []
        for h in range(XA_HEADS):
            for kv, (hbm, buf) in enumerate(((k_hbm, kbuf), (v_hbm, vbuf))):
                out.append(pltpu.make_async_copy(hbm.at[layer, pl.ds(step * bb, bb), :, h, :], buf.at[slot_, h],
                                                 sem.at[kv, slot_, h]))
        return out

    @pl.when(i == 0)
    def _():
        for cp in copies(0, 0):
            cp.start()

    @pl.when(i + 1 < pl.num_programs(0))
    def _():
        for cp in copies(i + 1, 1 - slot):
            cp.start()

    for cp in copies(i, slot):
        cp.wait()

    per = DS_ROWS // t_len
    rows = lax.broadcasted_iota(jnp.int32, (DS_ROWS, XA_DIM), 0)
    pairs = []
    for b in range(bb):
        r0 = (b // per) * DS_ROWS
        for h in range(XA_HEADS):
            q16 = xq_ref[r0:r0 + DS_ROWS, h * XA_DIM:(h + 1) * XA_DIM].astype(BF16)
            pairs.append((q16, kbuf[slot, h, b], vbuf[slot, h, b]))
    outs = _attn_pairs(pairs)
    for grp in range(bb // per):
        r0 = grp * DS_ROWS
        for h in range(XA_HEADS):
            o = outs[(grp * per) * XA_HEADS + h]
            for s in range(1, per):
                o = jnp.where(rows < s * t_len, o, outs[(grp * per + s) * XA_HEADS + h])
            o_ref[r0:r0 + DS_ROWS, h * XA_DIM:(h + 1) * XA_DIM] = o.astype(o_ref.dtype)


def _attn_sample(proj, ck, cv, layer, nb, t_len, bb):
    rows = bb * t_len
    buf = pltpu.VMEM((2, XA_HEADS, bb, N_MEM, XA_DIM), F32)
    return pl.pallas_call(
        functools.partial(_attn_sample_kernel, layer=layer, bb=bb, t_len=t_len),
        grid=(nb // bb,),
        in_specs=[pl.BlockSpec((rows, XW), lambda i: (i, COL_XQ)), pl.BlockSpec(memory_space=pl.ANY),
                  pl.BlockSpec(memory_space=pl.ANY)],
        out_specs=pl.BlockSpec((rows, XW), lambda i: (i, 0)),
        out_shape=jax.ShapeDtypeStruct((nb * t_len, XW), F32),
        scratch_shapes=[buf, buf, pltpu.SemaphoreType.DMA((2, 2, XA_HEADS))],
        compiler_params=_params(("arbitrary",)),
        name="attn_sample",
    )(proj, ck, cv)


def _merge_kernel(c_ref, d_ref, m_ref, gc_ref, gd_ref, gx_ref, x_ref, wc_ref, bc_ref, wd_ref, wm_ref, wo_ref,
                  gpm_ref, h_ref):
    br_c = _dot(c_ref[...].astype(BF16), wc_ref[...]) + bc_ref[...]
    mix = _sigmoid(gc_ref[...]) * br_c
    mix = mix + _sigmoid(gd_ref[...]) * _dot(d_ref[...].astype(BF16), wd_ref[...])
    mix = mix + _sigmoid(gx_ref[...]) * _dot(m_ref[...].astype(BF16), wm_ref[...])
    out = _dot(mix.astype(BF16), wo_ref[...])
    h_ref[...] = x_ref[...] + _rms_rows(out, gpm_ref[...])


def _merge(c, d, m, proj, x2d, wc, bc, wd, wm, wo, gpm, tm):
    rows = x2d.shape[0]
    row = lambda width, col=0: pl.BlockSpec((tm, width), lambda i: (i, col))
    return pl.pallas_call(
        _merge_kernel,
        grid=(rows // tm,),
        in_specs=[
            row(1024), row(1024), row(1024),
            row(D_MODEL, 0), row(D_MODEL, 1), row(D_MODEL, 2),
            row(D_MODEL),
            _const_spec((C_CONV, D_MODEL)), _const_spec((1, D_MODEL)),
            _const_spec((1024, D_MODEL)), _const_spec((XW, D_MODEL)),
            _const_spec((D_MODEL, D_MODEL)), _const_spec((1, D_MODEL)),
        ],
        out_specs=row(D_MODEL),
        out_shape=jax.ShapeDtypeStruct((rows, D_MODEL), F32),
        compiler_params=_params(("parallel",)),
        name="merge",
    )(c, d, m, proj, proj, proj, x2d, wc, bc, wd, wm, wo, gpm)


FFN_TF = 512
FFN_TN = 512


def _ffn_kernel(h_ref, g1_ref, wg_ref, wu_ref, wd_ref, g2_ref, y_ref, fn_ref, act_ref, *, n_up):
    f = pl.program_id(1)

    @pl.when(f == 0)
    def _():
        _norm_rows_to(h_ref, g1_ref, fn_ref, 64)

    @pl.when(f < n_up)
    def _():
        fn = fn_ref[...]
        col = pl.multiple_of(f * FFN_TF, FFN_TF)
        act_ref[:, pl.ds(col, FFN_TF)] = (_silu(_dot(fn, wg_ref[...])) * _dot(fn, wu_ref[...])).astype(BF16)

    @pl.when(f >= n_up)
    def _():
        col = pl.multiple_of((f - n_up) * FFN_TN, FFN_TN)
        y_ref[:, pl.ds(col, FFN_TN)] = _dot(act_ref[...], wd_ref[...])

    @pl.when(f == pl.num_programs(1) - 1)
    def _():
        g2 = g2_ref[...]
        rows = 64

        def body(i, c):
            r = pl.multiple_of(i * rows, rows)
            y_ref[pl.ds(r, rows), :] = h_ref[pl.ds(r, rows), :] + _rms_rows(y_ref[pl.ds(r, rows), :], g2)
            return c

        lax.fori_loop(0, h_ref.shape[0] // rows, body, 0)


def _ffn(h2d, g1, wg, wu, wd, g2, tm):
    rows = h2d.shape[0]
    d_ff = wg.shape[1]
    n_up = d_ff // FFN_TF
    n_down = D_MODEL // FFN_TN
    return pl.pallas_call(
        functools.partial(_ffn_kernel, n_up=n_up),
        grid=(rows // tm, n_up + n_down),
        in_specs=[
            pl.BlockSpec((tm, D_MODEL), lambda i, f: (i, 0)),
            pl.BlockSpec((1, D_MODEL), lambda i, f: (0, 0)),
            pl.BlockSpec((D_MODEL, FFN_TF), lambda i, f: (0, jnp.minimum(f, n_up - 1))),
            pl.BlockSpec((D_MODEL, FFN_TF), lambda i, f: (0, jnp.minimum(f, n_up - 1))),
            pl.BlockSpec((d_ff, FFN_TN), lambda i, f: (0, jnp.maximum(f - n_up, 0))),
            pl.BlockSpec((1, D_MODEL), lambda i, f: (0, 0)),
        ],
        out_specs=pl.BlockSpec((tm, D_MODEL), lambda i, f: (i, 0)),
        out_shape=jax.ShapeDtypeStruct((rows, D_MODEL), F32),
        scratch_shapes=[pltpu.VMEM((tm, D_MODEL), BF16), pltpu.VMEM((tm, d_ff), BF16)],
        compiler_params=_params(("parallel", "arbitrary")),
        name="ffn",
    )(h2d, g1, wg, wu, wd, g2)


def _row(v):
    return v.reshape(1, -1).astype(F32)


def _layer_weights(w_in, w_dw, b_dw, ln_g, ln_b, w_p_conv, b_p_conv, w_dconv, a_log, dt_bias, g_dn_norm,
                   w_p_delta, w_p_mem, w_o, w_gate, w_up, w_down):
    lane_pad = (0, SMALL_W - 2 * DN_HEADS)
    arow = jnp.pad(jnp.concatenate([jnp.zeros((DN_HEADS,), F32), -jnp.exp(a_log.astype(F32))]), lane_pad)
    dtrow = jnp.pad(jnp.concatenate([jnp.zeros((DN_HEADS,), F32), dt_bias.astype(F32)]), lane_pad)
    return dict(
        w_t=jnp.swapaxes(w_in, 0, 1), w_dw=w_dw, b_dw=_row(b_dw), ln_g=_row(ln_g), ln_b=_row(ln_b),
        w_p_conv=w_p_conv.astype(BF16), b_p_conv=_row(b_p_conv), w_dconv=w_dconv, arow=_row(arow),
        dtrow=_row(dtrow), g_dn=_row(g_dn_norm), w_p_delta=w_p_delta.astype(BF16),
        w_p_mem=w_p_mem.astype(BF16), w_o=w_o.astype(BF16), w_gate=w_gate.astype(BF16),
        w_up=w_up.astype(BF16), w_down=w_down.astype(BF16))


def _tile(n, pref):
    return pref if n % pref == 0 else n


def kernel(x_prompt, x_sample, mem_prompt, cache_mem_k, cache_mem_v, state_conv, state_delta_conv, state_delta, g_pre_mix, w_in, w_dw, b_dw, ln_g, ln_b, w_p_conv, b_p_conv, w_dconv, a_log, dt_bias, g_dn_norm, w_p_delta, g_mem, w_mem_kv, w_p_mem, w_o, g_post_mix, g_pre_ffn, w_gate, w_up, w_down, g_post_ffn):
    depth = w_in.shape[0]
    assert depth == 1, "single-layer stack"
    l = 0
    pb, pt, _ = x_prompt.shape
    sb, st, _ = x_sample.shape
    assert pt % DN_CHUNK == 0 and st <= DS_ROWS and DS_ROWS % st == 0

    lw = _layer_weights(w_in[l], w_dw[l], b_dw[l], ln_g[l], ln_b[l], w_p_conv[l], b_p_conv[l], w_dconv[l],
                        a_log[l], dt_bias[l], g_dn_norm[l], w_p_delta[l], w_p_mem[l], w_o[l], w_gate[l],
                        w_up[l], w_down[l])
    gpre, gpm, gff1, gff2 = _row(g_pre_mix[l]), _row(g_post_mix[l]), _row(g_pre_ffn[l]), _row(g_post_ffn[l])

    def tail(x2d, proj, c, d, m):
        rows = x2d.shape[0]
        h = _merge(c, d, m, proj, x2d, lw["w_p_conv"], lw["b_p_conv"], lw["w_p_delta"], lw["w_p_mem"],
                   lw["w_o"], gpm, _tile(rows, 256))
        return _ffn(h, gff1, lw["w_gate"], lw["w_up"], lw["w_down"], gff2, _tile(rows, 512))

    xp = x_prompt.reshape(pb * pt, D_MODEL)
    mkv = _norm_proj(mem_prompt.reshape(pb * N_MEM, D_MODEL), _row(g_mem[l]), w_mem_kv[l].astype(BF16),
                     _tile(pb * N_MEM, 512), 1024)
    proj_p, ba_p = _in_proj(xp, gpre, lw["w_t"], _tile(pb * pt, 1024))
    c_p, conv_p = _conv_prompt(proj_p, lw["w_dw"], lw["b_dw"], lw["ln_g"], lw["ln_b"], pb, pt, _tile(pt, 256))
    d_p, dconv_p, s_p = _delta_prompt(proj_p, ba_p, lw["w_dconv"], lw["arow"], lw["dtrow"], lw["g_dn"], pb, pt)
    m_p = _attn_prompt(proj_p, mkv, pb, pt, _tile(pt, 512))
    y_p = tail(xp, proj_p, c_p, d_p, m_p).reshape(pb, pt, D_MODEL)
    mk = mkv[:, 0:XW].reshape(pb, N_MEM, XA_HEADS, XA_DIM)
    mv = mkv[:, XW:2 * XW].reshape(pb, N_MEM, XA_HEADS, XA_DIM)

    xs = x_sample.reshape(sb * st, D_MODEL)
    proj_s, ba_s = _in_proj(xs, gpre, lw["w_t"], _tile(sb * st, 512))
    c_s, conv_s = _conv_sample(proj_s, state_conv[l], lw["w_dw"], lw["b_dw"], lw["ln_g"], lw["ln_b"], sb, st,
                               _tile(sb, 8))
    d_s, dconv_s, s_s = _delta_sample(proj_s, ba_s, state_delta_conv[l], state_delta[l], lw["w_dconv"],
                                      lw["arow"], lw["dtrow"], lw["g_dn"], sb, st)
    m_s = _attn_sample(proj_s, cache_mem_k, cache_mem_v, l, sb, st, 4)
    y_s = tail(xs, proj_s, c_s, d_s, m_s).reshape(sb, st, D_MODEL)

    return (y_p, y_s, mk[None], mv[None], conv_p[None], dconv_p[None], s_p[None],
            conv_s[None], dconv_s[None], s_s[None])
```

```python
import functools

import jax
import jax.numpy as jnp
from jax import lax
from jax.experimental import pallas as pl
from jax.experimental.pallas import tpu as pltpu

F32 = jnp.float32
BF16 = jnp.bfloat16

D_MODEL = 2048
N_MEM = 256
C_CONV = 1024
CONV_WIDTH = 31
DN_HEADS = 8
DN_DK = 128
DN_DV = 128
DN_CONV = 4
DN_CHUNK = 64
XA_HEADS = 4
XA_DIM = 256
QKV_W = DN_HEADS * (2 * DN_DK + DN_DV)
XW = XA_HEADS * XA_DIM

COL_GA, COL_GB = 6, 7
COL_Q, COL_K, COL_V = 8, 9, 10
COL_Z = 11
COL_XQ = 12
N_MAIN = 13 * 1024
SMALL_W = 128

VMEM_LIMIT = 52 * 1024 * 1024
MERGE_CONV_VMEM_LIMIT = 56 * 1024 * 1024

_NT = (((1,), (1,)), ((), ()))


def _dot(a, b):
    return jnp.dot(a, b, preferred_element_type=F32)


def _dot_nt(a, b):
    return lax.dot_general(a, b, _NT, preferred_element_type=F32)


def _sigmoid(x):
    return jax.nn.sigmoid(x)


def _silu(x):
    return x * jax.nn.sigmoid(x)


def _rms_rows(x, g, eps=1e-6):
    ms = jnp.mean(x * x, axis=-1, keepdims=True)
    return x * lax.rsqrt(ms + eps) * g


def _params(sem):
    return pltpu.CompilerParams(dimension_semantics=sem, vmem_limit_bytes=VMEM_LIMIT)


def _const_spec(shape):
    nd = len(shape)
    return pl.BlockSpec(shape, lambda *_: (0,) * nd, pipeline_mode=pl.Buffered(1))


def _norm_rows_to(x_ref, g_ref, u_ref, rows):
    g = g_ref[...]
    tm = x_ref.shape[0]

    def body(i, c):
        r = pl.multiple_of(i * rows, rows)
        xx = x_ref[pl.ds(r, rows), :]
        u_ref[pl.ds(r, rows), :] = _rms_rows(xx, g).astype(BF16)
        return c

    lax.fori_loop(0, tm // rows, body, 0)


def _norm_cast_kernel(x_ref, g_ref, wba_ref, u_ref, ba_ref):
    _norm_rows_to(x_ref, g_ref, u_ref, 64)
    ba_ref[...] = _dot_nt(u_ref[...], wba_ref[...].astype(BF16))


def _norm_cast(x2d, g, w_t, tm):
    m, k = x2d.shape
    return pl.pallas_call(
        _norm_cast_kernel,
        grid=(m // tm,),
        in_specs=[
            pl.BlockSpec((tm, k), lambda i: (i, 0)),
            pl.BlockSpec((1, k), lambda i: (0, 0)),
            pl.BlockSpec((pl.Element(SMALL_W), pl.Element(k)), lambda i: (i * 0 + IN_B0, 0)),
        ],
        out_specs=[pl.BlockSpec((tm, k), lambda i: (i, 0)), pl.BlockSpec((tm, SMALL_W), lambda i: (i, 0))],
        out_shape=[jax.ShapeDtypeStruct((m, k), BF16), jax.ShapeDtypeStruct((m, SMALL_W), F32)],
        compiler_params=_params(("parallel",)),
        name="norm_cast",
    )(x2d, g, w_t)


def _proj_wt_kernel(u_ref, w_ref, o_ref, wb_ref):
    @pl.when(pl.program_id(1) == 0)
    def _():
        rows = 64

        def body(i, c):
            r = pl.multiple_of(i * rows, rows)
            wb_ref[pl.ds(r, rows), :] = w_ref[pl.ds(r, rows), :].astype(BF16)
            return c

        lax.fori_loop(0, w_ref.shape[0] // rows, body, 0)

    o_ref[...] = _dot_nt(u_ref[...], wb_ref[...])


def _proj_wt(u, w_t, row_offset, n_blocks, tn, tm):
    m, k = u.shape
    return pl.pallas_call(
        _proj_wt_kernel,
        grid=(n_blocks, m // tm),
        in_specs=[
            pl.BlockSpec((tm, k), lambda j, i: (i, 0)),
            pl.BlockSpec((pl.Element(tn), pl.Element(k)), lambda j, i: (row_offset(j), 0)),
        ],
        out_specs=pl.BlockSpec((tm, tn), lambda j, i: (i, j)),
        out_shape=jax.ShapeDtypeStruct((m, n_blocks * tn), F32),
        scratch_shapes=[pltpu.VMEM((tn, k), BF16)],
        compiler_params=_params(("parallel", "arbitrary")),
        name="in_proj",
    )(u, w_t)


IN_GLU0 = 0
IN_B0 = 2 * C_CONV + QKV_W + DN_HEADS * DN_DV
IN_XQ0 = IN_B0 + 2 * DN_HEADS
IN_GATES0 = IN_XQ0 + XW


ROW_ALIGN = 16


def _main_row_offset(j):
    unit = 1024 // ROW_ALIGN
    units = jnp.where(j < 6, IN_GATES0 // ROW_ALIGN + unit * j,
                      jnp.where(j < 12, IN_GLU0 // ROW_ALIGN + unit * (j - 6), IN_XQ0 // ROW_ALIGN))
    return units * ROW_ALIGN


def _in_proj(x2d, g, w_t, tm):
    u, ba = _norm_cast(x2d, g, w_t, _tile(x2d.shape[0], 512))
    return _proj_wt(u, w_t, _main_row_offset, N_MAIN // 1024, 1024, tm), ba


def _norm_proj_kernel(x_ref, g_ref, w_ref, o_ref, u_ref):
    @pl.when(pl.program_id(1) == 0)
    def _():
        _norm_rows_to(x_ref, g_ref, u_ref, 64)

    o_ref[...] = _dot(u_ref[...], w_ref[...])


def _norm_proj(x2d, g, w, tm, tn):
    m, k = x2d.shape
    n = w.shape[1]
    return pl.pallas_call(
        _norm_proj_kernel,
        grid=(m // tm, n // tn),
        in_specs=[
            pl.BlockSpec((tm, k), lambda i, j: (i, 0)),
            pl.BlockSpec((1, k), lambda i, j: (0, 0)),
            pl.BlockSpec((k, tn), lambda i, j: (0, j)),
        ],
        out_specs=pl.BlockSpec((tm, tn), lambda i, j: (i, j)),
        out_shape=jax.ShapeDtypeStruct((m, n), F32),
        scratch_shapes=[pltpu.VMEM((tm, k), BF16)],
        compiler_params=_params(("parallel", "arbitrary")),
        name="mem_kv_proj",
    )(x2d, g, w)


def _ln_silu(c, g, b, eps=1e-5):
    mu = jnp.mean(c, axis=-1, keepdims=True)
    xc = c - mu
    var = jnp.mean(xc * xc, axis=-1, keepdims=True)
    return _silu(xc * lax.rsqrt(var + eps) * g + b)


CONV_PAD = 32
CONV_ROWS = 32
LANES = 128


def _conv_slabs():
    return [slice(s * LANES, (s + 1) * LANES) for s in range(C_CONV // LANES)]


def _conv_tile_start(ext_ref, t):
    @pl.when(t == 0)
    def _():
        ext_ref[:, 0:CONV_PAD, :] = jnp.zeros((C_CONV // LANES, CONV_PAD, LANES), F32)


def _conv_tile(ga_ref, gb_ref, wdw_ref, bdw_ref, lng_ref, lnb_ref, ext_ref, conv_ref, c_ref):
    tt = ga_ref.shape[0]
    hist = CONV_WIDTH - 1
    slabs = _conv_slabs()
    cin = ga_ref[...] * _sigmoid(gb_ref[...])
    for s, sl in enumerate(slabs):
        ext_ref[s, CONV_PAD:CONV_PAD + tt, :] = cin[:, sl]
    blocks = range(0, tt, CONV_ROWS)
    for s, sl in enumerate(slabs):
        w_all = wdw_ref[:, sl]
        accs = [jnp.broadcast_to(bdw_ref[:, sl], (CONV_ROWS, LANES)) for _ in blocks]
        for j in range(CONV_WIDTH):
            w = jnp.broadcast_to(w_all[j:j + 1, :], (CONV_ROWS, LANES))
            for i, r0 in enumerate(blocks):
                off = CONV_PAD - hist + r0 + j
                accs[i] = accs[i] + w * ext_ref[s, off:off + CONV_ROWS, :]
        for i, r0 in enumerate(blocks):
            conv_ref[r0:r0 + CONV_ROWS, sl] = accs[i]
    lng = lng_ref[...]
    lnb = lnb_ref[...]
    for r0 in blocks:
        c_ref[r0:r0 + CONV_ROWS, :] = _ln_silu(conv_ref[r0:r0 + CONV_ROWS, :], lng, lnb).astype(c_ref.dtype)


def _conv_tile_finish(st_ref, ext_ref, tt, t, n_t):
    hist = CONV_WIDTH - 1

    @pl.when(t == n_t - 1)
    def _():
        for s, sl in enumerate(_conv_slabs()):
            st_ref[0, :, sl] = ext_ref[s, CONV_PAD + tt - hist:CONV_PAD + tt, :]

    ext_ref[:, 0:CONV_PAD, :] = ext_ref[:, tt:tt + CONV_PAD, :]


def _conv_sample_kernel(ga_ref, gb_ref, st_ref, wdw_ref, bdw_ref, lng_ref, lnb_ref, c_ref, nst_ref, ext_ref,
                        *, bb, t_len):
    hist = CONV_WIDTH - 1
    cin = ga_ref[...] * _sigmoid(gb_ref[...])
    bias = bdw_ref[...]
    rows = []
    for i in range(bb):
        ext_ref[i, 0:hist, :] = st_ref[i]
        ext_ref[i, hist:hist + t_len, :] = cin[i * t_len:(i + 1) * t_len, :]
        acc = jnp.broadcast_to(bias, (t_len, C_CONV))
        for j in range(CONV_WIDTH):
            acc = acc + wdw_ref[j:j + 1, :] * ext_ref[i, j:j + t_len, :]
        rows.append(acc)
        nst_ref[i] = ext_ref[i, t_len:t_len + hist, :]
    conv = jnp.concatenate(rows, axis=0)
    c_ref[...] = _ln_silu(conv, lng_ref[...], lnb_ref[...]).astype(c_ref.dtype)


def _conv_sample(proj, state, wdw, bdw, lng, lnb, nb, t_len, bb):
    hist = CONV_WIDTH - 1
    rows = bb * t_len
    return pl.pallas_call(
        functools.partial(_conv_sample_kernel, bb=bb, t_len=t_len),
        grid=(nb // bb,),
        in_specs=[
            pl.BlockSpec((rows, C_CONV), lambda i: (i, COL_GA)),
            pl.BlockSpec((rows, C_CONV), lambda i: (i, COL_GB)),
            pl.BlockSpec((bb, hist, C_CONV), lambda i: (i, 0, 0)),
            pl.BlockSpec((CONV_WIDTH, C_CONV), lambda i: (0, 0)),
            pl.BlockSpec((1, C_CONV), lambda i: (0, 0)),
            pl.BlockSpec((1, C_CONV), lambda i: (0, 0)),
            pl.BlockSpec((1, C_CONV), lambda i: (0, 0)),
        ],
        out_specs=[
            pl.BlockSpec((rows, C_CONV), lambda i: (i, 0)),
            pl.BlockSpec((bb, hist, C_CONV), lambda i: (i, 0, 0)),
        ],
        out_shape=[
            jax.ShapeDtypeStruct((nb * t_len, C_CONV), F32),
            jax.ShapeDtypeStruct((nb, hist, C_CONV), F32),
        ],
        scratch_shapes=[pltpu.VMEM((bb, hist + 2 * t_len + 8, C_CONV), F32)],
        compiler_params=_params(("parallel",)),
        name="conv_sample",
    )(proj, proj, state, wdw, bdw, lng, lnb)


DN_PAD = 8


def _split3(x):
    hi = x.astype(BF16)
    r1 = x - hi.astype(F32)
    mid = r1.astype(BF16)
    lo = (r1 - mid.astype(F32)).astype(BF16)
    return hi, mid, lo


def _delta_run(seqs, wdc_ref, arow_ref, dtrow_ref, gdn_ref, *, c, n_valid):
    hist = DN_CONV - 1
    n_steps = 0
    while (1 << (n_steps + 1)) < n_valid:
        n_steps += 1

    row = lax.broadcasted_iota(jnp.int32, (c, SMALL_W), 0)
    valid = row < n_valid
    ii = lax.broadcasted_iota(jnp.int32, (c, c), 0)
    jj = lax.broadcasted_iota(jnp.int32, (c, c), 1)
    causal_f = jnp.where(jj <= ii, 1.0, 0.0)
    strict_f = jnp.where(jj < ii, 1.0, 0.0)
    above = (causal_f - 1.0) * 1e30
    tri = causal_f.astype(BF16)
    ones = jnp.ones((DN_DK, DN_DK), BF16)
    arow = arow_ref[...]
    dtrow = dtrow_ref[...]
    gdn = gdn_ref[...]

    def conv_slab(ch, s):
        sl = slice(s * LANES, (s + 1) * LANES)
        conv = wdc_ref[0:1, sl] * ch["win"](s, DN_PAD - hist, DN_PAD - hist + c)
        for j in range(1, DN_CONV):
            conv = conv + wdc_ref[j:j + 1, sl] * ch["win"](s, DN_PAD - hist + j, DN_PAD - hist + j + c)
        return _silu(conv)

    def unit_norm_pair(x, y):
        sq = jnp.concatenate([x * x, y * y], axis=0)
        hi = sq.astype(BF16)
        lo = (sq - hi.astype(F32)).astype(BF16)
        inv = lax.rsqrt(_dot(hi, ones) + _dot(lo, ones) + 1e-6)
        return x * inv[0:c], y * inv[c:2 * c]

    chunks = [ch for seq in seqs for ch in seq]
    for ch in chunks:
        ba = ch["ba"]
        beta = _sigmoid(ba)
        g = arow * jax.nn.softplus(ba + dtrow)
        if n_valid < c:
            beta = jnp.where(valid, beta, 0.0)
            g = jnp.where(valid, g, 0.0)
        ch["beta"] = beta
        ch["g3"] = _split3(g)
    for ch in chunks:
        ghi, gmid, glo = ch["g3"]
        gam = _dot(tri, ghi) + _dot(tri, gmid) + _dot(tri, glo)
        ch["gam"] = gam
        ch["gam_t"] = gam.T

    units = []
    heads = DN_HEADS
    for ch in chunks:
        ch["units"] = []
        for h in range(heads):
            q, k = unit_norm_pair(conv_slab(ch, h), conv_slab(ch, heads + h))
            q = q * (DN_DK ** -0.5)
            v = conv_slab(ch, 2 * heads + h)
            if n_valid < c:
                k = jnp.where(valid[:, 0:1], k, 0.0)
            beta = ch["beta"][:, h:h + 1]
            gam_c = ch["gam"][:, 8 + h:9 + h]
            gam_r = ch["gam_t"][8 + h:9 + h, :]
            g_last = gam_r[:, c - 1:c]
            kb = k * beta
            u = dict(ch=ch, h=h, q=q, k=k, kb=kb, vb=v * beta, gam_c=gam_c, g_last=g_last,
                     dec=jnp.exp((gam_c - gam_r) * causal_f + above))
            ch["units"].append(u)
            units.append(u)

    for u in units:
        u["gmat"] = _dot_nt(jnp.concatenate([u["kb"], u["q"]], axis=0).astype(BF16), u["k"].astype(BF16))
    for u in units:
        gmat = u.pop("gmat")
        u["e"] = -(gmat[0:c] * (u["dec"] * strict_f))
        u["p"] = u["e"]
        u["qk"] = gmat[c:2 * c] * u.pop("dec")
    if n_steps:
        for u in units:
            p16 = u["p"].astype(BF16)
            u["p"] = _dot(p16, p16)
        for _ in range(n_steps - 1):
            for u in units:
                p16 = u["p"].astype(BF16)
                u["pe"] = _dot(jnp.concatenate([u["p"], u["e"]], axis=0).astype(BF16), p16)
            for u in units:
                pe = u.pop("pe")
                u["e"] = u["e"] + u["p"] + pe[c:2 * c]
                u["p"] = pe[0:c]
        for u in units:
            u["ep"] = _dot(u["e"].astype(BF16), u["p"].astype(BF16))
        for u in units:
            u["e"] = u["e"] + u.pop("p") + u.pop("ep")
    for u in units:
        eg = jnp.exp(u["gam_c"])
        rhs = jnp.concatenate([u.pop("vb"), u.pop("kb") * eg], axis=1)
        u["rhs"] = rhs
        u["qg"] = u.pop("q") * eg
        u["sol"] = _dot(u.pop("e").astype(BF16), rhs.astype(BF16))
    for u in units:
        sol = u.pop("rhs") + u.pop("sol")
        u["u"] = sol[:, 0:DN_DV]
        u["lhs16"] = jnp.concatenate([sol[:, DN_DV:DN_DV + DN_DK], u.pop("qg")], axis=0).astype(BF16)
        kg = u.pop("k") * jnp.exp(u["g_last"] - u["gam_c"])
        u["qk_kgt16"] = jnp.concatenate([u.pop("qk"), kg.T], axis=0).astype(BF16)
        u["decay"] = jnp.exp(u["g_last"])

    for j in range(len(seqs[0])):
        cur = [u for seq in seqs for u in seq[j]["units"]]
        for u in cur:
            u["s_old"] = u["ch"]["s"][u["h"]]
            u["xs"] = _dot(u["lhs16"], u["s_old"].astype(BF16))
        for u in cur:
            xs = u.pop("xs")
            u["w16"] = (u["u"] - xs[0:c]).astype(BF16)
            u["o"] = xs[c:2 * c]
        for u in cur:
            u["o2ds"] = _dot(u["qk_kgt16"], u["w16"])
        for u in cur:
            ch, h = u["ch"], u["h"]
            o2ds = u.pop("o2ds")
            ch["s"][h] = u["s_old"] * u["decay"] + o2ds[c:c + DN_DK]
            zh = ch["z"][:, h * DN_DV:(h + 1) * DN_DV]
            ch["store_o"](h, _rms_rows(u["o"] + o2ds[0:c], gdn) * _silu(zh))


DP_CHUNKS = 2


def _delta_prompt_kernel(q_ref, k_ref, v_ref, z_ref, ba_ref, wdc_ref, arow_ref, dtrow_ref, gdn_ref,
                         o_ref, nd_ref, ns_ref, ext_ref, s_ref):
    t = pl.program_id(1)
    rows = q_ref.shape[0]
    c = DN_CHUNK
    hist = DN_CONV - 1

    @pl.when(t == 0)
    def _():
        ext_ref[:, 0:DN_PAD, :] = jnp.zeros((QKV_W // LANES, DN_PAD, LANES), F32)
        s_ref[...] = jnp.zeros(s_ref.shape, F32)

    n_slab = QKV_W // LANES
    for i, ref in enumerate((q_ref, k_ref, v_ref)):
        for s in range(1024 // LANES):
            ext_ref[i * (1024 // LANES) + s, DN_PAD:DN_PAD + rows, :] = ref[:, s * LANES:(s + 1) * LANES]

    seq = []
    for g in range(rows // c):
        def win(s, j0, j1, g=g):
            return ext_ref[s, g * c + j0:g * c + j1, :]

        def store_o(h, val, g=g):
            o_ref[g * c:(g + 1) * c, h * DN_DV:(h + 1) * DN_DV] = val.astype(o_ref.dtype)

        seq.append(dict(win=win, z=z_ref[g * c:(g + 1) * c, :], ba=ba_ref[g * c:(g + 1) * c, :], s=s_ref,
                        store_o=store_o))
    _delta_run([seq], wdc_ref, arow_ref, dtrow_ref, gdn_ref, c=c, n_valid=c)

    @pl.when(t == pl.num_programs(1) - 1)
    def _():
        for s in range(n_slab):
            nd_ref[0, :, s * LANES:(s + 1) * LANES] = ext_ref[s, DN_PAD + rows - hist:DN_PAD + rows, :]
        ns_ref[0] = s_ref[...]

    ext_ref[:, DN_PAD - hist:DN_PAD, :] = ext_ref[:, DN_PAD + rows - hist:DN_PAD + rows, :]


def _delta_prompt(proj, ba, wdc, arow, dtrow, gdn, nb, t_len):
    rows = DN_CHUNK * (DP_CHUNKS if t_len % (DN_CHUNK * DP_CHUNKS) == 0 else 1)
    nt = t_len // rows
    row_spec = lambda col: pl.BlockSpec((rows, 1024), lambda b, t: (b * nt + t, col))
    small = lambda shape: pl.BlockSpec(shape, lambda b, t: (0,) * len(shape))
    return pl.pallas_call(
        _delta_prompt_kernel,
        grid=(nb, nt),
        in_specs=[
            row_spec(COL_Q), row_spec(COL_K), row_spec(COL_V), row_spec(COL_Z),
            pl.BlockSpec((rows, SMALL_W), lambda b, t: (b * nt + t, 0)),
            small((DN_CONV, QKV_W)), small((1, SMALL_W)), small((1, SMALL_W)), small((1, DN_DV)),
        ],
        out_specs=[
            pl.BlockSpec((rows, 1024), lambda b, t: (b * nt + t, 0)),
            pl.BlockSpec((1, DN_CONV - 1, QKV_W), lambda b, t: (b, 0, 0)),
            pl.BlockSpec((1, DN_HEADS, DN_DK, DN_DV), lambda b, t: (b, 0, 0, 0)),
        ],
        out_shape=[
            jax.ShapeDtypeStruct((nb * t_len, 1024), BF16),
            jax.ShapeDtypeStruct((nb, DN_CONV - 1, QKV_W), F32),
            jax.ShapeDtypeStruct((nb, DN_HEADS, DN_DK, DN_DV), F32),
        ],
        scratch_shapes=[pltpu.VMEM((QKV_W // LANES, DN_PAD + rows, LANES), F32),
                        pltpu.VMEM((DN_HEADS, DN_DK, DN_DV), F32)],
        compiler_params=_params(("parallel", "arbitrary")),
        name="delta_prompt",
    )(proj, proj, proj, proj, ba, wdc, arow, dtrow, gdn)


DS_ROWS = 8
DS_GROUPS = 2


def _delta_sample_kernel(q_ref, k_ref, v_ref, z_ref, ba_ref, dst_ref, s0_ref, wdc_ref, arow_ref, dtrow_ref,
                         gdn_ref, o_ref, nd_ref, ns_ref, ext_ref, *, t_len):
    hist = DN_CONV - 1
    per = DS_ROWS // t_len
    n_slab = QKV_W // LANES
    ns_ref[...] = s0_ref[...]
    seqs = []
    for i in range(dst_ref.shape[0]):
        g0 = (i // per) * DS_ROWS
        r0 = (i % per) * t_len
        ext_ref[i] = jnp.zeros((n_slab, DN_PAD + DS_ROWS, LANES), F32)
        for s in range(n_slab):
            sl = slice(s * LANES, (s + 1) * LANES)
            ext_ref[i, s, DN_PAD - hist:DN_PAD, :] = dst_ref[i, :, sl]
            src = (q_ref, k_ref, v_ref)[s // (1024 // LANES)]
            col = (s % (1024 // LANES)) * LANES
            ext_ref[i, s, DN_PAD:DN_PAD + t_len, :] = src[g0 + r0:g0 + r0 + t_len, col:col + LANES]
            nd_ref[i, :, sl] = ext_ref[i, s, DN_PAD + t_len - hist:DN_PAD + t_len, :]
        z = z_ref[g0:g0 + DS_ROWS, :]
        ba = ba_ref[g0:g0 + DS_ROWS, :]
        if r0:
            z = pltpu.roll(z, DS_ROWS - r0, 0)
            ba = pltpu.roll(ba, DS_ROWS - r0, 0)

        def win(s, j0, j1, i=i):
            return ext_ref[i, s, j0:j1, :]

        def store_o(h, val, g0=g0, r0=r0):
            o_ref[g0 + r0:g0 + r0 + t_len, h * DN_DV:(h + 1) * DN_DV] = val[0:t_len].astype(o_ref.dtype)

        seqs.append([dict(win=win, z=z, ba=ba, s=ns_ref.at[i], store_o=store_o)])
    _delta_run(seqs, wdc_ref, arow_ref, dtrow_ref, gdn_ref, c=DS_ROWS, n_valid=t_len)


def _delta_sample(proj, ba, dstate, s0, wdc, arow, dtrow, gdn, nb, t_len):
    per = DS_ROWS // t_len
    groups = DS_GROUPS if nb % (per * DS_GROUPS) == 0 else 1
    bb = per * groups
    rows = DS_ROWS * groups
    hist = DN_CONV - 1
    row_spec = lambda col: pl.BlockSpec((rows, 1024), lambda i: (i, col))
    small = lambda shape: pl.BlockSpec(shape, lambda i: (0,) * len(shape))
    return pl.pallas_call(
        functools.partial(_delta_sample_kernel, t_len=t_len),
        grid=(nb // bb,),
        in_specs=[
            row_spec(COL_Q), row_spec(COL_K), row_spec(COL_V), row_spec(COL_Z),
            pl.BlockSpec((rows, SMALL_W), lambda i: (i, 0)),
            pl.BlockSpec((bb, hist, QKV_W), lambda i: (i, 0, 0)),
            pl.BlockSpec((bb, DN_HEADS, DN_DK, DN_DV), lambda i: (i, 0, 0, 0)),
            small((DN_CONV, QKV_W)), small((1, SMALL_W)), small((1, SMALL_W)), small((1, DN_DV)),
        ],
        out_specs=[
            pl.BlockSpec((rows, 1024), lambda i: (i, 0)),
            pl.BlockSpec((bb, hist, QKV_W), lambda i: (i, 0, 0)),
            pl.BlockSpec((bb, DN_HEADS, DN_DK, DN_DV), lambda i: (i, 0, 0, 0)),
        ],
        out_shape=[
            jax.ShapeDtypeStruct((nb * t_len, 1024), F32),
            jax.ShapeDtypeStruct((nb, hist, QKV_W), F32),
            jax.ShapeDtypeStruct((nb, DN_HEADS, DN_DK, DN_DV), F32),
        ],
        scratch_shapes=[pltpu.VMEM((bb, QKV_W // LANES, DN_PAD + DS_ROWS, LANES), F32)],
        compiler_params=_params(("parallel",)),
        name="delta_sample",
    )(proj, proj, proj, proj, ba, dstate, s0, wdc, arow, dtrow, gdn)


def _softmax_rows(s):
    m = jnp.max(s, axis=-1, keepdims=True)
    p = jnp.exp(s - m)
    return p / jnp.sum(p, axis=-1, keepdims=True)


def _attn_pairs(pairs):
    scores = [_dot_nt(q16, k.astype(BF16)) * (XA_DIM ** -0.5) for q16, k, _ in pairs]
    probs = [_softmax_rows(s).astype(BF16) for s in scores]
    return [_dot(p, v.astype(BF16)) for p, (_, _, v) in zip(probs, pairs)]


def _attn_prompt_kernel(xq_ref, k_ref, v_ref, o_ref):
    sls = [slice(h * XA_DIM, (h + 1) * XA_DIM) for h in range(XA_HEADS)]
    outs = _attn_pairs([(xq_ref[:, sl].astype(BF16), k_ref[:, sl], v_ref[:, sl]) for sl in sls])
    for sl, o in zip(sls, outs):
        o_ref[:, sl] = o.astype(o_ref.dtype)


def _attn_prompt(proj, mkv, nb, t_len, tq):
    nt = t_len // tq
    return pl.pallas_call(
        _attn_prompt_kernel,
        grid=(nb, nt),
        in_specs=[
            pl.BlockSpec((tq, XW), lambda b, t: (b * nt + t, COL_XQ)),
            pl.BlockSpec((N_MEM, XW), lambda b, t: (b, 0)),
            pl.BlockSpec((N_MEM, XW), lambda b, t: (b, 1)),
        ],
        out_specs=pl.BlockSpec((tq, XW), lambda b, t: (b * nt + t, 0)),
        out_shape=jax.ShapeDtypeStruct((nb * t_len, XW), BF16),
        compiler_params=_params(("parallel", "arbitrary")),
        name="attn_prompt",
    )(proj, mkv, mkv)


def _attn_sample_kernel(xq_ref, k_hbm, v_hbm, o_ref, kbuf, vbuf, sem, *, layer, bb, t_len):
    i = pl.program_id(0)
    slot = i % 2

    def copies(step, slot_):
        out = []
        for h in range(XA_HEADS):
            for kv, (hbm, buf) in enumerate(((k_hbm, kbuf), (v_hbm, vbuf))):
                out.append(pltpu.make_async_copy(hbm.at[layer, pl.ds(step * bb, bb), :, h, :], buf.at[slot_, h],
                                                 sem.at[kv, slot_, h]))
        return out

    @pl.when(i == 0)
    def _():
        for cp in copies(0, 0):
            cp.start()

    @pl.when(i + 1 < pl.num_programs(0))
    def _():
        for cp in copies(i + 1, 1 - slot):
            cp.start()

    for cp in copies(i, slot):
        cp.wait()

    per = DS_ROWS // t_len
    rows = lax.broadcasted_iota(jnp.int32, (DS_ROWS, XA_DIM), 0)
    pairs = []
    for b in range(bb):
        r0 = (b // per) * DS_ROWS
        for h in range(XA_HEADS):
            q16 = xq_ref[r0:r0 + DS_ROWS, h * XA_DIM:(h + 1) * XA_DIM].astype(BF16)
            pairs.append((q16, kbuf[slot, h, b], vbuf[slot, h, b]))
    outs = _attn_pairs(pairs)
    for grp in range(bb // per):
        r0 = grp * DS_ROWS
        for h in range(XA_HEADS):
            o = outs[(grp * per) * XA_HEADS + h]
            for s in range(1, per):
                o = jnp.where(rows < s * t_len, o, outs[(grp * per + s) * XA_HEADS + h])
            o_ref[r0:r0 + DS_ROWS, h * XA_DIM:(h + 1) * XA_DIM] = o.astype(o_ref.dtype)


def _attn_sample(proj, ck, cv, layer, nb, t_len, bb):
    rows = bb * t_len
    buf = pltpu.VMEM((2, XA_HEADS, bb, N_MEM, XA_DIM), F32)
    return pl.pallas_call(
        functools.partial(_attn_sample_kernel, layer=layer, bb=bb, t_len=t_len),
        grid=(nb // bb,),
        in_specs=[pl.BlockSpec((rows, XW), lambda i: (i, COL_XQ)), pl.BlockSpec(memory_space=pl.ANY),
                  pl.BlockSpec(memory_space=pl.ANY)],
        out_specs=pl.BlockSpec((rows, XW), lambda i: (i, 0)),
        out_shape=jax.ShapeDtypeStruct((nb * t_len, XW), F32),
        scratch_shapes=[buf, buf, pltpu.SemaphoreType.DMA((2, 2, XA_HEADS))],
        compiler_params=_params(("arbitrary",)),
        name="attn_sample",
    )(proj, ck, cv)


def _merge_others(d_ref, m_ref, gd_ref, gx_ref, wd_ref, wm_ref):
    mix = _sigmoid(gd_ref[...]) * _dot(d_ref[...].astype(BF16), wd_ref[...])
    return mix + _sigmoid(gx_ref[...]) * _dot(m_ref[...].astype(BF16), wm_ref[...])


def _merge_finish(mix, c16, gc_ref, x_ref, wc_ref, bc_ref, wo_ref, gpm_ref, h_ref):
    mix = mix + _sigmoid(gc_ref[...]) * (_dot(c16, wc_ref[...]) + bc_ref[...])
    out = _dot(mix.astype(BF16), wo_ref[...])
    h_ref[...] = x_ref[...] + _rms_rows(out, gpm_ref[...])


def _merge_kernel(c_ref, d_ref, m_ref, gc_ref, gd_ref, gx_ref, x_ref, wc_ref, bc_ref, wd_ref, wm_ref, wo_ref,
                  gpm_ref, h_ref):
    mix = _merge_others(d_ref, m_ref, gd_ref, gx_ref, wd_ref, wm_ref)
    _merge_finish(mix, c_ref[...].astype(BF16), gc_ref, x_ref, wc_ref, bc_ref, wo_ref, gpm_ref, h_ref)


def _merge_conv_kernel(ga_ref, gb_ref, wdw_ref, bdw_ref, lng_ref, lnb_ref, d_ref, m_ref, gc_ref, gd_ref, gx_ref,
                       x_ref, wc_ref, bc_ref, wd_ref, wm_ref, wo_ref, gpm_ref, h_ref, st_ref, ext_ref, conv_ref,
                       c_ref):
    t = pl.program_id(1)
    _conv_tile_start(ext_ref, t)
    mix = _merge_others(d_ref, m_ref, gd_ref, gx_ref, wd_ref, wm_ref)
    _conv_tile(ga_ref, gb_ref, wdw_ref, bdw_ref, lng_ref, lnb_ref, ext_ref, conv_ref, c_ref)
    _merge_finish(mix, c_ref[...], gc_ref, x_ref, wc_ref, bc_ref, wo_ref, gpm_ref, h_ref)
    _conv_tile_finish(st_ref, ext_ref, ga_ref.shape[0], t, pl.num_programs(1))


_MERGE_WEIGHT_SHAPES = ((C_CONV, D_MODEL), (1, D_MODEL), (1024, D_MODEL), (XW, D_MODEL), (D_MODEL, D_MODEL),
                        (1, D_MODEL))


def _merge(c, d, m, proj, x2d, wc, bc, wd, wm, wo, gpm, tm):
    rows = x2d.shape[0]
    row = lambda width, col=0: pl.BlockSpec((tm, width), lambda i: (i, col))
    return pl.pallas_call(
        _merge_kernel,
        grid=(rows // tm,),
        in_specs=[
            row(1024), row(1024), row(1024),
            row(D_MODEL, 0), row(D_MODEL, 1), row(D_MODEL, 2),
            row(D_MODEL),
        ] + [_const_spec(shape) for shape in _MERGE_WEIGHT_SHAPES],
        out_specs=row(D_MODEL),
        out_shape=jax.ShapeDtypeStruct((rows, D_MODEL), F32),
        compiler_params=_params(("parallel",)),
        name="merge",
    )(c, d, m, proj, proj, proj, x2d, wc, bc, wd, wm, wo, gpm)


def _merge_conv(d, m, proj, x2d, wdw, bdw, lng, lnb, wc, bc, wd, wm, wo, gpm, nb, t_len, tt):
    nt = t_len // tt
    row = lambda width, col=0: pl.BlockSpec((tt, width), lambda b, t: (b * nt + t, col))
    vec = lambda n: pl.BlockSpec((1, n), lambda b, t: (0, 0))
    return pl.pallas_call(
        _merge_conv_kernel,
        grid=(nb, nt),
        in_specs=[
            row(C_CONV, COL_GA), row(C_CONV, COL_GB),
            pl.BlockSpec((CONV_WIDTH, C_CONV), lambda b, t: (0, 0)), vec(C_CONV), vec(C_CONV), vec(C_CONV),
            row(1024), row(1024),
            row(D_MODEL, 0), row(D_MODEL, 1), row(D_MODEL, 2),
            row(D_MODEL),
        ] + [_const_spec(shape) for shape in _MERGE_WEIGHT_SHAPES],
        out_specs=[row(D_MODEL), pl.BlockSpec((1, CONV_WIDTH - 1, C_CONV), lambda b, t: (b, 0, 0))],
        out_shape=[
            jax.ShapeDtypeStruct((nb * t_len, D_MODEL), F32),
            jax.ShapeDtypeStruct((nb, CONV_WIDTH - 1, C_CONV), F32),
        ],
        scratch_shapes=[pltpu.VMEM((C_CONV // LANES, CONV_PAD + tt, LANES), F32), pltpu.VMEM((tt, C_CONV), F32),
                        pltpu.VMEM((tt, C_CONV), BF16)],
        compiler_params=pltpu.CompilerParams(dimension_semantics=("parallel", "arbitrary"),
                                             vmem_limit_bytes=MERGE_CONV_VMEM_LIMIT),
        name="merge_conv",
    )(proj, proj, wdw, bdw, lng, lnb, d, m, proj, proj, proj, x2d, wc, bc, wd, wm, wo, gpm)


FFN_TF = 512
FFN_TN = 512


def _ffn_kernel(h_ref, g1_ref, wg_ref, wu_ref, wd_ref, g2_ref, y_ref, fn_ref, act_ref, *, n_up):
    f = pl.program_id(1)

    @pl.when(f == 0)
    def _():
        _norm_rows_to(h_ref, g1_ref, fn_ref, 64)

    @pl.when(f < n_up)
    def _():
        fn = fn_ref[...]
        col = pl.multiple_of(f * FFN_TF, FFN_TF)
        act_ref[:, pl.ds(col, FFN_TF)] = (_silu(_dot(fn, wg_ref[...])) * _dot(fn, wu_ref[...])).astype(BF16)

    @pl.when(f >= n_up)
    def _():
        col = pl.multiple_of((f - n_up) * FFN_TN, FFN_TN)
        y_ref[:, pl.ds(col, FFN_TN)] = _dot(act_ref[...], wd_ref[...])

    @pl.when(f == pl.num_programs(1) - 1)
    def _():
        g2 = g2_ref[...]
        rows = 64

        def body(i, c):
            r = pl.multiple_of(i * rows, rows)
            y_ref[pl.ds(r, rows), :] = h_ref[pl.ds(r, rows), :] + _rms_rows(y_ref[pl.ds(r, rows), :], g2)
            return c

        lax.fori_loop(0, h_ref.shape[0] // rows, body, 0)


def _ffn(h2d, g1, wg, wu, wd, g2, tm):
    rows = h2d.shape[0]
    d_ff = wg.shape[1]
    n_up = d_ff // FFN_TF
    n_down = D_MODEL // FFN_TN
    return pl.pallas_call(
        functools.partial(_ffn_kernel, n_up=n_up),
        grid=(rows // tm, n_up + n_down),
        in_specs=[
            pl.BlockSpec((tm, D_MODEL), lambda i, f: (i, 0)),
            pl.BlockSpec((1, D_MODEL), lambda i, f: (0, 0)),
            pl.BlockSpec((D_MODEL, FFN_TF), lambda i, f: (0, jnp.minimum(f, n_up - 1))),
            pl.BlockSpec((D_MODEL, FFN_TF), lambda i, f: (0, jnp.minimum(f, n_up - 1))),
            pl.BlockSpec((d_ff, FFN_TN), lambda i, f: (0, jnp.maximum(f - n_up, 0))),
            pl.BlockSpec((1, D_MODEL), lambda i, f: (0, 0)),
        ],
        out_specs=pl.BlockSpec((tm, D_MODEL), lambda i, f: (i, 0)),
        out_shape=jax.ShapeDtypeStruct((rows, D_MODEL), F32),
        scratch_shapes=[pltpu.VMEM((tm, D_MODEL), BF16), pltpu.VMEM((tm, d_ff), BF16)],
        compiler_params=_params(("parallel", "arbitrary")),
        name="ffn",
    )(h2d, g1, wg, wu, wd, g2)


def _row(v):
    return v.reshape(1, -1).astype(F32)


def _layer_weights(w_in, w_dw, b_dw, ln_g, ln_b, w_p_conv, b_p_conv, w_dconv, a_log, dt_bias, g_dn_norm,
                   w_p_delta, w_p_mem, w_o, w_gate, w_up, w_down):
    lane_pad = (0, SMALL_W - 2 * DN_HEADS)
    arow = jnp.pad(jnp.concatenate([jnp.zeros((DN_HEADS,), F32), -jnp.exp(a_log.astype(F32))]), lane_pad)
    dtrow = jnp.pad(jnp.concatenate([jnp.zeros((DN_HEADS,), F32), dt_bias.astype(F32)]), lane_pad)
    return dict(
        w_t=jnp.swapaxes(w_in, 0, 1), w_dw=w_dw, b_dw=_row(b_dw), ln_g=_row(ln_g), ln_b=_row(ln_b),
        w_p_conv=w_p_conv.astype(BF16), b_p_conv=_row(b_p_conv), w_dconv=w_dconv, arow=_row(arow),
        dtrow=_row(dtrow), g_dn=_row(g_dn_norm), w_p_delta=w_p_delta.astype(BF16),
        w_p_mem=w_p_mem.astype(BF16), w_o=w_o.astype(BF16), w_gate=w_gate.astype(BF16),
        w_up=w_up.astype(BF16), w_down=w_down.astype(BF16))


def _tile(n, pref):
    return pref if n % pref == 0 else n


def kernel(x_prompt, x_sample, mem_prompt, cache_mem_k, cache_mem_v, state_conv, state_delta_conv, state_delta, g_pre_mix, w_in, w_dw, b_dw, ln_g, ln_b, w_p_conv, b_p_conv, w_dconv, a_log, dt_bias, g_dn_norm, w_p_delta, g_mem, w_mem_kv, w_p_mem, w_o, g_post_mix, g_pre_ffn, w_gate, w_up, w_down, g_post_ffn):
    depth = w_in.shape[0]
    assert depth == 1, "single-layer stack"
    l = 0
    pb, pt, _ = x_prompt.shape
    sb, st, _ = x_sample.shape
    assert pt % DN_CHUNK == 0 and st <= DS_ROWS and DS_ROWS % st == 0

    lw = _layer_weights(w_in[l], w_dw[l], b_dw[l], ln_g[l], ln_b[l], w_p_conv[l], b_p_conv[l], w_dconv[l],
                        a_log[l], dt_bias[l], g_dn_norm[l], w_p_delta[l], w_p_mem[l], w_o[l], w_gate[l],
                        w_up[l], w_down[l])
    gpre, gpm, gff1, gff2 = _row(g_pre_mix[l]), _row(g_post_mix[l]), _row(g_pre_ffn[l]), _row(g_post_ffn[l])

    merge_w = (lw["w_p_conv"], lw["b_p_conv"], lw["w_p_delta"], lw["w_p_mem"], lw["w_o"], gpm)

    def ffn(h):
        return _ffn(h, gff1, lw["w_gate"], lw["w_up"], lw["w_down"], gff2, _tile(h.shape[0], 512))

    xp = x_prompt.reshape(pb * pt, D_MODEL)
    mkv = _norm_proj(mem_prompt.reshape(pb * N_MEM, D_MODEL), _row(g_mem[l]), w_mem_kv[l].astype(BF16),
                     _tile(pb * N_MEM, 512), 1024)
    proj_p, ba_p = _in_proj(xp, gpre, lw["w_t"], _tile(pb * pt, 1024))
    d_p, dconv_p, s_p = _delta_prompt(proj_p, ba_p, lw["w_dconv"], lw["arow"], lw["dtrow"], lw["g_dn"], pb, pt)
    m_p = _attn_prompt(proj_p, mkv, pb, pt, _tile(pt, 512))
    h_p, conv_p = _merge_conv(d_p, m_p, proj_p, xp, lw["w_dw"], lw["b_dw"], lw["ln_g"], lw["ln_b"], *merge_w, pb, pt,
                              _tile(pt, 256))
    y_p = ffn(h_p).reshape(pb, pt, D_MODEL)
    mk = mkv[:, 0:XW].reshape(pb, N_MEM, XA_HEADS, XA_DIM)
    mv = mkv[:, XW:2 * XW].reshape(pb, N_MEM, XA_HEADS, XA_DIM)

    xs = x_sample.reshape(sb * st, D_MODEL)
    proj_s, ba_s = _in_proj(xs, gpre, lw["w_t"], _tile(sb * st, 512))
    c_s, conv_s = _conv_sample(proj_s, state_conv[l], lw["w_dw"], lw["b_dw"], lw["ln_g"], lw["ln_b"], sb, st,
                               _tile(sb, 8))
    d_s, dconv_s, s_s = _delta_sample(proj_s, ba_s, state_delta_conv[l], state_delta[l], lw["w_dconv"],
                                      lw["arow"], lw["dtrow"], lw["g_dn"], sb, st)
    m_s = _attn_sample(proj_s, cache_mem_k, cache_mem_v, l, sb, st, 4)
    h_s = _merge(c_s, d_s, m_s, proj_s, xs, *merge_w, _tile(sb * st, 256))
    y_s = ffn(h_s).reshape(sb, st, D_MODEL)

    return (y_p, y_s, mk[None], mv[None], conv_p[None], dconv_p[None], s_p[None],
            conv_s[None], dconv_s[None], s_s[None])
```

```python
import functools

import jax
import jax.numpy as jnp
from jax import lax
from jax.experimental import pallas as pl
from jax.experimental.pallas import tpu as pltpu

F32 = jnp.float32
BF16 = jnp.bfloat16

D_MODEL = 2048
N_MEM = 256
C_CONV = 1024
CONV_WIDTH = 31
DN_HEADS = 8
DN_DK = 128
DN_DV = 128
DN_CONV = 4
DN_CHUNK = 64
XA_HEADS = 4
XA_DIM = 256
QKV_W = DN_HEADS * (2 * DN_DK + DN_DV)
XW = XA_HEADS * XA_DIM

COL_GA, COL_GB = 6, 7
COL_Q, COL_K, COL_V = 8, 9, 10
COL_Z = 11
COL_XQ = 12
N_MAIN = 13 * 1024
SMALL_W = 128

VMEM_LIMIT = 52 * 1024 * 1024
MERGE_CONV_VMEM_LIMIT = 56 * 1024 * 1024

_NT = (((1,), (1,)), ((), ()))


def _dot(a, b):
    return jnp.dot(a, b, preferred_element_type=F32)


def _dot_nt(a, b):
    return lax.dot_general(a, b, _NT, preferred_element_type=F32)


def _sigmoid(x):
    return jax.nn.sigmoid(x)


def _silu(x):
    return x * jax.nn.sigmoid(x)


def _rms_rows(x, g, eps=1e-6):
    ms = jnp.mean(x * x, axis=-1, keepdims=True)
    return x * lax.rsqrt(ms + eps) * g


def _params(sem):
    return pltpu.CompilerParams(dimension_semantics=sem, vmem_limit_bytes=VMEM_LIMIT)


def _const_spec(shape):
    nd = len(shape)
    return pl.BlockSpec(shape, lambda *_: (0,) * nd, pipeline_mode=pl.Buffered(1))


def _norm_rows_to(x_ref, g_ref, u_ref, rows):
    g = g_ref[...]
    tm = x_ref.shape[0]

    def body(i, c):
        r = pl.multiple_of(i * rows, rows)
        xx = x_ref[pl.ds(r, rows), :]
        u_ref[pl.ds(r, rows), :] = _rms_rows(xx, g).astype(BF16)
        return c

    lax.fori_loop(0, tm // rows, body, 0)


def _norm_cast_kernel(x_ref, g_ref, wba_ref, u_ref, ba_ref):
    _norm_rows_to(x_ref, g_ref, u_ref, 64)
    ba_ref[...] = _dot_nt(u_ref[...], wba_ref[...].astype(BF16))


def _norm_cast(x2d, g, w_t, tm):
    m, k = x2d.shape
    return pl.pallas_call(
        _norm_cast_kernel,
        grid=(m // tm,),
        in_specs=[
            pl.BlockSpec((tm, k), lambda i: (i, 0)),
            pl.BlockSpec((1, k), lambda i: (0, 0)),
            pl.BlockSpec((pl.Element(SMALL_W), pl.Element(k)), lambda i: (i * 0 + IN_B0, 0)),
        ],
        out_specs=[pl.BlockSpec((tm, k), lambda i: (i, 0)), pl.BlockSpec((tm, SMALL_W), lambda i: (i, 0))],
        out_shape=[jax.ShapeDtypeStruct((m, k), BF16), jax.ShapeDtypeStruct((m, SMALL_W), F32)],
        compiler_params=_params(("parallel",)),
        name="norm_cast",
    )(x2d, g, w_t)


def _proj_wt_kernel(*refs, n_cast):
    u_ref, w_ref = refs[0:2]
    side_in = refs[2:2 + n_cast]
    o_ref = refs[2 + n_cast]
    side_out = refs[3 + n_cast:3 + 2 * n_cast]
    wb_ref = refs[3 + 2 * n_cast]

    @pl.when(pl.program_id(1) == 0)
    def _():
        rows = 64

        def body(i, c):
            r = pl.multiple_of(i * rows, rows)
            wb_ref[pl.ds(r, rows), :] = w_ref[pl.ds(r, rows), :].astype(BF16)
            return c

        lax.fori_loop(0, w_ref.shape[0] // rows, body, 0)

    o_ref[...] = _dot_nt(u_ref[...], wb_ref[...])
    for src, dst in zip(side_in, side_out):
        dst[...] = src[...].astype(BF16)


def _proj_wt(u, w_t, row_offset, n_blocks, tn, tm, side=()):
    m, k = u.shape
    n_i = m // tm
    n_steps = n_blocks * n_i

    count = 1
    while side and 2 * count * len(side) <= n_steps and 2 * count <= SIDE_BLOCKS_MAX:
        count *= 2

    def side_spec(arr, start):
        rows = arr.shape[0] // count
        assert arr.shape[0] % count == 0 and rows % 16 == 0
        return pl.BlockSpec((rows, arr.shape[1]), lambda j, i: (jnp.clip(j * n_i + i - start, 0, count - 1), 0))

    side_specs = [side_spec(arr, n * count) for n, arr in enumerate(side)]
    outs = pl.pallas_call(
        functools.partial(_proj_wt_kernel, n_cast=len(side)),
        grid=(n_blocks, n_i),
        in_specs=[
            pl.BlockSpec((tm, k), lambda j, i: (i, 0)),
            pl.BlockSpec((pl.Element(tn), pl.Element(k)), lambda j, i: (row_offset(j), 0)),
        ] + side_specs,
        out_specs=[pl.BlockSpec((tm, tn), lambda j, i: (i, j))] + side_specs,
        out_shape=[jax.ShapeDtypeStruct((m, n_blocks * tn), F32)]
        + [jax.ShapeDtypeStruct(arr.shape, BF16) for arr in side],
        scratch_shapes=[pltpu.VMEM((tn, k), BF16)],
        compiler_params=_params(("arbitrary", "arbitrary")),
        name="in_proj",
    )(u, w_t, *side)
    return outs[0], list(outs[1:])


IN_GLU0 = 0
IN_B0 = 2 * C_CONV + QKV_W + DN_HEADS * DN_DV
IN_XQ0 = IN_B0 + 2 * DN_HEADS
IN_GATES0 = IN_XQ0 + XW


SIDE_BLOCKS_MAX = 32
ROW_ALIGN = 16


def _main_row_offset(j):
    unit = 1024 // ROW_ALIGN
    units = jnp.where(j < 6, IN_GATES0 // ROW_ALIGN + unit * j,
                      jnp.where(j < 12, IN_GLU0 // ROW_ALIGN + unit * (j - 6), IN_XQ0 // ROW_ALIGN))
    return units * ROW_ALIGN


def _in_proj(x2d, g, w_t, tm, side=()):
    u, ba = _norm_cast(x2d, g, w_t, _tile(x2d.shape[0], 512))
    proj, side16 = _proj_wt(u, w_t, _main_row_offset, N_MAIN // 1024, 1024, tm, side)
    return proj, ba, side16


def _norm_proj_kernel(x_ref, g_ref, w_ref, o_ref, u_ref):
    @pl.when(pl.program_id(1) == 0)
    def _():
        _norm_rows_to(x_ref, g_ref, u_ref, 64)

    o_ref[...] = _dot(u_ref[...], w_ref[...])


def _norm_proj(x2d, g, w, tm, tn):
    m, k = x2d.shape
    n = w.shape[1]
    return pl.pallas_call(
        _norm_proj_kernel,
        grid=(m // tm, n // tn),
        in_specs=[
            pl.BlockSpec((tm, k), lambda i, j: (i, 0)),
            pl.BlockSpec((1, k), lambda i, j: (0, 0)),
            pl.BlockSpec((k, tn), lambda i, j: (0, j)),
        ],
        out_specs=pl.BlockSpec((tm, tn), lambda i, j: (i, j)),
        out_shape=jax.ShapeDtypeStruct((m, n), F32),
        scratch_shapes=[pltpu.VMEM((tm, k), BF16)],
        compiler_params=_params(("parallel", "arbitrary")),
        name="mem_kv_proj",
    )(x2d, g, w)


def _ln_silu(c, g, b, eps=1e-5):
    mu = jnp.mean(c, axis=-1, keepdims=True)
    xc = c - mu
    var = jnp.mean(xc * xc, axis=-1, keepdims=True)
    return _silu(xc * lax.rsqrt(var + eps) * g + b)


CONV_PAD = 32
CONV_ROWS = 32
LANES = 128


def _conv_slabs():
    return [slice(s * LANES, (s + 1) * LANES) for s in range(C_CONV // LANES)]


def _conv_tile_start(ext_ref, t):
    @pl.when(t == 0)
    def _():
        ext_ref[:, 0:CONV_PAD, :] = jnp.zeros((C_CONV // LANES, CONV_PAD, LANES), F32)


def _conv_tile(ga_ref, gb_ref, wdw_ref, bdw_ref, lng_ref, lnb_ref, ext_ref, conv_ref, c_ref):
    tt = ga_ref.shape[0]
    hist = CONV_WIDTH - 1
    slabs = _conv_slabs()
    cin = ga_ref[...] * _sigmoid(gb_ref[...])
    for s, sl in enumerate(slabs):
        ext_ref[s, CONV_PAD:CONV_PAD + tt, :] = cin[:, sl]
    blocks = range(0, tt, CONV_ROWS)
    for s, sl in enumerate(slabs):
        w_all = wdw_ref[:, sl]
        accs = [jnp.broadcast_to(bdw_ref[:, sl], (CONV_ROWS, LANES)) for _ in blocks]
        for j in range(CONV_WIDTH):
            w = jnp.broadcast_to(w_all[j:j + 1, :], (CONV_ROWS, LANES))
            for i, r0 in enumerate(blocks):
                off = CONV_PAD - hist + r0 + j
                accs[i] = accs[i] + w * ext_ref[s, off:off + CONV_ROWS, :]
        for i, r0 in enumerate(blocks):
            conv_ref[r0:r0 + CONV_ROWS, sl] = accs[i]
    lng = lng_ref[...]
    lnb = lnb_ref[...]
    for r0 in blocks:
        c_ref[r0:r0 + CONV_ROWS, :] = _ln_silu(conv_ref[r0:r0 + CONV_ROWS, :], lng, lnb).astype(c_ref.dtype)


def _conv_tile_finish(st_ref, ext_ref, tt, t, n_t):
    hist = CONV_WIDTH - 1

    @pl.when(t == n_t - 1)
    def _():
        for s, sl in enumerate(_conv_slabs()):
            st_ref[0, :, sl] = ext_ref[s, CONV_PAD + tt - hist:CONV_PAD + tt, :]

    ext_ref[:, 0:CONV_PAD, :] = ext_ref[:, tt:tt + CONV_PAD, :]


def _conv_sample_kernel(ga_ref, gb_ref, st_ref, wdw_ref, bdw_ref, lng_ref, lnb_ref, c_ref, nst_ref, ext_ref,
                        *, bb, t_len):
    hist = CONV_WIDTH - 1
    cin = ga_ref[...] * _sigmoid(gb_ref[...])
    bias = bdw_ref[...]
    rows = []
    for i in range(bb):
        ext_ref[i, 0:hist, :] = st_ref[i]
        ext_ref[i, hist:hist + t_len, :] = cin[i * t_len:(i + 1) * t_len, :]
        acc = jnp.broadcast_to(bias, (t_len, C_CONV))
        for j in range(CONV_WIDTH):
            acc = acc + wdw_ref[j:j + 1, :] * ext_ref[i, j:j + t_len, :]
        rows.append(acc)
        nst_ref[i] = ext_ref[i, t_len:t_len + hist, :]
    conv = jnp.concatenate(rows, axis=0)
    c_ref[...] = _ln_silu(conv, lng_ref[...], lnb_ref[...]).astype(c_ref.dtype)


def _conv_sample(proj, state, wdw, bdw, lng, lnb, nb, t_len, bb):
    hist = CONV_WIDTH - 1
    rows = bb * t_len
    return pl.pallas_call(
        functools.partial(_conv_sample_kernel, bb=bb, t_len=t_len),
        grid=(nb // bb,),
        in_specs=[
            pl.BlockSpec((rows, C_CONV), lambda i: (i, COL_GA)),
            pl.BlockSpec((rows, C_CONV), lambda i: (i, COL_GB)),
            pl.BlockSpec((bb, hist, C_CONV), lambda i: (i, 0, 0)),
            pl.BlockSpec((CONV_WIDTH, C_CONV), lambda i: (0, 0)),
            pl.BlockSpec((1, C_CONV), lambda i: (0, 0)),
            pl.BlockSpec((1, C_CONV), lambda i: (0, 0)),
            pl.BlockSpec((1, C_CONV), lambda i: (0, 0)),
        ],
        out_specs=[
            pl.BlockSpec((rows, C_CONV), lambda i: (i, 0)),
            pl.BlockSpec((bb, hist, C_CONV), lambda i: (i, 0, 0)),
        ],
        out_shape=[
            jax.ShapeDtypeStruct((nb * t_len, C_CONV), F32),
            jax.ShapeDtypeStruct((nb, hist, C_CONV), F32),
        ],
        scratch_shapes=[pltpu.VMEM((bb, hist + 2 * t_len + 8, C_CONV), F32)],
        compiler_params=_params(("parallel",)),
        name="conv_sample",
    )(proj, proj, state, wdw, bdw, lng, lnb)


DN_PAD = 8


def _split3(x):
    hi = x.astype(BF16)
    r1 = x - hi.astype(F32)
    mid = r1.astype(BF16)
    lo = (r1 - mid.astype(F32)).astype(BF16)
    return hi, mid, lo


def _delta_run(seqs, wdc_ref, arow_ref, dtrow_ref, gdn_ref, *, c, n_valid):
    hist = DN_CONV - 1
    n_steps = 0
    while (1 << (n_steps + 1)) < n_valid:
        n_steps += 1

    row = lax.broadcasted_iota(jnp.int32, (c, SMALL_W), 0)
    valid = row < n_valid
    ii = lax.broadcasted_iota(jnp.int32, (c, c), 0)
    jj = lax.broadcasted_iota(jnp.int32, (c, c), 1)
    causal_f = jnp.where(jj <= ii, 1.0, 0.0)
    strict_f = jnp.where(jj < ii, 1.0, 0.0)
    above = (causal_f - 1.0) * 1e30
    tri = causal_f.astype(BF16)
    ones = jnp.ones((DN_DK, DN_DK), BF16)
    arow = arow_ref[...]
    dtrow = dtrow_ref[...]
    gdn = gdn_ref[...]

    def conv_slab(ch, s):
        sl = slice(s * LANES, (s + 1) * LANES)
        conv = wdc_ref[0:1, sl] * ch["win"](s, DN_PAD - hist, DN_PAD - hist + c)
        for j in range(1, DN_CONV):
            conv = conv + wdc_ref[j:j + 1, sl] * ch["win"](s, DN_PAD - hist + j, DN_PAD - hist + j + c)
        return _silu(conv)

    def unit_norm_pair(x, y):
        sq = jnp.concatenate([x * x, y * y], axis=0)
        hi = sq.astype(BF16)
        lo = (sq - hi.astype(F32)).astype(BF16)
        inv = lax.rsqrt(_dot(hi, ones) + _dot(lo, ones) + 1e-6)
        return x * inv[0:c], y * inv[c:2 * c]

    chunks = [ch for seq in seqs for ch in seq]
    for ch in chunks:
        ba = ch["ba"]
        beta = _sigmoid(ba)
        g = arow * jax.nn.softplus(ba + dtrow)
        if n_valid < c:
            beta = jnp.where(valid, beta, 0.0)
            g = jnp.where(valid, g, 0.0)
        ch["beta"] = beta
        ch["g3"] = _split3(g)
    for ch in chunks:
        ghi, gmid, glo = ch["g3"]
        gam = _dot(tri, ghi) + _dot(tri, gmid) + _dot(tri, glo)
        ch["gam"] = gam
        ch["gam_t"] = gam.T

    units = []
    heads = DN_HEADS
    for ch in chunks:
        ch["units"] = []
        for h in range(heads):
            q, k = unit_norm_pair(conv_slab(ch, h), conv_slab(ch, heads + h))
            q = q * (DN_DK ** -0.5)
            v = conv_slab(ch, 2 * heads + h)
            if n_valid < c:
                k = jnp.where(valid[:, 0:1], k, 0.0)
            beta = ch["beta"][:, h:h + 1]
            gam_c = ch["gam"][:, 8 + h:9 + h]
            gam_r = ch["gam_t"][8 + h:9 + h, :]
            g_last = gam_r[:, c - 1:c]
            kb = k * beta
            u = dict(ch=ch, h=h, q=q, k=k, kb=kb, vb=v * beta, gam_c=gam_c, g_last=g_last,
                     dec=jnp.exp((gam_c - gam_r) * causal_f + above))
            ch["units"].append(u)
            units.append(u)

    for u in units:
        u["gmat"] = _dot_nt(jnp.concatenate([u["kb"], u["q"]], axis=0).astype(BF16), u["k"].astype(BF16))
    for u in units:
        gmat = u.pop("gmat")
        u["e"] = -(gmat[0:c] * (u["dec"] * strict_f))
        u["p"] = u["e"]
        u["qk"] = gmat[c:2 * c] * u.pop("dec")
    if n_steps:
        for u in units:
            p16 = u["p"].astype(BF16)
            u["p"] = _dot(p16, p16)
        for _ in range(n_steps - 1):
            for u in units:
                p16 = u["p"].astype(BF16)
                u["pe"] = _dot(jnp.concatenate([u["p"], u["e"]], axis=0).astype(BF16), p16)
            for u in units:
                pe = u.pop("pe")
                u["e"] = u["e"] + u["p"] + pe[c:2 * c]
                u["p"] = pe[0:c]
        for u in units:
            u["ep"] = _dot(u["e"].astype(BF16), u["p"].astype(BF16))
        for u in units:
            u["e"] = u["e"] + u.pop("p") + u.pop("ep")
    for u in units:
        eg = jnp.exp(u["gam_c"])
        rhs = jnp.concatenate([u.pop("vb"), u.pop("kb") * eg], axis=1)
        u["rhs"] = rhs
        u["qg"] = u.pop("q") * eg
        u["sol"] = _dot(u.pop("e").astype(BF16), rhs.astype(BF16))
    for u in units:
        sol = u.pop("rhs") + u.pop("sol")
        u["u"] = sol[:, 0:DN_DV]
        u["lhs16"] = jnp.concatenate([sol[:, DN_DV:DN_DV + DN_DK], u.pop("qg")], axis=0).astype(BF16)
        kg = u.pop("k") * jnp.exp(u["g_last"] - u["gam_c"])
        u["qk_kgt16"] = jnp.concatenate([u.pop("qk"), kg.T], axis=0).astype(BF16)
        u["decay"] = jnp.exp(u["g_last"])

    for j in range(len(seqs[0])):
        cur = [u for seq in seqs for u in seq[j]["units"]]
        for u in cur:
            u["s_old"] = u["ch"]["s"][u["h"]]
            u["xs"] = _dot(u["lhs16"], u["s_old"].astype(BF16))
        for u in cur:
            xs = u.pop("xs")
            u["w16"] = (u["u"] - xs[0:c]).astype(BF16)
            u["o"] = xs[c:2 * c]
        for u in cur:
            u["o2ds"] = _dot(u["qk_kgt16"], u["w16"])
        for u in cur:
            ch, h = u["ch"], u["h"]
            o2ds = u.pop("o2ds")
            ch["s"][h] = u["s_old"] * u["decay"] + o2ds[c:c + DN_DK]
            zh = ch["z"][:, h * DN_DV:(h + 1) * DN_DV]
            ch["store_o"](h, _rms_rows(u["o"] + o2ds[0:c], gdn) * _silu(zh))


DP_CHUNKS = 2


def _delta_prompt_kernel(q_ref, k_ref, v_ref, z_ref, ba_ref, wdc_ref, arow_ref, dtrow_ref, gdn_ref,
                         o_ref, nd_ref, ns_ref, ext_ref, s_ref):
    t = pl.program_id(1)
    rows = q_ref.shape[0]
    c = DN_CHUNK
    hist = DN_CONV - 1

    @pl.when(t == 0)
    def _():
        ext_ref[:, 0:DN_PAD, :] = jnp.zeros((QKV_W // LANES, DN_PAD, LANES), F32)
        s_ref[...] = jnp.zeros(s_ref.shape, F32)

    n_slab = QKV_W // LANES
    for i, ref in enumerate((q_ref, k_ref, v_ref)):
        for s in range(1024 // LANES):
            ext_ref[i * (1024 // LANES) + s, DN_PAD:DN_PAD + rows, :] = ref[:, s * LANES:(s + 1) * LANES]

    seq = []
    for g in range(rows // c):
        def win(s, j0, j1, g=g):
            return ext_ref[s, g * c + j0:g * c + j1, :]

        def store_o(h, val, g=g):
            o_ref[g * c:(g + 1) * c, h * DN_DV:(h + 1) * DN_DV] = val.astype(o_ref.dtype)

        seq.append(dict(win=win, z=z_ref[g * c:(g + 1) * c, :], ba=ba_ref[g * c:(g + 1) * c, :], s=s_ref,
                        store_o=store_o))
    _delta_run([seq], wdc_ref, arow_ref, dtrow_ref, gdn_ref, c=c, n_valid=c)

    @pl.when(t == pl.num_programs(1) - 1)
    def _():
        for s in range(n_slab):
            nd_ref[0, :, s * LANES:(s + 1) * LANES] = ext_ref[s, DN_PAD + rows - hist:DN_PAD + rows, :]
        ns_ref[0] = s_ref[...]

    ext_ref[:, DN_PAD - hist:DN_PAD, :] = ext_ref[:, DN_PAD + rows - hist:DN_PAD + rows, :]


def _delta_prompt(proj, ba, wdc, arow, dtrow, gdn, nb, t_len):
    rows = DN_CHUNK * (DP_CHUNKS if t_len % (DN_CHUNK * DP_CHUNKS) == 0 else 1)
    nt = t_len // rows
    row_spec = lambda col: pl.BlockSpec((rows, 1024), lambda b, t: (b * nt + t, col))
    small = lambda shape: pl.BlockSpec(shape, lambda b, t: (0,) * len(shape))
    return pl.pallas_call(
        _delta_prompt_kernel,
        grid=(nb, nt),
        in_specs=[
            row_spec(COL_Q), row_spec(COL_K), row_spec(COL_V), row_spec(COL_Z),
            pl.BlockSpec((rows, SMALL_W), lambda b, t: (b * nt + t, 0)),
            small((DN_CONV, QKV_W)), small((1, SMALL_W)), small((1, SMALL_W)), small((1, DN_DV)),
        ],
        out_specs=[
            pl.BlockSpec((rows, 1024), lambda b, t: (b * nt + t, 0)),
            pl.BlockSpec((1, DN_CONV - 1, QKV_W), lambda b, t: (b, 0, 0)),
            pl.BlockSpec((1, DN_HEADS, DN_DK, DN_DV), lambda b, t: (b, 0, 0, 0)),
        ],
        out_shape=[
            jax.ShapeDtypeStruct((nb * t_len, 1024), BF16),
            jax.ShapeDtypeStruct((nb, DN_CONV - 1, QKV_W), F32),
            jax.ShapeDtypeStruct((nb, DN_HEADS, DN_DK, DN_DV), F32),
        ],
        scratch_shapes=[pltpu.VMEM((QKV_W // LANES, DN_PAD + rows, LANES), F32),
                        pltpu.VMEM((DN_HEADS, DN_DK, DN_DV), F32)],
        compiler_params=_params(("parallel", "arbitrary")),
        name="delta_prompt",
    )(proj, proj, proj, proj, ba, wdc, arow, dtrow, gdn)


DS_ROWS = 8
DS_GROUPS = 2


def _delta_sample_kernel(q_ref, k_ref, v_ref, z_ref, ba_ref, dst_ref, s0_ref, wdc_ref, arow_ref, dtrow_ref,
                         gdn_ref, o_ref, nd_ref, ns_ref, ext_ref, *, t_len):
    hist = DN_CONV - 1
    per = DS_ROWS // t_len
    n_slab = QKV_W // LANES
    ns_ref[...] = s0_ref[...]
    seqs = []
    for i in range(dst_ref.shape[0]):
        g0 = (i // per) * DS_ROWS
        r0 = (i % per) * t_len
        ext_ref[i] = jnp.zeros((n_slab, DN_PAD + DS_ROWS, LANES), F32)
        for s in range(n_slab):
            sl = slice(s * LANES, (s + 1) * LANES)
            ext_ref[i, s, DN_PAD - hist:DN_PAD, :] = dst_ref[i, :, sl]
            src = (q_ref, k_ref, v_ref)[s // (1024 // LANES)]
            col = (s % (1024 // LANES)) * LANES
            ext_ref[i, s, DN_PAD:DN_PAD + t_len, :] = src[g0 + r0:g0 + r0 + t_len, col:col + LANES]
            nd_ref[i, :, sl] = ext_ref[i, s, DN_PAD + t_len - hist:DN_PAD + t_len, :]
        z = z_ref[g0:g0 + DS_ROWS, :]
        ba = ba_ref[g0:g0 + DS_ROWS, :]
        if r0:
            z = pltpu.roll(z, DS_ROWS - r0, 0)
            ba = pltpu.roll(ba, DS_ROWS - r0, 0)

        def win(s, j0, j1, i=i):
            return ext_ref[i, s, j0:j1, :]

        def store_o(h, val, g0=g0, r0=r0):
            o_ref[g0 + r0:g0 + r0 + t_len, h * DN_DV:(h + 1) * DN_DV] = val[0:t_len].astype(o_ref.dtype)

        seqs.append([dict(win=win, z=z, ba=ba, s=ns_ref.at[i], store_o=store_o)])
    _delta_run(seqs, wdc_ref, arow_ref, dtrow_ref, gdn_ref, c=DS_ROWS, n_valid=t_len)


def _delta_sample(proj, ba, dstate, s0, wdc, arow, dtrow, gdn, nb, t_len):
    per = DS_ROWS // t_len
    groups = DS_GROUPS if nb % (per * DS_GROUPS) == 0 else 1
    bb = per * groups
    rows = DS_ROWS * groups
    hist = DN_CONV - 1
    row_spec = lambda col: pl.BlockSpec((rows, 1024), lambda i: (i, col))
    small = lambda shape: pl.BlockSpec(shape, lambda i: (0,) * len(shape))
    return pl.pallas_call(
        functools.partial(_delta_sample_kernel, t_len=t_len),
        grid=(nb // bb,),
        in_specs=[
            row_spec(COL_Q), row_spec(COL_K), row_spec(COL_V), row_spec(COL_Z),
            pl.BlockSpec((rows, SMALL_W), lambda i: (i, 0)),
            pl.BlockSpec((bb, hist, QKV_W), lambda i: (i, 0, 0)),
            pl.BlockSpec((bb, DN_HEADS, DN_DK, DN_DV), lambda i: (i, 0, 0, 0)),
            small((DN_CONV, QKV_W)), small((1, SMALL_W)), small((1, SMALL_W)), small((1, DN_DV)),
        ],
        out_specs=[
            pl.BlockSpec((rows, 1024), lambda i: (i, 0)),
            pl.BlockSpec((bb, hist, QKV_W), lambda i: (i, 0, 0)),
            pl.BlockSpec((bb, DN_HEADS, DN_DK, DN_DV), lambda i: (i, 0, 0, 0)),
        ],
        out_shape=[
            jax.ShapeDtypeStruct((nb * t_len, 1024), F32),
            jax.ShapeDtypeStruct((nb, hist, QKV_W), F32),
            jax.ShapeDtypeStruct((nb, DN_HEADS, DN_DK, DN_DV), F32),
        ],
        scratch_shapes=[pltpu.VMEM((bb, QKV_W // LANES, DN_PAD + DS_ROWS, LANES), F32)],
        compiler_params=_params(("parallel",)),
        name="delta_sample",
    )(proj, proj, proj, proj, ba, dstate, s0, wdc, arow, dtrow, gdn)


def _softmax_rows(s):
    m = jnp.max(s, axis=-1, keepdims=True)
    p = jnp.exp(s - m)
    return p / jnp.sum(p, axis=-1, keepdims=True)


def _attn_pairs(pairs):
    scores = [_dot_nt(q16, k.astype(BF16)) * (XA_DIM ** -0.5) for q16, k, _ in pairs]
    probs = [_softmax_rows(s).astype(BF16) for s in scores]
    return [_dot(p, v.astype(BF16)) for p, (_, _, v) in zip(probs, pairs)]


def _attn_prompt_kernel(xq_ref, k_ref, v_ref, o_ref):
    sls = [slice(h * XA_DIM, (h + 1) * XA_DIM) for h in range(XA_HEADS)]
    outs = _attn_pairs([(xq_ref[:, sl].astype(BF16), k_ref[:, sl], v_ref[:, sl]) for sl in sls])
    for sl, o in zip(sls, outs):
        o_ref[:, sl] = o.astype(o_ref.dtype)


def _attn_prompt(proj, mkv, nb, t_len, tq):
    nt = t_len // tq
    return pl.pallas_call(
        _attn_prompt_kernel,
        grid=(nb, nt),
        in_specs=[
            pl.BlockSpec((tq, XW), lambda b, t: (b * nt + t, COL_XQ)),
            pl.BlockSpec((N_MEM, XW), lambda b, t: (b, 0)),
            pl.BlockSpec((N_MEM, XW), lambda b, t: (b, 1)),
        ],
        out_specs=pl.BlockSpec((tq, XW), lambda b, t: (b * nt + t, 0)),
        out_shape=jax.ShapeDtypeStruct((nb * t_len, XW), BF16),
        compiler_params=_params(("parallel", "arbitrary")),
        name="attn_prompt",
    )(proj, mkv, mkv)


def _attn_sample_kernel(xq_ref, k_hbm, v_hbm, o_ref, kbuf, vbuf, sem, *, layer, bb, t_len):
    i = pl.program_id(0)
    slot = i % 2

    def copies(step, slot_):
        out = []
        for h in range(XA_HEADS):
            for kv, (hbm, buf) in enumerate(((k_hbm, kbuf), (v_hbm, vbuf))):
                out.append(pltpu.make_async_copy(hbm.at[layer, pl.ds(step * bb, bb), :, h, :], buf.at[slot_, h],
                                                 sem.at[kv, slot_, h]))
        return out

    @pl.when(i == 0)
    def _():
        for cp in copies(0, 0):
            cp.start()

    @pl.when(i + 1 < pl.num_programs(0))
    def _():
        for cp in copies(i + 1, 1 - slot):
            cp.start()

    for cp in copies(i, slot):
        cp.wait()

    per = DS_ROWS // t_len
    rows = lax.broadcasted_iota(jnp.int32, (DS_ROWS, XA_DIM), 0)
    pairs = []
    for b in range(bb):
        r0 = (b // per) * DS_ROWS
        for h in range(XA_HEADS):
            q16 = xq_ref[r0:r0 + DS_ROWS, h * XA_DIM:(h + 1) * XA_DIM].astype(BF16)
            pairs.append((q16, kbuf[slot, h, b], vbuf[slot, h, b]))
    outs = _attn_pairs(pairs)
    for grp in range(bb // per):
        r0 = grp * DS_ROWS
        for h in range(XA_HEADS):
            o = outs[(grp * per) * XA_HEADS + h]
            for s in range(1, per):
                o = jnp.where(rows < s * t_len, o, outs[(grp * per + s) * XA_HEADS + h])
            o_ref[r0:r0 + DS_ROWS, h * XA_DIM:(h + 1) * XA_DIM] = o.astype(o_ref.dtype)


def _attn_sample(proj, ck, cv, layer, nb, t_len, bb):
    rows = bb * t_len
    buf = pltpu.VMEM((2, XA_HEADS, bb, N_MEM, XA_DIM), F32)
    return pl.pallas_call(
        functools.partial(_attn_sample_kernel, layer=layer, bb=bb, t_len=t_len),
        grid=(nb // bb,),
        in_specs=[pl.BlockSpec((rows, XW), lambda i: (i, COL_XQ)), pl.BlockSpec(memory_space=pl.ANY),
                  pl.BlockSpec(memory_space=pl.ANY)],
        out_specs=pl.BlockSpec((rows, XW), lambda i: (i, 0)),
        out_shape=jax.ShapeDtypeStruct((nb * t_len, XW), F32),
        scratch_shapes=[buf, buf, pltpu.SemaphoreType.DMA((2, 2, XA_HEADS))],
        compiler_params=_params(("arbitrary",)),
        name="attn_sample",
    )(proj, ck, cv)


def _merge_others(d_ref, m_ref, gd_ref, gx_ref, wd_ref, wm_ref):
    mix = _sigmoid(gd_ref[...]) * _dot(d_ref[...].astype(BF16), wd_ref[...])
    return mix + _sigmoid(gx_ref[...]) * _dot(m_ref[...].astype(BF16), wm_ref[...])


def _merge_finish(mix, c16, gc_ref, x_ref, wc_ref, bc_ref, wo_ref, gpm_ref, h_ref):
    mix = mix + _sigmoid(gc_ref[...]) * (_dot(c16, wc_ref[...]) + bc_ref[...])
    out = _dot(mix.astype(BF16), wo_ref[...])
    h_ref[...] = x_ref[...] + _rms_rows(out, gpm_ref[...])


def _merge_kernel(c_ref, d_ref, m_ref, gc_ref, gd_ref, gx_ref, x_ref, wc_ref, bc_ref, wd_ref, wm_ref, wo_ref,
                  gpm_ref, h_ref):
    mix = _merge_others(d_ref, m_ref, gd_ref, gx_ref, wd_ref, wm_ref)
    _merge_finish(mix, c_ref[...].astype(BF16), gc_ref, x_ref, wc_ref, bc_ref, wo_ref, gpm_ref, h_ref)


def _merge_conv_kernel(ga_ref, gb_ref, wdw_ref, bdw_ref, lng_ref, lnb_ref, d_ref, m_ref, gc_ref, gd_ref, gx_ref,
                       x_ref, wc_ref, bc_ref, wd_ref, wm_ref, wo_ref, gpm_ref, h_ref, st_ref, ext_ref, conv_ref,
                       c_ref):
    t = pl.program_id(1)
    _conv_tile_start(ext_ref, t)
    mix = _merge_others(d_ref, m_ref, gd_ref, gx_ref, wd_ref, wm_ref)
    _conv_tile(ga_ref, gb_ref, wdw_ref, bdw_ref, lng_ref, lnb_ref, ext_ref, conv_ref, c_ref)
    _merge_finish(mix, c_ref[...], gc_ref, x_ref, wc_ref, bc_ref, wo_ref, gpm_ref, h_ref)
    _conv_tile_finish(st_ref, ext_ref, ga_ref.shape[0], t, pl.num_programs(1))


_MERGE_WEIGHT_SHAPES = ((C_CONV, D_MODEL), (1, D_MODEL), (1024, D_MODEL), (XW, D_MODEL), (D_MODEL, D_MODEL),
                        (1, D_MODEL))


def _merge(c, d, m, proj, x2d, wc, bc, wd, wm, wo, gpm, tm):
    rows = x2d.shape[0]
    row = lambda width, col=0: pl.BlockSpec((tm, width), lambda i: (i, col))
    return pl.pallas_call(
        _merge_kernel,
        grid=(rows // tm,),
        in_specs=[
            row(1024), row(1024), row(1024),
            row(D_MODEL, 0), row(D_MODEL, 1), row(D_MODEL, 2),
            row(D_MODEL),
        ] + [_const_spec(shape) for shape in _MERGE_WEIGHT_SHAPES],
        out_specs=row(D_MODEL),
        out_shape=jax.ShapeDtypeStruct((rows, D_MODEL), F32),
        compiler_params=_params(("parallel",)),
        name="merge",
    )(c, d, m, proj, proj, proj, x2d, wc, bc, wd, wm, wo, gpm)


def _merge_conv(d, m, proj, x2d, wdw, bdw, lng, lnb, wc, bc, wd, wm, wo, gpm, nb, t_len, tt):
    nt = t_len // tt
    row = lambda width, col=0: pl.BlockSpec((tt, width), lambda b, t: (b * nt + t, col))
    vec = lambda n: pl.BlockSpec((1, n), lambda b, t: (0, 0))
    return pl.pallas_call(
        _merge_conv_kernel,
        grid=(nb, nt),
        in_specs=[
            row(C_CONV, COL_GA), row(C_CONV, COL_GB),
            pl.BlockSpec((CONV_WIDTH, C_CONV), lambda b, t: (0, 0)), vec(C_CONV), vec(C_CONV), vec(C_CONV),
            row(1024), row(1024),
            row(D_MODEL, 0), row(D_MODEL, 1), row(D_MODEL, 2),
            row(D_MODEL),
        ] + [_const_spec(shape) for shape in _MERGE_WEIGHT_SHAPES],
        out_specs=[row(D_MODEL), pl.BlockSpec((1, CONV_WIDTH - 1, C_CONV), lambda b, t: (b, 0, 0))],
        out_shape=[
            jax.ShapeDtypeStruct((nb * t_len, D_MODEL), F32),
            jax.ShapeDtypeStruct((nb, CONV_WIDTH - 1, C_CONV), F32),
        ],
        scratch_shapes=[pltpu.VMEM((C_CONV // LANES, CONV_PAD + tt, LANES), F32), pltpu.VMEM((tt, C_CONV), F32),
                        pltpu.VMEM((tt, C_CONV), BF16)],
        compiler_params=pltpu.CompilerParams(dimension_semantics=("parallel", "arbitrary"),
                                             vmem_limit_bytes=MERGE_CONV_VMEM_LIMIT),
        name="merge_conv",
    )(proj, proj, wdw, bdw, lng, lnb, d, m, proj, proj, proj, x2d, wc, bc, wd, wm, wo, gpm)


FFN_TF = 512
FFN_TN = 512


def _ffn_kernel(h_ref, g1_ref, wg_ref, wu_ref, wd_ref, g2_ref, y_ref, fn_ref, act_ref, *, n_up):
    f = pl.program_id(1)

    @pl.when(f == 0)
    def _():
        _norm_rows_to(h_ref, g1_ref, fn_ref, 64)

    @pl.when(f < n_up)
    def _():
        fn = fn_ref[...]
        col = pl.multiple_of(f * FFN_TF, FFN_TF)
        act_ref[:, pl.ds(col, FFN_TF)] = (_silu(_dot(fn, wg_ref[...])) * _dot(fn, wu_ref[...])).astype(BF16)

    @pl.when(f >= n_up)
    def _():
        col = pl.multiple_of((f - n_up) * FFN_TN, FFN_TN)
        y_ref[:, pl.ds(col, FFN_TN)] = _dot(act_ref[...], wd_ref[...])

    @pl.when(f == pl.num_programs(1) - 1)
    def _():
        g2 = g2_ref[...]
        rows = 64

        def body(i, c):
            r = pl.multiple_of(i * rows, rows)
            y_ref[pl.ds(r, rows), :] = h_ref[pl.ds(r, rows), :] + _rms_rows(y_ref[pl.ds(r, rows), :], g2)
            return c

        lax.fori_loop(0, h_ref.shape[0] // rows, body, 0)


def _ffn(h2d, g1, wg, wu, wd, g2, tm):
    rows = h2d.shape[0]
    d_ff = wg.shape[1]
    n_up = d_ff // FFN_TF
    n_down = D_MODEL // FFN_TN
    return pl.pallas_call(
        functools.partial(_ffn_kernel, n_up=n_up),
        grid=(rows // tm, n_up + n_down),
        in_specs=[
            pl.BlockSpec((tm, D_MODEL), lambda i, f: (i, 0)),
            pl.BlockSpec((1, D_MODEL), lambda i, f: (0, 0)),
            pl.BlockSpec((D_MODEL, FFN_TF), lambda i, f: (0, jnp.minimum(f, n_up - 1))),
            pl.BlockSpec((D_MODEL, FFN_TF), lambda i, f: (0, jnp.minimum(f, n_up - 1))),
            pl.BlockSpec((d_ff, FFN_TN), lambda i, f: (0, jnp.maximum(f - n_up, 0))),
            pl.BlockSpec((1, D_MODEL), lambda i, f: (0, 0)),
        ],
        out_specs=pl.BlockSpec((tm, D_MODEL), lambda i, f: (i, 0)),
        out_shape=jax.ShapeDtypeStruct((rows, D_MODEL), F32),
        scratch_shapes=[pltpu.VMEM((tm, D_MODEL), BF16), pltpu.VMEM((tm, d_ff), BF16)],
        compiler_params=_params(("parallel", "arbitrary")),
        name="ffn",
    )(h2d, g1, wg, wu, wd, g2)


def _row(v):
    return v.reshape(1, -1).astype(F32)


def _layer_weights(w_in, w_dw, b_dw, ln_g, ln_b, w_p_conv, b_p_conv, w_dconv, a_log, dt_bias, g_dn_norm,
                   w_p_delta, w_p_mem, w_o, w_gate, w_up, w_down):
    lane_pad = (0, SMALL_W - 2 * DN_HEADS)
    arow = jnp.pad(jnp.concatenate([jnp.zeros((DN_HEADS,), F32), -jnp.exp(a_log.astype(F32))]), lane_pad)
    dtrow = jnp.pad(jnp.concatenate([jnp.zeros((DN_HEADS,), F32), dt_bias.astype(F32)]), lane_pad)
    return dict(
        w_t=jnp.swapaxes(w_in, 0, 1), w_dw=w_dw, b_dw=_row(b_dw), ln_g=_row(ln_g), ln_b=_row(ln_b),
        w_p_conv=w_p_conv.astype(BF16), b_p_conv=_row(b_p_conv), w_dconv=w_dconv, arow=_row(arow),
        dtrow=_row(dtrow), g_dn=_row(g_dn_norm), w_p_delta=w_p_delta.astype(BF16),
        w_p_mem=w_p_mem.astype(BF16), w_o=w_o.astype(BF16), w_gate=w_gate, w_up=w_up, w_down=w_down.astype(BF16))


def _tile(n, pref):
    return pref if n % pref == 0 else n


def kernel(x_prompt, x_sample, mem_prompt, cache_mem_k, cache_mem_v, state_conv, state_delta_conv, state_delta, g_pre_mix, w_in, w_dw, b_dw, ln_g, ln_b, w_p_conv, b_p_conv, w_dconv, a_log, dt_bias, g_dn_norm, w_p_delta, g_mem, w_mem_kv, w_p_mem, w_o, g_post_mix, g_pre_ffn, w_gate, w_up, w_down, g_post_ffn):
    depth = w_in.shape[0]
    assert depth == 1, "single-layer stack"
    l = 0
    pb, pt, _ = x_prompt.shape
    sb, st, _ = x_sample.shape
    assert pt % DN_CHUNK == 0 and st <= DS_ROWS and DS_ROWS % st == 0

    lw = _layer_weights(w_in[l], w_dw[l], b_dw[l], ln_g[l], ln_b[l], w_p_conv[l], b_p_conv[l], w_dconv[l],
                        a_log[l], dt_bias[l], g_dn_norm[l], w_p_delta[l], w_p_mem[l], w_o[l], w_gate[l],
                        w_up[l], w_down[l])
    gpre, gpm, gff1, gff2 = _row(g_pre_mix[l]), _row(g_post_mix[l]), _row(g_pre_ffn[l]), _row(g_post_ffn[l])

    merge_w = (lw["w_p_conv"], lw["b_p_conv"], lw["w_p_delta"], lw["w_p_mem"], lw["w_o"], gpm)

    def ffn(h):
        return _ffn(h, gff1, *ffn_w, lw["w_down"], gff2, _tile(h.shape[0], 512))

    xp = x_prompt.reshape(pb * pt, D_MODEL)
    mkv = _norm_proj(mem_prompt.reshape(pb * N_MEM, D_MODEL), _row(g_mem[l]), w_mem_kv[l].astype(BF16),
                     _tile(pb * N_MEM, 512), 1024)
    proj_p, ba_p, ffn_w = _in_proj(xp, gpre, lw["w_t"], _tile(pb * pt, 1024), side=(lw["w_gate"], lw["w_up"]))
    d_p, dconv_p, s_p = _delta_prompt(proj_p, ba_p, lw["w_dconv"], lw["arow"], lw["dtrow"], lw["g_dn"], pb, pt)
    m_p = _attn_prompt(proj_p, mkv, pb, pt, _tile(pt, 512))
    h_p, conv_p = _merge_conv(d_p, m_p, proj_p, xp, lw["w_dw"], lw["b_dw"], lw["ln_g"], lw["ln_b"], *merge_w, pb, pt,
                              _tile(pt, 256))
    y_p = ffn(h_p).reshape(pb, pt, D_MODEL)
    mk = mkv[:, 0:XW].reshape(pb, N_MEM, XA_HEADS, XA_DIM)
    mv = mkv[:, XW:2 * XW].reshape(pb, N_MEM, XA_HEADS, XA_DIM)

    xs = x_sample.reshape(sb * st, D_MODEL)
    proj_s, ba_s, _ = _in_proj(xs, gpre, lw["w_t"], _tile(sb * st, 512))
    c_s, conv_s = _conv_sample(proj_s, state_conv[l], lw["w_dw"], lw["b_dw"], lw["ln_g"], lw["ln_b"], sb, st,
                               _tile(sb, 8))
    d_s, dconv_s, s_s = _delta_sample(proj_s, ba_s, state_delta_conv[l], state_delta[l], lw["w_dconv"],
                                      lw["arow"], lw["dtrow"], lw["g_dn"], sb, st)
    m_s = _attn_sample(proj_s, cache_mem_k, cache_mem_v, l, sb, st, 4)
    h_s = _merge(c_s, d_s, m_s, proj_s, xs, *merge_w, _tile(sb * st, 256))
    y_s = ffn(h_s).reshape(sb, st, D_MODEL)

    return (y_p, y_s, mk[None], mv[None], conv_p[None], dconv_p[None], s_p[None],
            conv_s[None], dconv_s[None], s_s[None])
```

```python
import functools

import jax
import jax.numpy as jnp
from jax import lax
from jax.experimental import pallas as pl
from jax.experimental.pallas import tpu as pltpu

F32 = jnp.float32
BF16 = jnp.bfloat16

D_MODEL = 2048
N_MEM = 256
C_CONV = 1024
CONV_WIDTH = 31
DN_HEADS = 8
DN_DK = 128
DN_DV = 128
DN_CONV = 4
DN_CHUNK = 64
XA_HEADS = 4
XA_DIM = 256
QKV_W = DN_HEADS * (2 * DN_DK + DN_DV)
XW = XA_HEADS * XA_DIM

COL_GA, COL_GB = 6, 7
COL_Q, COL_K, COL_V = 8, 9, 10
COL_Z = 11
COL_XQ = 12
N_MAIN = 13 * 1024
SMALL_W = 128

VMEM_LIMIT = 52 * 1024 * 1024
MERGE_CONV_VMEM_LIMIT = 56 * 1024 * 1024

_NT = (((1,), (1,)), ((), ()))


def _dot(a, b):
    return jnp.dot(a, b, preferred_element_type=F32)


def _dot_nt(a, b):
    return lax.dot_general(a, b, _NT, preferred_element_type=F32)


def _sigmoid(x):
    return jax.nn.sigmoid(x)


def _silu(x):
    return x * jax.nn.sigmoid(x)


def _rms_rows(x, g, eps=1e-6):
    ms = jnp.mean(x * x, axis=-1, keepdims=True)
    return x * lax.rsqrt(ms + eps) * g


def _params(sem):
    return pltpu.CompilerParams(dimension_semantics=sem, vmem_limit_bytes=VMEM_LIMIT)


def _const_spec(shape):
    nd = len(shape)
    return pl.BlockSpec(shape, lambda *_: (0,) * nd, pipeline_mode=pl.Buffered(1))


def _norm_rows_to(x_ref, g_ref, u_ref, rows):
    g = g_ref[...]
    tm = x_ref.shape[0]

    def body(i, c):
        r = pl.multiple_of(i * rows, rows)
        xx = x_ref[pl.ds(r, rows), :]
        u_ref[pl.ds(r, rows), :] = _rms_rows(xx, g).astype(BF16)
        return c

    lax.fori_loop(0, tm // rows, body, 0)


def _norm_cast_kernel(x_ref, g_ref, wba_ref, u_ref, ba_ref):
    _norm_rows_to(x_ref, g_ref, u_ref, 64)
    ba_ref[...] = _dot_nt(u_ref[...], wba_ref[...].astype(BF16))


def _norm_cast(x2d, g, w_t, tm):
    m, k = x2d.shape
    return pl.pallas_call(
        _norm_cast_kernel,
        grid=(m // tm,),
        in_specs=[
            pl.BlockSpec((tm, k), lambda i: (i, 0)),
            pl.BlockSpec((1, k), lambda i: (0, 0)),
            pl.BlockSpec((pl.Element(SMALL_W), pl.Element(k)), lambda i: (i * 0 + IN_B0, 0)),
        ],
        out_specs=[pl.BlockSpec((tm, k), lambda i: (i, 0)), pl.BlockSpec((tm, SMALL_W), lambda i: (i, 0))],
        out_shape=[jax.ShapeDtypeStruct((m, k), BF16), jax.ShapeDtypeStruct((m, SMALL_W), F32)],
        compiler_params=_params(("parallel",)),
        name="norm_cast",
    )(x2d, g, w_t)


def _side_spec(arr, count, start, step_of):
    rows = arr.shape[0] // count
    assert arr.shape[0] % count == 0 and rows % 16 == 0
    return pl.BlockSpec((rows, arr.shape[1]), lambda *idx: (jnp.clip(step_of(*idx) - start, 0, count - 1), 0))


def _proj_wt_kernel(*refs, n_cast):
    u_ref, w_ref = refs[0:2]
    side_in = refs[2:2 + n_cast]
    o_ref = refs[2 + n_cast]
    side_out = refs[3 + n_cast:3 + 2 * n_cast]
    wb_ref = refs[3 + 2 * n_cast]

    @pl.when(pl.program_id(1) == 0)
    def _():
        rows = 64

        def body(i, c):
            r = pl.multiple_of(i * rows, rows)
            wb_ref[pl.ds(r, rows), :] = w_ref[pl.ds(r, rows), :].astype(BF16)
            return c

        lax.fori_loop(0, w_ref.shape[0] // rows, body, 0)

    o_ref[...] = _dot_nt(u_ref[...], wb_ref[...])
    for src, dst in zip(side_in, side_out):
        dst[...] = src[...].astype(BF16)


def _proj_wt(u, w_t, row_offset, n_blocks, tn, tm, side=()):
    m, k = u.shape
    n_i = m // tm
    n_steps = n_blocks * n_i

    count = 1
    while side and 2 * count * len(side) <= n_steps and 2 * count <= SIDE_BLOCKS_MAX:
        count *= 2

    def side_spec(arr, start):
        return _side_spec(arr, count, start, lambda j, i: j * n_i + i)

    side_specs = [side_spec(arr, n * count) for n, arr in enumerate(side)]
    outs = pl.pallas_call(
        functools.partial(_proj_wt_kernel, n_cast=len(side)),
        grid=(n_blocks, n_i),
        in_specs=[
            pl.BlockSpec((tm, k), lambda j, i: (i, 0)),
            pl.BlockSpec((pl.Element(tn), pl.Element(k)), lambda j, i: (row_offset(j), 0)),
        ] + side_specs,
        out_specs=[pl.BlockSpec((tm, tn), lambda j, i: (i, j))] + side_specs,
        out_shape=[jax.ShapeDtypeStruct((m, n_blocks * tn), F32)]
        + [jax.ShapeDtypeStruct(arr.shape, BF16) for arr in side],
        scratch_shapes=[pltpu.VMEM((tn, k), BF16)],
        compiler_params=_params(("arbitrary", "arbitrary")),
        name="in_proj",
    )(u, w_t, *side)
    return outs[0], list(outs[1:])


IN_GLU0 = 0
IN_B0 = 2 * C_CONV + QKV_W + DN_HEADS * DN_DV
IN_XQ0 = IN_B0 + 2 * DN_HEADS
IN_GATES0 = IN_XQ0 + XW


SIDE_BLOCKS_MAX = 32
ROW_ALIGN = 16


def _main_row_offset(j):
    unit = 1024 // ROW_ALIGN
    units = jnp.where(j < 6, IN_GATES0 // ROW_ALIGN + unit * j,
                      jnp.where(j < 12, IN_GLU0 // ROW_ALIGN + unit * (j - 6), IN_XQ0 // ROW_ALIGN))
    return units * ROW_ALIGN


def _in_proj(x2d, g, w_t, tm, side=()):
    u, ba = _norm_cast(x2d, g, w_t, _tile(x2d.shape[0], 512))
    proj, side16 = _proj_wt(u, w_t, _main_row_offset, N_MAIN // 1024, 1024, tm, side)
    return proj, ba, side16


def _norm_proj_kernel(x_ref, g_ref, w_ref, o_ref, u_ref):
    @pl.when(pl.program_id(1) == 0)
    def _():
        _norm_rows_to(x_ref, g_ref, u_ref, 64)

    o_ref[...] = _dot(u_ref[...], w_ref[...])


def _norm_proj(x2d, g, w, tm, tn):
    m, k = x2d.shape
    n = w.shape[1]
    return pl.pallas_call(
        _norm_proj_kernel,
        grid=(m // tm, n // tn),
        in_specs=[
            pl.BlockSpec((tm, k), lambda i, j: (i, 0)),
            pl.BlockSpec((1, k), lambda i, j: (0, 0)),
            pl.BlockSpec((k, tn), lambda i, j: (0, j)),
        ],
        out_specs=pl.BlockSpec((tm, tn), lambda i, j: (i, j)),
        out_shape=jax.ShapeDtypeStruct((m, n), F32),
        scratch_shapes=[pltpu.VMEM((tm, k), BF16)],
        compiler_params=_params(("parallel", "arbitrary")),
        name="mem_kv_proj",
    )(x2d, g, w)


def _ln_silu(c, g, b, eps=1e-5):
    mu = jnp.mean(c, axis=-1, keepdims=True)
    xc = c - mu
    var = jnp.mean(xc * xc, axis=-1, keepdims=True)
    return _silu(xc * lax.rsqrt(var + eps) * g + b)


CONV_PAD = 32
CONV_ROWS = 32
LANES = 128


def _conv_slabs():
    return [slice(s * LANES, (s + 1) * LANES) for s in range(C_CONV // LANES)]


def _conv_tile_start(ext_ref, t):
    @pl.when(t == 0)
    def _():
        ext_ref[:, 0:CONV_PAD, :] = jnp.zeros((C_CONV // LANES, CONV_PAD, LANES), F32)


def _conv_tile(ga_ref, gb_ref, wdw_ref, bdw_ref, lng_ref, lnb_ref, ext_ref, conv_ref, c_ref):
    tt = ga_ref.shape[0]
    hist = CONV_WIDTH - 1
    slabs = _conv_slabs()
    cin = ga_ref[...] * _sigmoid(gb_ref[...])
    for s, sl in enumerate(slabs):
        ext_ref[s, CONV_PAD:CONV_PAD + tt, :] = cin[:, sl]
    blocks = range(0, tt, CONV_ROWS)
    for s, sl in enumerate(slabs):
        w_all = wdw_ref[:, sl]
        accs = [jnp.broadcast_to(bdw_ref[:, sl], (CONV_ROWS, LANES)) for _ in blocks]
        for j in range(CONV_WIDTH):
            w = jnp.broadcast_to(w_all[j:j + 1, :], (CONV_ROWS, LANES))
            for i, r0 in enumerate(blocks):
                off = CONV_PAD - hist + r0 + j
                accs[i] = accs[i] + w * ext_ref[s, off:off + CONV_ROWS, :]
        for i, r0 in enumerate(blocks):
            conv_ref[r0:r0 + CONV_ROWS, sl] = accs[i]
    lng = lng_ref[...]
    lnb = lnb_ref[...]
    for r0 in blocks:
        c_ref[r0:r0 + CONV_ROWS, :] = _ln_silu(conv_ref[r0:r0 + CONV_ROWS, :], lng, lnb).astype(c_ref.dtype)


def _conv_tile_finish(st_ref, ext_ref, tt, t, n_t):
    hist = CONV_WIDTH - 1

    @pl.when(t == n_t - 1)
    def _():
        for s, sl in enumerate(_conv_slabs()):
            st_ref[0, :, sl] = ext_ref[s, CONV_PAD + tt - hist:CONV_PAD + tt, :]

    ext_ref[:, 0:CONV_PAD, :] = ext_ref[:, tt:tt + CONV_PAD, :]


def _conv_sample_kernel(ga_ref, gb_ref, st_ref, wdw_ref, bdw_ref, lng_ref, lnb_ref, c_ref, nst_ref, ext_ref,
                        *, bb, t_len):
    hist = CONV_WIDTH - 1
    cin = ga_ref[...] * _sigmoid(gb_ref[...])
    bias = bdw_ref[...]
    rows = []
    for i in range(bb):
        ext_ref[i, 0:hist, :] = st_ref[i]
        ext_ref[i, hist:hist + t_len, :] = cin[i * t_len:(i + 1) * t_len, :]
        acc = jnp.broadcast_to(bias, (t_len, C_CONV))
        for j in range(CONV_WIDTH):
            acc = acc + wdw_ref[j:j + 1, :] * ext_ref[i, j:j + t_len, :]
        rows.append(acc)
        nst_ref[i] = ext_ref[i, t_len:t_len + hist, :]
    conv = jnp.concatenate(rows, axis=0)
    c_ref[...] = _ln_silu(conv, lng_ref[...], lnb_ref[...]).astype(c_ref.dtype)


def _conv_sample(proj, state, wdw, bdw, lng, lnb, nb, t_len, bb):
    hist = CONV_WIDTH - 1
    rows = bb * t_len
    return pl.pallas_call(
        functools.partial(_conv_sample_kernel, bb=bb, t_len=t_len),
        grid=(nb // bb,),
        in_specs=[
            pl.BlockSpec((rows, C_CONV), lambda i: (i, COL_GA)),
            pl.BlockSpec((rows, C_CONV), lambda i: (i, COL_GB)),
            pl.BlockSpec((bb, hist, C_CONV), lambda i: (i, 0, 0)),
            pl.BlockSpec((CONV_WIDTH, C_CONV), lambda i: (0, 0)),
            pl.BlockSpec((1, C_CONV), lambda i: (0, 0)),
            pl.BlockSpec((1, C_CONV), lambda i: (0, 0)),
            pl.BlockSpec((1, C_CONV), lambda i: (0, 0)),
        ],
        out_specs=[
            pl.BlockSpec((rows, C_CONV), lambda i: (i, 0)),
            pl.BlockSpec((bb, hist, C_CONV), lambda i: (i, 0, 0)),
        ],
        out_shape=[
            jax.ShapeDtypeStruct((nb * t_len, C_CONV), F32),
            jax.ShapeDtypeStruct((nb, hist, C_CONV), F32),
        ],
        scratch_shapes=[pltpu.VMEM((bb, hist + 2 * t_len + 8, C_CONV), F32)],
        compiler_params=_params(("parallel",)),
        name="conv_sample",
    )(proj, proj, state, wdw, bdw, lng, lnb)


DN_PAD = 8


def _split3(x):
    hi = x.astype(BF16)
    r1 = x - hi.astype(F32)
    mid = r1.astype(BF16)
    lo = (r1 - mid.astype(F32)).astype(BF16)
    return hi, mid, lo


def _delta_run(seqs, wdc_ref, arow_ref, dtrow_ref, gdn_ref, *, c, n_valid):
    hist = DN_CONV - 1
    n_steps = 0
    while (1 << (n_steps + 1)) < n_valid:
        n_steps += 1

    row = lax.broadcasted_iota(jnp.int32, (c, SMALL_W), 0)
    valid = row < n_valid
    ii = lax.broadcasted_iota(jnp.int32, (c, c), 0)
    jj = lax.broadcasted_iota(jnp.int32, (c, c), 1)
    causal_f = jnp.where(jj <= ii, 1.0, 0.0)
    strict_f = jnp.where(jj < ii, 1.0, 0.0)
    above = (causal_f - 1.0) * 1e30
    tri = causal_f.astype(BF16)
    ones = jnp.ones((DN_DK, DN_DK), BF16)
    arow = arow_ref[...]
    dtrow = dtrow_ref[...]
    gdn = gdn_ref[...]

    def conv_slab(ch, s):
        sl = slice(s * LANES, (s + 1) * LANES)
        conv = wdc_ref[0:1, sl] * ch["win"](s, DN_PAD - hist, DN_PAD - hist + c)
        for j in range(1, DN_CONV):
            conv = conv + wdc_ref[j:j + 1, sl] * ch["win"](s, DN_PAD - hist + j, DN_PAD - hist + j + c)
        return _silu(conv)

    def unit_norm_pair(x, y):
        sq = jnp.concatenate([x * x, y * y], axis=0)
        hi = sq.astype(BF16)
        lo = (sq - hi.astype(F32)).astype(BF16)
        inv = lax.rsqrt(_dot(hi, ones) + _dot(lo, ones) + 1e-6)
        return x * inv[0:c], y * inv[c:2 * c]

    chunks = [ch for seq in seqs for ch in seq]
    for ch in chunks:
        ba = ch["ba"]
        beta = _sigmoid(ba)
        g = arow * jax.nn.softplus(ba + dtrow)
        if n_valid < c:
            beta = jnp.where(valid, beta, 0.0)
            g = jnp.where(valid, g, 0.0)
        ch["beta"] = beta
        ch["g3"] = _split3(g)
    for ch in chunks:
        ghi, gmid, glo = ch["g3"]
        gam = _dot(tri, ghi) + _dot(tri, gmid) + _dot(tri, glo)
        ch["gam"] = gam
        ch["gam_t"] = gam.T

    units = []
    heads = DN_HEADS
    for ch in chunks:
        ch["units"] = []
        for h in range(heads):
            q, k = unit_norm_pair(conv_slab(ch, h), conv_slab(ch, heads + h))
            q = q * (DN_DK ** -0.5)
            v = conv_slab(ch, 2 * heads + h)
            if n_valid < c:
                k = jnp.where(valid[:, 0:1], k, 0.0)
            beta = ch["beta"][:, h:h + 1]
            gam_c = ch["gam"][:, 8 + h:9 + h]
            gam_r = ch["gam_t"][8 + h:9 + h, :]
            g_last = gam_r[:, c - 1:c]
            kb = k * beta
            u = dict(ch=ch, h=h, q=q, k=k, kb=kb, vb=v * beta, gam_c=gam_c, g_last=g_last,
                     dec=jnp.exp((gam_c - gam_r) * causal_f + above))
            ch["units"].append(u)
            units.append(u)

    for u in units:
        u["gmat"] = _dot_nt(jnp.concatenate([u["kb"], u["q"]], axis=0).astype(BF16), u["k"].astype(BF16))
    for u in units:
        gmat = u.pop("gmat")
        u["e"] = -(gmat[0:c] * (u["dec"] * strict_f))
        u["p"] = u["e"]
        u["qk"] = gmat[c:2 * c] * u.pop("dec")
    if n_steps:
        for u in units:
            p16 = u["p"].astype(BF16)
            u["p"] = _dot(p16, p16)
        for _ in range(n_steps - 1):
            for u in units:
                p16 = u["p"].astype(BF16)
                u["pe"] = _dot(jnp.concatenate([u["p"], u["e"]], axis=0).astype(BF16), p16)
            for u in units:
                pe = u.pop("pe")
                u["e"] = u["e"] + u["p"] + pe[c:2 * c]
                u["p"] = pe[0:c]
        for u in units:
            u["ep"] = _dot(u["e"].astype(BF16), u["p"].astype(BF16))
        for u in units:
            u["e"] = u["e"] + u.pop("p") + u.pop("ep")
    for u in units:
        eg = jnp.exp(u["gam_c"])
        rhs = jnp.concatenate([u.pop("vb"), u.pop("kb") * eg], axis=1)
        u["rhs"] = rhs
        u["qg"] = u.pop("q") * eg
        u["sol"] = _dot(u.pop("e").astype(BF16), rhs.astype(BF16))
    for u in units:
        sol = u.pop("rhs") + u.pop("sol")
        u["u"] = sol[:, 0:DN_DV]
        u["lhs16"] = jnp.concatenate([sol[:, DN_DV:DN_DV + DN_DK], u.pop("qg")], axis=0).astype(BF16)
        kg = u.pop("k") * jnp.exp(u["g_last"] - u["gam_c"])
        u["qk_kgt16"] = jnp.concatenate([u.pop("qk"), kg.T], axis=0).astype(BF16)
        u["decay"] = jnp.exp(u["g_last"])

    for j in range(len(seqs[0])):
        cur = [u for seq in seqs for u in seq[j]["units"]]
        for u in cur:
            u["s_old"] = u["ch"]["s"][u["h"]]
            u["xs"] = _dot(u["lhs16"], u["s_old"].astype(BF16))
        for u in cur:
            xs = u.pop("xs")
            u["w16"] = (u["u"] - xs[0:c]).astype(BF16)
            u["o"] = xs[c:2 * c]
        for u in cur:
            u["o2ds"] = _dot(u["qk_kgt16"], u["w16"])
        for u in cur:
            ch, h = u["ch"], u["h"]
            o2ds = u.pop("o2ds")
            ch["s"][h] = u["s_old"] * u["decay"] + o2ds[c:c + DN_DK]
            zh = ch["z"][:, h * DN_DV:(h + 1) * DN_DV]
            ch["store_o"](h, _rms_rows(u["o"] + o2ds[0:c], gdn) * _silu(zh))


DP_CHUNKS = 2


def _delta_prompt_kernel(q_ref, k_ref, v_ref, z_ref, ba_ref, wdc_ref, arow_ref, dtrow_ref, gdn_ref, *rest,
                         side_ranges):
    n_side = len(side_ranges)
    side_in = rest[0:n_side]
    o_ref, nd_ref, ns_ref = rest[n_side:n_side + 3]
    side_out = rest[n_side + 3:2 * n_side + 3]
    ext_ref, s_ref = rest[2 * n_side + 3:]
    t = pl.program_id(1)
    rows = q_ref.shape[0]
    c = DN_CHUNK
    hist = DN_CONV - 1

    @pl.when(t == 0)
    def _():
        ext_ref[:, 0:DN_PAD, :] = jnp.zeros((QKV_W // LANES, DN_PAD, LANES), F32)
        s_ref[...] = jnp.zeros(s_ref.shape, F32)

    n_slab = QKV_W // LANES
    for i, ref in enumerate((q_ref, k_ref, v_ref)):
        for s in range(1024 // LANES):
            ext_ref[i * (1024 // LANES) + s, DN_PAD:DN_PAD + rows, :] = ref[:, s * LANES:(s + 1) * LANES]

    seq = []
    for g in range(rows // c):
        def win(s, j0, j1, g=g):
            return ext_ref[s, g * c + j0:g * c + j1, :]

        def store_o(h, val, g=g):
            o_ref[g * c:(g + 1) * c, h * DN_DV:(h + 1) * DN_DV] = val.astype(o_ref.dtype)

        seq.append(dict(win=win, z=z_ref[g * c:(g + 1) * c, :], ba=ba_ref[g * c:(g + 1) * c, :], s=s_ref,
                        store_o=store_o))
    _delta_run([seq], wdc_ref, arow_ref, dtrow_ref, gdn_ref, c=c, n_valid=c)

    @pl.when(t == pl.num_programs(1) - 1)
    def _():
        for s in range(n_slab):
            nd_ref[0, :, s * LANES:(s + 1) * LANES] = ext_ref[s, DN_PAD + rows - hist:DN_PAD + rows, :]
        ns_ref[0] = s_ref[...]

    ext_ref[:, DN_PAD - hist:DN_PAD, :] = ext_ref[:, DN_PAD + rows - hist:DN_PAD + rows, :]

    step = pl.program_id(0) * pl.num_programs(1) + t
    for src, dst, (start, count) in zip(side_in, side_out, side_ranges):
        @pl.when((step >= start) & (step < start + count))
        def _(src=src, dst=dst):
            dst[...] = src[...].astype(BF16)


def _side_plan(side, n_steps):
    plan, start = [], 0
    for arr, count in side:
        while count > 1 and (count > n_steps or arr.shape[0] % count or (arr.shape[0] // count) % 16):
            count //= 2
        plan.append((min(start, n_steps - count), count))
        start += count
    return plan


def _delta_prompt(proj, ba, wdc, arow, dtrow, gdn, nb, t_len, side=()):
    rows = DN_CHUNK * (DP_CHUNKS if t_len % (DN_CHUNK * DP_CHUNKS) == 0 else 1)
    nt = t_len // rows
    side_ranges = _side_plan(side, nb * nt)
    side_specs = [_side_spec(arr, count, start, lambda b, t: b * nt + t)
                  for (arr, _), (start, count) in zip(side, side_ranges)]
    row_spec = lambda col: pl.BlockSpec((rows, 1024), lambda b, t: (b * nt + t, col))
    small = lambda shape: pl.BlockSpec(shape, lambda b, t: (0,) * len(shape))
    outs = pl.pallas_call(
        functools.partial(_delta_prompt_kernel, side_ranges=tuple(side_ranges)),
        grid=(nb, nt),
        in_specs=[
            row_spec(COL_Q), row_spec(COL_K), row_spec(COL_V), row_spec(COL_Z),
            pl.BlockSpec((rows, SMALL_W), lambda b, t: (b * nt + t, 0)),
            small((DN_CONV, QKV_W)), small((1, SMALL_W)), small((1, SMALL_W)), small((1, DN_DV)),
        ] + side_specs,
        out_specs=[
            pl.BlockSpec((rows, 1024), lambda b, t: (b * nt + t, 0)),
            pl.BlockSpec((1, DN_CONV - 1, QKV_W), lambda b, t: (b, 0, 0)),
            pl.BlockSpec((1, DN_HEADS, DN_DK, DN_DV), lambda b, t: (b, 0, 0, 0)),
        ] + side_specs,
        out_shape=[
            jax.ShapeDtypeStruct((nb * t_len, 1024), BF16),
            jax.ShapeDtypeStruct((nb, DN_CONV - 1, QKV_W), F32),
            jax.ShapeDtypeStruct((nb, DN_HEADS, DN_DK, DN_DV), F32),
        ] + [jax.ShapeDtypeStruct(arr.shape, BF16) for arr, _ in side],
        scratch_shapes=[pltpu.VMEM((QKV_W // LANES, DN_PAD + rows, LANES), F32),
                        pltpu.VMEM((DN_HEADS, DN_DK, DN_DV), F32)],
        compiler_params=_params(("arbitrary", "arbitrary")),
        name="delta_prompt",
    )(proj, proj, proj, proj, ba, wdc, arow, dtrow, gdn, *[arr for arr, _ in side])
    return outs[0], outs[1], outs[2], list(outs[3:])


DS_ROWS = 8
DS_GROUPS = 2


def _delta_sample_kernel(q_ref, k_ref, v_ref, z_ref, ba_ref, dst_ref, s0_ref, wdc_ref, arow_ref, dtrow_ref,
                         gdn_ref, o_ref, nd_ref, ns_ref, ext_ref, *, t_len):
    hist = DN_CONV - 1
    per = DS_ROWS // t_len
    n_slab = QKV_W // LANES
    ns_ref[...] = s0_ref[...]
    seqs = []
    for i in range(dst_ref.shape[0]):
        g0 = (i // per) * DS_ROWS
        r0 = (i % per) * t_len
        ext_ref[i] = jnp.zeros((n_slab, DN_PAD + DS_ROWS, LANES), F32)
        for s in range(n_slab):
            sl = slice(s * LANES, (s + 1) * LANES)
            ext_ref[i, s, DN_PAD - hist:DN_PAD, :] = dst_ref[i, :, sl]
            src = (q_ref, k_ref, v_ref)[s // (1024 // LANES)]
            col = (s % (1024 // LANES)) * LANES
            ext_ref[i, s, DN_PAD:DN_PAD + t_len, :] = src[g0 + r0:g0 + r0 + t_len, col:col + LANES]
            nd_ref[i, :, sl] = ext_ref[i, s, DN_PAD + t_len - hist:DN_PAD + t_len, :]
        z = z_ref[g0:g0 + DS_ROWS, :]
        ba = ba_ref[g0:g0 + DS_ROWS, :]
        if r0:
            z = pltpu.roll(z, DS_ROWS - r0, 0)
            ba = pltpu.roll(ba, DS_ROWS - r0, 0)

        def win(s, j0, j1, i=i):
            return ext_ref[i, s, j0:j1, :]

        def store_o(h, val, g0=g0, r0=r0):
            o_ref[g0 + r0:g0 + r0 + t_len, h * DN_DV:(h + 1) * DN_DV] = val[0:t_len].astype(o_ref.dtype)

        seqs.append([dict(win=win, z=z, ba=ba, s=ns_ref.at[i], store_o=store_o)])
    _delta_run(seqs, wdc_ref, arow_ref, dtrow_ref, gdn_ref, c=DS_ROWS, n_valid=t_len)


def _delta_sample(proj, ba, dstate, s0, wdc, arow, dtrow, gdn, nb, t_len):
    per = DS_ROWS // t_len
    groups = DS_GROUPS if nb % (per * DS_GROUPS) == 0 else 1
    bb = per * groups
    rows = DS_ROWS * groups
    hist = DN_CONV - 1
    row_spec = lambda col: pl.BlockSpec((rows, 1024), lambda i: (i, col))
    small = lambda shape: pl.BlockSpec(shape, lambda i: (0,) * len(shape))
    return pl.pallas_call(
        functools.partial(_delta_sample_kernel, t_len=t_len),
        grid=(nb // bb,),
        in_specs=[
            row_spec(COL_Q), row_spec(COL_K), row_spec(COL_V), row_spec(COL_Z),
            pl.BlockSpec((rows, SMALL_W), lambda i: (i, 0)),
            pl.BlockSpec((bb, hist, QKV_W), lambda i: (i, 0, 0)),
            pl.BlockSpec((bb, DN_HEADS, DN_DK, DN_DV), lambda i: (i, 0, 0, 0)),
            small((DN_CONV, QKV_W)), small((1, SMALL_W)), small((1, SMALL_W)), small((1, DN_DV)),
        ],
        out_specs=[
            pl.BlockSpec((rows, 1024), lambda i: (i, 0)),
            pl.BlockSpec((bb, hist, QKV_W), lambda i: (i, 0, 0)),
            pl.BlockSpec((bb, DN_HEADS, DN_DK, DN_DV), lambda i: (i, 0, 0, 0)),
        ],
        out_shape=[
            jax.ShapeDtypeStruct((nb * t_len, 1024), F32),
            jax.ShapeDtypeStruct((nb, hist, QKV_W), F32),
            jax.ShapeDtypeStruct((nb, DN_HEADS, DN_DK, DN_DV), F32),
        ],
        scratch_shapes=[pltpu.VMEM((bb, QKV_W // LANES, DN_PAD + DS_ROWS, LANES), F32)],
        compiler_params=_params(("parallel",)),
        name="delta_sample",
    )(proj, proj, proj, proj, ba, dstate, s0, wdc, arow, dtrow, gdn)


def _softmax_rows(s):
    m = jnp.max(s, axis=-1, keepdims=True)
    p = jnp.exp(s - m)
    return p / jnp.sum(p, axis=-1, keepdims=True)


def _attn_pairs(pairs):
    scores = [_dot_nt(q16, k.astype(BF16)) * (XA_DIM ** -0.5) for q16, k, _ in pairs]
    probs = [_softmax_rows(s).astype(BF16) for s in scores]
    return [_dot(p, v.astype(BF16)) for p, (_, _, v) in zip(probs, pairs)]


def _attn_prompt_kernel(xq_ref, k_ref, v_ref, o_ref):
    sls = [slice(h * XA_DIM, (h + 1) * XA_DIM) for h in range(XA_HEADS)]
    outs = _attn_pairs([(xq_ref[:, sl].astype(BF16), k_ref[:, sl], v_ref[:, sl]) for sl in sls])
    for sl, o in zip(sls, outs):
        o_ref[:, sl] = o.astype(o_ref.dtype)


def _attn_prompt(proj, mkv, nb, t_len, tq):
    nt = t_len // tq
    return pl.pallas_call(
        _attn_prompt_kernel,
        grid=(nb, nt),
        in_specs=[
            pl.BlockSpec((tq, XW), lambda b, t: (b * nt + t, COL_XQ)),
            pl.BlockSpec((N_MEM, XW), lambda b, t: (b, 0)),
            pl.BlockSpec((N_MEM, XW), lambda b, t: (b, 1)),
        ],
        out_specs=pl.BlockSpec((tq, XW), lambda b, t: (b * nt + t, 0)),
        out_shape=jax.ShapeDtypeStruct((nb * t_len, XW), BF16),
        compiler_params=_params(("parallel", "arbitrary")),
        name="attn_prompt",
    )(proj, mkv, mkv)


def _attn_sample_kernel(xq_ref, k_hbm, v_hbm, o_ref, kbuf, vbuf, sem, *, layer, bb, t_len):
    i = pl.program_id(0)
    slot = i % 2

    def copies(step, slot_):
        out = []
        for h in range(XA_HEADS):
            for kv, (hbm, buf) in enumerate(((k_hbm, kbuf), (v_hbm, vbuf))):
                out.append(pltpu.make_async_copy(hbm.at[layer, pl.ds(step * bb, bb), :, h, :], buf.at[slot_, h],
                                                 sem.at[kv, slot_, h]))
        return out

    @pl.when(i == 0)
    def _():
        for cp in copies(0, 0):
            cp.start()

    @pl.when(i + 1 < pl.num_programs(0))
    def _():
        for cp in copies(i + 1, 1 - slot):
            cp.start()

    for cp in copies(i, slot):
        cp.wait()

    per = DS_ROWS // t_len
    rows = lax.broadcasted_iota(jnp.int32, (DS_ROWS, XA_DIM), 0)
    pairs = []
    for b in range(bb):
        r0 = (b // per) * DS_ROWS
        for h in range(XA_HEADS):
            q16 = xq_ref[r0:r0 + DS_ROWS, h * XA_DIM:(h + 1) * XA_DIM].astype(BF16)
            pairs.append((q16, kbuf[slot, h, b], vbuf[slot, h, b]))
    outs = _attn_pairs(pairs)
    for grp in range(bb // per):
        r0 = grp * DS_ROWS
        for h in range(XA_HEADS):
            o = outs[(grp * per) * XA_HEADS + h]
            for s in range(1, per):
                o = jnp.where(rows < s * t_len, o, outs[(grp * per + s) * XA_HEADS + h])
            o_ref[r0:r0 + DS_ROWS, h * XA_DIM:(h + 1) * XA_DIM] = o.astype(o_ref.dtype)


def _attn_sample(proj, ck, cv, layer, nb, t_len, bb):
    rows = bb * t_len
    buf = pltpu.VMEM((2, XA_HEADS, bb, N_MEM, XA_DIM), F32)
    return pl.pallas_call(
        functools.partial(_attn_sample_kernel, layer=layer, bb=bb, t_len=t_len),
        grid=(nb // bb,),
        in_specs=[pl.BlockSpec((rows, XW), lambda i: (i, COL_XQ)), pl.BlockSpec(memory_space=pl.ANY),
                  pl.BlockSpec(memory_space=pl.ANY)],
        out_specs=pl.BlockSpec((rows, XW), lambda i: (i, 0)),
        out_shape=jax.ShapeDtypeStruct((nb * t_len, XW), F32),
        scratch_shapes=[buf, buf, pltpu.SemaphoreType.DMA((2, 2, XA_HEADS))],
        compiler_params=_params(("arbitrary",)),
        name="attn_sample",
    )(proj, ck, cv)


def _merge_others(d_ref, m_ref, gd_ref, gx_ref, wd_ref, wm_ref):
    mix = _sigmoid(gd_ref[...]) * _dot(d_ref[...].astype(BF16), wd_ref[...])
    return mix + _sigmoid(gx_ref[...]) * _dot(m_ref[...].astype(BF16), wm_ref[...])


def _merge_finish(mix, c16, gc_ref, x_ref, wc_ref, bc_ref, wo_ref, gpm_ref, h_ref):
    mix = mix + _sigmoid(gc_ref[...]) * (_dot(c16, wc_ref[...]) + bc_ref[...])
    out = _dot(mix.astype(BF16), wo_ref[...])
    h_ref[...] = x_ref[...] + _rms_rows(out, gpm_ref[...])


def _merge_kernel(c_ref, d_ref, m_ref, gc_ref, gd_ref, gx_ref, x_ref, wc_ref, bc_ref, wd_ref, wm_ref, wo_ref,
                  gpm_ref, h_ref):
    mix = _merge_others(d_ref, m_ref, gd_ref, gx_ref, wd_ref, wm_ref)
    _merge_finish(mix, c_ref[...].astype(BF16), gc_ref, x_ref, wc_ref, bc_ref, wo_ref, gpm_ref, h_ref)


def _merge_conv_kernel(ga_ref, gb_ref, wdw_ref, bdw_ref, lng_ref, lnb_ref, d_ref, m_ref, gc_ref, gd_ref, gx_ref,
                       x_ref, wc_ref, bc_ref, wd_ref, wm_ref, wo_ref, gpm_ref, h_ref, st_ref, ext_ref, conv_ref,
                       c_ref):
    t = pl.program_id(1)
    _conv_tile_start(ext_ref, t)
    mix = _merge_others(d_ref, m_ref, gd_ref, gx_ref, wd_ref, wm_ref)
    _conv_tile(ga_ref, gb_ref, wdw_ref, bdw_ref, lng_ref, lnb_ref, ext_ref, conv_ref, c_ref)
    _merge_finish(mix, c_ref[...], gc_ref, x_ref, wc_ref, bc_ref, wo_ref, gpm_ref, h_ref)
    _conv_tile_finish(st_ref, ext_ref, ga_ref.shape[0], t, pl.num_programs(1))


_MERGE_WEIGHT_SHAPES = ((C_CONV, D_MODEL), (1, D_MODEL), (1024, D_MODEL), (XW, D_MODEL), (D_MODEL, D_MODEL),
                        (1, D_MODEL))


def _merge(c, d, m, proj, x2d, wc, bc, wd, wm, wo, gpm, tm):
    rows = x2d.shape[0]
    row = lambda width, col=0: pl.BlockSpec((tm, width), lambda i: (i, col))
    return pl.pallas_call(
        _merge_kernel,
        grid=(rows // tm,),
        in_specs=[
            row(1024), row(1024), row(1024),
            row(D_MODEL, 0), row(D_MODEL, 1), row(D_MODEL, 2),
            row(D_MODEL),
        ] + [_const_spec(shape) for shape in _MERGE_WEIGHT_SHAPES],
        out_specs=row(D_MODEL),
        out_shape=jax.ShapeDtypeStruct((rows, D_MODEL), F32),
        compiler_params=_params(("parallel",)),
        name="merge",
    )(c, d, m, proj, proj, proj, x2d, wc, bc, wd, wm, wo, gpm)


def _merge_conv(d, m, proj, x2d, wdw, bdw, lng, lnb, wc, bc, wd, wm, wo, gpm, nb, t_len, tt):
    nt = t_len // tt
    row = lambda width, col=0: pl.BlockSpec((tt, width), lambda b, t: (b * nt + t, col))
    vec = lambda n: pl.BlockSpec((1, n), lambda b, t: (0, 0))
    return pl.pallas_call(
        _merge_conv_kernel,
        grid=(nb, nt),
        in_specs=[
            row(C_CONV, COL_GA), row(C_CONV, COL_GB),
            pl.BlockSpec((CONV_WIDTH, C_CONV), lambda b, t: (0, 0)), vec(C_CONV), vec(C_CONV), vec(C_CONV),
            row(1024), row(1024),
            row(D_MODEL, 0), row(D_MODEL, 1), row(D_MODEL, 2),
            row(D_MODEL),
        ] + [_const_spec(shape) for shape in _MERGE_WEIGHT_SHAPES],
        out_specs=[row(D_MODEL), pl.BlockSpec((1, CONV_WIDTH - 1, C_CONV), lambda b, t: (b, 0, 0))],
        out_shape=[
            jax.ShapeDtypeStruct((nb * t_len, D_MODEL), F32),
            jax.ShapeDtypeStruct((nb, CONV_WIDTH - 1, C_CONV), F32),
        ],
        scratch_shapes=[pltpu.VMEM((C_CONV // LANES, CONV_PAD + tt, LANES), F32), pltpu.VMEM((tt, C_CONV), F32),
                        pltpu.VMEM((tt, C_CONV), BF16)],
        compiler_params=pltpu.CompilerParams(dimension_semantics=("parallel", "arbitrary"),
                                             vmem_limit_bytes=MERGE_CONV_VMEM_LIMIT),
        name="merge_conv",
    )(proj, proj, wdw, bdw, lng, lnb, d, m, proj, proj, proj, x2d, wc, bc, wd, wm, wo, gpm)


FFN_TF = 512
FFN_TN = 512


def _ffn_kernel(h_ref, g1_ref, wg_ref, wu_ref, wd_ref, g2_ref, y_ref, fn_ref, act_ref, *, n_up):
    f = pl.program_id(1)

    @pl.when(f == 0)
    def _():
        _norm_rows_to(h_ref, g1_ref, fn_ref, 64)

    @pl.when(f < n_up)
    def _():
        fn = fn_ref[...]
        col = pl.multiple_of(f * FFN_TF, FFN_TF)
        act_ref[:, pl.ds(col, FFN_TF)] = (_silu(_dot(fn, wg_ref[...])) * _dot(fn, wu_ref[...])).astype(BF16)

    @pl.when(f >= n_up)
    def _():
        col = pl.multiple_of((f - n_up) * FFN_TN, FFN_TN)
        y_ref[:, pl.ds(col, FFN_TN)] = _dot(act_ref[...], wd_ref[...])

    @pl.when(f == pl.num_programs(1) - 1)
    def _():
        g2 = g2_ref[...]
        rows = 64

        def body(i, c):
            r = pl.multiple_of(i * rows, rows)
            y_ref[pl.ds(r, rows), :] = h_ref[pl.ds(r, rows), :] + _rms_rows(y_ref[pl.ds(r, rows), :], g2)
            return c

        lax.fori_loop(0, h_ref.shape[0] // rows, body, 0)


def _ffn(h2d, g1, wg, wu, wd, g2, tm):
    rows = h2d.shape[0]
    d_ff = wg.shape[1]
    n_up = d_ff // FFN_TF
    n_down = D_MODEL // FFN_TN
    return pl.pallas_call(
        functools.partial(_ffn_kernel, n_up=n_up),
        grid=(rows // tm, n_up + n_down),
        in_specs=[
            pl.BlockSpec((tm, D_MODEL), lambda i, f: (i, 0)),
            pl.BlockSpec((1, D_MODEL), lambda i, f: (0, 0)),
            pl.BlockSpec((D_MODEL, FFN_TF), lambda i, f: (0, jnp.minimum(f, n_up - 1))),
            pl.BlockSpec((D_MODEL, FFN_TF), lambda i, f: (0, jnp.minimum(f, n_up - 1))),
            pl.BlockSpec((d_ff, FFN_TN), lambda i, f: (0, jnp.maximum(f - n_up, 0))),
            pl.BlockSpec((1, D_MODEL), lambda i, f: (0, 0)),
        ],
        out_specs=pl.BlockSpec((tm, D_MODEL), lambda i, f: (i, 0)),
        out_shape=jax.ShapeDtypeStruct((rows, D_MODEL), F32),
        scratch_shapes=[pltpu.VMEM((tm, D_MODEL), BF16), pltpu.VMEM((tm, d_ff), BF16)],
        compiler_params=_params(("parallel", "arbitrary")),
        name="ffn",
    )(h2d, g1, wg, wu, wd, g2)


def _row(v):
    return v.reshape(1, -1).astype(F32)


def _layer_weights(w_in, w_dw, b_dw, ln_g, ln_b, w_p_conv, b_p_conv, w_dconv, a_log, dt_bias, g_dn_norm,
                   w_p_delta, w_p_mem, w_o, w_gate, w_up, w_down):
    lane_pad = (0, SMALL_W - 2 * DN_HEADS)
    arow = jnp.pad(jnp.concatenate([jnp.zeros((DN_HEADS,), F32), -jnp.exp(a_log.astype(F32))]), lane_pad)
    dtrow = jnp.pad(jnp.concatenate([jnp.zeros((DN_HEADS,), F32), dt_bias.astype(F32)]), lane_pad)
    return dict(
        w_t=jnp.swapaxes(w_in, 0, 1), w_dw=w_dw, b_dw=_row(b_dw), ln_g=_row(ln_g), ln_b=_row(ln_b),
        w_p_conv=w_p_conv, b_p_conv=_row(b_p_conv), w_dconv=w_dconv, arow=_row(arow),
        dtrow=_row(dtrow), g_dn=_row(g_dn_norm), w_p_delta=w_p_delta, w_p_mem=w_p_mem, w_o=w_o, w_gate=w_gate,
        w_up=w_up, w_down=w_down)


def _tile(n, pref):
    return pref if n % pref == 0 else n


def kernel(x_prompt, x_sample, mem_prompt, cache_mem_k, cache_mem_v, state_conv, state_delta_conv, state_delta, g_pre_mix, w_in, w_dw, b_dw, ln_g, ln_b, w_p_conv, b_p_conv, w_dconv, a_log, dt_bias, g_dn_norm, w_p_delta, g_mem, w_mem_kv, w_p_mem, w_o, g_post_mix, g_pre_ffn, w_gate, w_up, w_down, g_post_ffn):
    depth = w_in.shape[0]
    assert depth == 1, "single-layer stack"
    l = 0
    pb, pt, _ = x_prompt.shape
    sb, st, _ = x_sample.shape
    assert pt % DN_CHUNK == 0 and st <= DS_ROWS and DS_ROWS % st == 0

    lw = _layer_weights(w_in[l], w_dw[l], b_dw[l], ln_g[l], ln_b[l], w_p_conv[l], b_p_conv[l], w_dconv[l],
                        a_log[l], dt_bias[l], g_dn_norm[l], w_p_delta[l], w_p_mem[l], w_o[l], w_gate[l],
                        w_up[l], w_down[l])
    gpre, gpm, gff1, gff2 = _row(g_pre_mix[l]), _row(g_post_mix[l]), _row(g_pre_ffn[l]), _row(g_post_ffn[l])

    def ffn(h):
        return _ffn(h, gff1, *ffn_w, w_down16, gff2, _tile(h.shape[0], 512))

    xp = x_prompt.reshape(pb * pt, D_MODEL)
    mkv = _norm_proj(mem_prompt.reshape(pb * N_MEM, D_MODEL), _row(g_mem[l]), w_mem_kv[l].astype(BF16),
                     _tile(pb * N_MEM, 512), 1024)
    proj_p, ba_p, ffn_w = _in_proj(xp, gpre, lw["w_t"], _tile(pb * pt, 1024), side=(lw["w_gate"], lw["w_up"]))
    d_p, dconv_p, s_p, (w_down16, w_o16, w_pc16, w_pd16, w_pm16) = _delta_prompt(
        proj_p, ba_p, lw["w_dconv"], lw["arow"], lw["dtrow"], lw["g_dn"], pb, pt,
        side=((lw["w_down"], 32), (lw["w_o"], 8), (lw["w_p_conv"], 8), (lw["w_p_delta"], 8), (lw["w_p_mem"], 8)))
    merge_w = (w_pc16, lw["b_p_conv"], w_pd16, w_pm16, w_o16, gpm)
    m_p = _attn_prompt(proj_p, mkv, pb, pt, _tile(pt, 512))
    h_p, conv_p = _merge_conv(d_p, m_p, proj_p, xp, lw["w_dw"], lw["b_dw"], lw["ln_g"], lw["ln_b"], *merge_w, pb, pt,
                              _tile(pt, 256))
    y_p = ffn(h_p).reshape(pb, pt, D_MODEL)
    mk = mkv[:, 0:XW].reshape(pb, N_MEM, XA_HEADS, XA_DIM)
    mv = mkv[:, XW:2 * XW].reshape(pb, N_MEM, XA_HEADS, XA_DIM)

    xs = x_sample.reshape(sb * st, D_MODEL)
    proj_s, ba_s, _ = _in_proj(xs, gpre, lw["w_t"], _tile(sb * st, 512))
    c_s, conv_s = _conv_sample(proj_s, state_conv[l], lw["w_dw"], lw["b_dw"], lw["ln_g"], lw["ln_b"], sb, st,
                               _tile(sb, 8))
    d_s, dconv_s, s_s = _delta_sample(proj_s, ba_s, state_delta_conv[l], state_delta[l], lw["w_dconv"],
                                      lw["arow"], lw["dtrow"], lw["g_dn"], sb, st)
    m_s = _attn_sample(proj_s, cache_mem_k, cache_mem_v, l, sb, st, 4)
    h_s = _merge(c_s, d_s, m_s, proj_s, xs, *merge_w, _tile(sb * st, 256))
    y_s = ffn(h_s).reshape(sb, st, D_MODEL)

    return (y_p, y_s, mk[None], mv[None], conv_p[None], dconv_p[None], s_p[None],
            conv_s[None], dconv_s[None], s_s[None])
```

```python
import functools

import jax
import jax.numpy as jnp
from jax import lax
from jax.experimental import pallas as pl
from jax.experimental.pallas import tpu as pltpu

F32 = jnp.float32
BF16 = jnp.bfloat16

D_MODEL = 2048
N_MEM = 256
C_CONV = 1024
CONV_WIDTH = 31
DN_HEADS = 8
DN_DK = 128
DN_DV = 128
DN_CONV = 4
DN_CHUNK = 64
XA_HEADS = 4
XA_DIM = 256
QKV_W = DN_HEADS * (2 * DN_DK + DN_DV)
XW = XA_HEADS * XA_DIM

COL_GA, COL_GB = 6, 7
COL_Q, COL_K, COL_V = 8, 9, 10
COL_Z = 11
COL_XQ = 12
N_MAIN = 13 * 1024
SMALL_W = 128

VMEM_LIMIT = 52 * 1024 * 1024
MERGE_CONV_VMEM_LIMIT = 56 * 1024 * 1024

_NT = (((1,), (1,)), ((), ()))


def _dot(a, b):
    return jnp.dot(a, b, preferred_element_type=F32)


def _dot_nt(a, b):
    return lax.dot_general(a, b, _NT, preferred_element_type=F32)


def _sigmoid(x):
    return jax.nn.sigmoid(x)


def _silu(x):
    return x * jax.nn.sigmoid(x)


def _rms_rows(x, g, eps=1e-6):
    ms = jnp.mean(x * x, axis=-1, keepdims=True)
    return x * lax.rsqrt(ms + eps) * g


def _params(sem):
    return pltpu.CompilerParams(dimension_semantics=sem, vmem_limit_bytes=VMEM_LIMIT)


def _const_spec(shape):
    nd = len(shape)
    return pl.BlockSpec(shape, lambda *_: (0,) * nd, pipeline_mode=pl.Buffered(1))


def _norm_rows_to(x_ref, g_ref, u_ref, rows):
    g = g_ref[...]
    tm = x_ref.shape[0]

    def body(i, c):
        r = pl.multiple_of(i * rows, rows)
        xx = x_ref[pl.ds(r, rows), :]
        u_ref[pl.ds(r, rows), :] = _rms_rows(xx, g).astype(BF16)
        return c

    lax.fori_loop(0, tm // rows, body, 0)


def _norm_cast_kernel(x_ref, g_ref, wba_ref, u_ref, ba_ref):
    _norm_rows_to(x_ref, g_ref, u_ref, 64)
    ba_ref[...] = _dot_nt(u_ref[...], wba_ref[...].astype(BF16))


def _norm_cast(x2d, g, w_t, tm):
    m, k = x2d.shape
    return pl.pallas_call(
        _norm_cast_kernel,
        grid=(m // tm,),
        in_specs=[
            pl.BlockSpec((tm, k), lambda i: (i, 0)),
            pl.BlockSpec((1, k), lambda i: (0, 0)),
            pl.BlockSpec((pl.Element(SMALL_W), pl.Element(k)), lambda i: (i * 0 + IN_B0, 0)),
        ],
        out_specs=[pl.BlockSpec((tm, k), lambda i: (i, 0)), pl.BlockSpec((tm, SMALL_W), lambda i: (i, 0))],
        out_shape=[jax.ShapeDtypeStruct((m, k), BF16), jax.ShapeDtypeStruct((m, SMALL_W), F32)],
        compiler_params=_params(("parallel",)),
        name="norm_cast",
    )(x2d, g, w_t)


def _side_spec(arr, count, start, step_of):
    rows = arr.shape[0] // count
    assert arr.shape[0] % count == 0 and rows % 16 == 0
    return pl.BlockSpec((rows, arr.shape[1]), lambda *idx: (jnp.clip(step_of(*idx) - start, 0, count - 1), 0))


def _proj_wt_kernel(*refs, n_cast):
    u_ref, w_ref = refs[0:2]
    side_in = refs[2:2 + n_cast]
    o_ref = refs[2 + n_cast]
    side_out = refs[3 + n_cast:3 + 2 * n_cast]
    wb_ref = refs[3 + 2 * n_cast]

    @pl.when(pl.program_id(1) == 0)
    def _():
        rows = 64

        def body(i, c):
            r = pl.multiple_of(i * rows, rows)
            wb_ref[pl.ds(r, rows), :] = w_ref[pl.ds(r, rows), :].astype(BF16)
            return c

        lax.fori_loop(0, w_ref.shape[0] // rows, body, 0)

    o_ref[...] = _dot_nt(u_ref[...], wb_ref[...])
    for src, dst in zip(side_in, side_out):
        dst[...] = src[...].astype(BF16)


def _proj_wt(u, w_t, row_offset, n_blocks, tn, tm, side=()):
    m, k = u.shape
    n_i = m // tm
    n_steps = n_blocks * n_i

    count = 1
    while side and 2 * count * len(side) <= n_steps and 2 * count <= SIDE_BLOCKS_MAX:
        count *= 2

    def side_spec(arr, start):
        return _side_spec(arr, count, start, lambda j, i: j * n_i + i)

    side_specs = [side_spec(arr, n * count) for n, arr in enumerate(side)]
    outs = pl.pallas_call(
        functools.partial(_proj_wt_kernel, n_cast=len(side)),
        grid=(n_blocks, n_i),
        in_specs=[
            pl.BlockSpec((tm, k), lambda j, i: (i, 0)),
            pl.BlockSpec((pl.Element(tn), pl.Element(k)), lambda j, i: (row_offset(j), 0)),
        ] + side_specs,
        out_specs=[pl.BlockSpec((tm, tn), lambda j, i: (i, j))] + side_specs,
        out_shape=[jax.ShapeDtypeStruct((m, n_blocks * tn), F32)]
        + [jax.ShapeDtypeStruct(arr.shape, BF16) for arr in side],
        scratch_shapes=[pltpu.VMEM((tn, k), BF16)],
        compiler_params=_params(("arbitrary", "arbitrary")),
        name="in_proj",
    )(u, w_t, *side)
    return outs[0], list(outs[1:])


IN_GLU0 = 0
IN_B0 = 2 * C_CONV + QKV_W + DN_HEADS * DN_DV
IN_XQ0 = IN_B0 + 2 * DN_HEADS
IN_GATES0 = IN_XQ0 + XW


SIDE_BLOCKS_MAX = 32
ROW_ALIGN = 16


def _main_row_offset(j):
    unit = 1024 // ROW_ALIGN
    units = jnp.where(j < 6, IN_GATES0 // ROW_ALIGN + unit * j,
                      jnp.where(j < 12, IN_GLU0 // ROW_ALIGN + unit * (j - 6), IN_XQ0 // ROW_ALIGN))
    return units * ROW_ALIGN


def _in_proj(x2d, g, w_t, tm, side=()):
    u, ba = _norm_cast(x2d, g, w_t, _tile(x2d.shape[0], 512))
    proj, side16 = _proj_wt(u, w_t, _main_row_offset, N_MAIN // 1024, 1024, tm, side)
    return proj, ba, side16


def _norm_proj_kernel(x_ref, g_ref, w_ref, o_ref, u_ref):
    @pl.when(pl.program_id(1) == 0)
    def _():
        _norm_rows_to(x_ref, g_ref, u_ref, 64)

    o_ref[...] = _dot(u_ref[...], w_ref[...])


def _norm_proj(x2d, g, w, tm, tn):
    m, k = x2d.shape
    n = w.shape[1]
    return pl.pallas_call(
        _norm_proj_kernel,
        grid=(m // tm, n // tn),
        in_specs=[
            pl.BlockSpec((tm, k), lambda i, j: (i, 0)),
            pl.BlockSpec((1, k), lambda i, j: (0, 0)),
            pl.BlockSpec((k, tn), lambda i, j: (0, j)),
        ],
        out_specs=pl.BlockSpec((tm, tn), lambda i, j: (i, j)),
        out_shape=jax.ShapeDtypeStruct((m, n), F32),
        scratch_shapes=[pltpu.VMEM((tm, k), BF16)],
        compiler_params=_params(("parallel", "arbitrary")),
        name="mem_kv_proj",
    )(x2d, g, w)


def _ln_silu(c, g, b, eps=1e-5):
    mu = jnp.mean(c, axis=-1, keepdims=True)
    xc = c - mu
    var = jnp.mean(xc * xc, axis=-1, keepdims=True)
    return _silu(xc * lax.rsqrt(var + eps) * g + b)


CONV_PAD = 32
CONV_ROWS = 32
LANES = 128


def _conv_slabs():
    return [slice(s * LANES, (s + 1) * LANES) for s in range(C_CONV // LANES)]


def _conv_tile_start(ext_ref, t):
    @pl.when(t == 0)
    def _():
        ext_ref[:, 0:CONV_PAD, :] = jnp.zeros((C_CONV // LANES, CONV_PAD, LANES), F32)


def _conv_tile(ga_ref, gb_ref, wdw_ref, bdw_ref, lng_ref, lnb_ref, ext_ref, conv_ref, c_ref):
    tt = ga_ref.shape[0]
    hist = CONV_WIDTH - 1
    slabs = _conv_slabs()
    cin = ga_ref[...] * _sigmoid(gb_ref[...])
    for s, sl in enumerate(slabs):
        ext_ref[s, CONV_PAD:CONV_PAD + tt, :] = cin[:, sl]
    blocks = range(0, tt, CONV_ROWS)
    for s, sl in enumerate(slabs):
        w_all = wdw_ref[:, sl]
        accs = [jnp.broadcast_to(bdw_ref[:, sl], (CONV_ROWS, LANES)) for _ in blocks]
        for j in range(CONV_WIDTH):
            w = jnp.broadcast_to(w_all[j:j + 1, :], (CONV_ROWS, LANES))
            for i, r0 in enumerate(blocks):
                off = CONV_PAD - hist + r0 + j
                accs[i] = accs[i] + w * ext_ref[s, off:off + CONV_ROWS, :]
        for i, r0 in enumerate(blocks):
            conv_ref[r0:r0 + CONV_ROWS, sl] = accs[i]
    lng = lng_ref[...]
    lnb = lnb_ref[...]
    for r0 in blocks:
        c_ref[r0:r0 + CONV_ROWS, :] = _ln_silu(conv_ref[r0:r0 + CONV_ROWS, :], lng, lnb).astype(c_ref.dtype)


def _conv_tile_finish(st_ref, ext_ref, tt, t, n_t):
    hist = CONV_WIDTH - 1

    @pl.when(t == n_t - 1)
    def _():
        for s, sl in enumerate(_conv_slabs()):
            st_ref[0, :, sl] = ext_ref[s, CONV_PAD + tt - hist:CONV_PAD + tt, :]

    ext_ref[:, 0:CONV_PAD, :] = ext_ref[:, tt:tt + CONV_PAD, :]


def _conv_sample_kernel(ga_ref, gb_ref, st_ref, wdw_ref, bdw_ref, lng_ref, lnb_ref, c_ref, nst_ref, *, bb, t_len):
    hist = CONV_WIDTH - 1
    rows = bb * t_len
    cin3 = _split3(ga_ref[...] * _sigmoid(gb_ref[...]))
    b_i = lax.broadcasted_iota(jnp.int32, (bb, rows), 0)
    r_i = lax.broadcasted_iota(jnp.int32, (bb, rows), 1)
    r_o = lax.broadcasted_iota(jnp.int32, (rows, bb), 0)
    b_o = lax.broadcasted_iota(jnp.int32, (rows, bb), 1)
    new = []
    for t in range(t_len):
        pick = jnp.where(r_i == b_i * t_len + t, 1.0, 0.0).astype(BF16)
        new.append(_dot(pick, cin3[0]) + _dot(pick, cin3[1]) + _dot(pick, cin3[2]))

    def window(r):
        return st_ref[r] if r < hist else new[r - hist]

    bias = bdw_ref[...]
    lng = lng_ref[...]
    lnb = lnb_ref[...]
    out = jnp.zeros((rows, C_CONV), F32)
    for t in range(t_len):
        acc = jnp.broadcast_to(bias, (bb, C_CONV))
        for j in range(CONV_WIDTH):
            acc = acc + wdw_ref[j:j + 1, :] * window(t + j)
        y3 = _split3(_ln_silu(acc, lng, lnb))
        put = jnp.where(r_o == b_o * t_len + t, 1.0, 0.0).astype(BF16)
        out = out + _dot(put, y3[0]) + _dot(put, y3[1]) + _dot(put, y3[2])
    c_ref[...] = out.astype(c_ref.dtype)
    for r in range(hist):
        nst_ref[r] = window(r + t_len)


def _conv_sample(proj, state_t, wdw, bdw, lng, lnb, nb, t_len, bb):
    hist = CONV_WIDTH - 1
    rows = bb * t_len
    return pl.pallas_call(
        functools.partial(_conv_sample_kernel, bb=bb, t_len=t_len),
        grid=(nb // bb,),
        in_specs=[
            pl.BlockSpec((rows, C_CONV), lambda i: (i, COL_GA)),
            pl.BlockSpec((rows, C_CONV), lambda i: (i, COL_GB)),
            pl.BlockSpec((hist, bb, C_CONV), lambda i: (0, i, 0)),
            pl.BlockSpec((CONV_WIDTH, C_CONV), lambda i: (0, 0)),
            pl.BlockSpec((1, C_CONV), lambda i: (0, 0)),
            pl.BlockSpec((1, C_CONV), lambda i: (0, 0)),
            pl.BlockSpec((1, C_CONV), lambda i: (0, 0)),
        ],
        out_specs=[
            pl.BlockSpec((rows, C_CONV), lambda i: (i, 0)),
            pl.BlockSpec((hist, bb, C_CONV), lambda i: (0, i, 0)),
        ],
        out_shape=[
            jax.ShapeDtypeStruct((nb * t_len, C_CONV), F32),
            jax.ShapeDtypeStruct((hist, nb, C_CONV), F32),
        ],
        compiler_params=_params(("parallel",)),
        name="conv_sample",
    )(proj, proj, state_t, wdw, bdw, lng, lnb)


DN_PAD = 8


def _split3(x):
    hi = x.astype(BF16)
    r1 = x - hi.astype(F32)
    mid = r1.astype(BF16)
    lo = (r1 - mid.astype(F32)).astype(BF16)
    return hi, mid, lo


def _delta_run(seqs, wdc_ref, arow_ref, dtrow_ref, gdn_ref, *, c, n_valid):
    hist = DN_CONV - 1
    n_steps = 0
    while (1 << (n_steps + 1)) < n_valid:
        n_steps += 1

    row = lax.broadcasted_iota(jnp.int32, (c, SMALL_W), 0)
    valid = row < n_valid
    ii = lax.broadcasted_iota(jnp.int32, (c, c), 0)
    jj = lax.broadcasted_iota(jnp.int32, (c, c), 1)
    causal_f = jnp.where(jj <= ii, 1.0, 0.0)
    strict_f = jnp.where(jj < ii, 1.0, 0.0)
    above = (causal_f - 1.0) * 1e30
    tri = causal_f.astype(BF16)
    ones = jnp.ones((DN_DK, DN_DK), BF16)
    arow = arow_ref[...]
    dtrow = dtrow_ref[...]
    gdn = gdn_ref[...]

    def conv_slab(ch, s):
        sl = slice(s * LANES, (s + 1) * LANES)
        conv = wdc_ref[0:1, sl] * ch["win"](s, DN_PAD - hist, DN_PAD - hist + c)
        for j in range(1, DN_CONV):
            conv = conv + wdc_ref[j:j + 1, sl] * ch["win"](s, DN_PAD - hist + j, DN_PAD - hist + j + c)
        return _silu(conv)

    def unit_norm_pair(x, y):
        sq = jnp.concatenate([x * x, y * y], axis=0)
        hi = sq.astype(BF16)
        lo = (sq - hi.astype(F32)).astype(BF16)
        inv = lax.rsqrt(_dot(hi, ones) + _dot(lo, ones) + 1e-6)
        return x * inv[0:c], y * inv[c:2 * c]

    chunks = [ch for seq in seqs for ch in seq]
    for ch in chunks:
        ba = ch["ba"]
        beta = _sigmoid(ba)
        g = arow * jax.nn.softplus(ba + dtrow)
        if n_valid < c:
            beta = jnp.where(valid, beta, 0.0)
            g = jnp.where(valid, g, 0.0)
        ch["beta"] = beta
        ch["g3"] = _split3(g)
    for ch in chunks:
        ghi, gmid, glo = ch["g3"]
        gam = _dot(tri, ghi) + _dot(tri, gmid) + _dot(tri, glo)
        ch["gam"] = gam
        ch["gam_t"] = gam.T

    units = []
    heads = DN_HEADS
    for ch in chunks:
        ch["units"] = []
        for h in range(heads):
            q, k = unit_norm_pair(conv_slab(ch, h), conv_slab(ch, heads + h))
            q = q * (DN_DK ** -0.5)
            v = conv_slab(ch, 2 * heads + h)
            if n_valid < c:
                k = jnp.where(valid[:, 0:1], k, 0.0)
            beta = ch["beta"][:, h:h + 1]
            gam_c = ch["gam"][:, 8 + h:9 + h]
            gam_r = ch["gam_t"][8 + h:9 + h, :]
            g_last = gam_r[:, c - 1:c]
            kb = k * beta
            u = dict(ch=ch, h=h, q=q, k=k, kb=kb, vb=v * beta, gam_c=gam_c, g_last=g_last,
                     dec=jnp.exp((gam_c - gam_r) * causal_f + above))
            ch["units"].append(u)
            units.append(u)

    for u in units:
        u["gmat"] = _dot_nt(jnp.concatenate([u["kb"], u["q"]], axis=0).astype(BF16), u["k"].astype(BF16))
    for u in units:
        gmat = u.pop("gmat")
        u["e"] = -(gmat[0:c] * (u["dec"] * strict_f))
        u["p"] = u["e"]
        u["qk"] = gmat[c:2 * c] * u.pop("dec")
    if n_steps:
        for u in units:
            p16 = u["p"].astype(BF16)
            u["p"] = _dot(p16, p16)
        for _ in range(n_steps - 1):
            for u in units:
                p16 = u["p"].astype(BF16)
                u["pe"] = _dot(jnp.concatenate([u["p"], u["e"]], axis=0).astype(BF16), p16)
            for u in units:
                pe = u.pop("pe")
                u["e"] = u["e"] + u["p"] + pe[c:2 * c]
                u["p"] = pe[0:c]
        for u in units:
            u["ep"] = _dot(u["e"].astype(BF16), u["p"].astype(BF16))
        for u in units:
            u["e"] = u["e"] + u.pop("p") + u.pop("ep")
    for u in units:
        eg = jnp.exp(u["gam_c"])
        rhs = jnp.concatenate([u.pop("vb"), u.pop("kb") * eg], axis=1)
        u["rhs"] = rhs
        u["qg"] = u.pop("q") * eg
        u["sol"] = _dot(u.pop("e").astype(BF16), rhs.astype(BF16))
    for u in units:
        sol = u.pop("rhs") + u.pop("sol")
        u["u"] = sol[:, 0:DN_DV]
        u["lhs16"] = jnp.concatenate([sol[:, DN_DV:DN_DV + DN_DK], u.pop("qg")], axis=0).astype(BF16)
        kg = u.pop("k") * jnp.exp(u["g_last"] - u["gam_c"])
        u["qk_kgt16"] = jnp.concatenate([u.pop("qk"), kg.T], axis=0).astype(BF16)
        u["decay"] = jnp.exp(u["g_last"])

    for j in range(len(seqs[0])):
        cur = [u for seq in seqs for u in seq[j]["units"]]
        for u in cur:
            u["s_old"] = u["ch"]["s"][u["h"]]
            u["xs"] = _dot(u["lhs16"], u["s_old"].astype(BF16))
        for u in cur:
            xs = u.pop("xs")
            u["w16"] = (u["u"] - xs[0:c]).astype(BF16)
            u["o"] = xs[c:2 * c]
        for u in cur:
            u["o2ds"] = _dot(u["qk_kgt16"], u["w16"])
        for u in cur:
            ch, h = u["ch"], u["h"]
            o2ds = u.pop("o2ds")
            ch["s"][h] = u["s_old"] * u["decay"] + o2ds[c:c + DN_DK]
            zh = ch["z"][:, h * DN_DV:(h + 1) * DN_DV]
            ch["store_o"](h, _rms_rows(u["o"] + o2ds[0:c], gdn) * _silu(zh))


DP_CHUNKS = 2


def _delta_prompt_kernel(q_ref, k_ref, v_ref, z_ref, ba_ref, wdc_ref, arow_ref, dtrow_ref, gdn_ref, *rest,
                         side_ranges):
    n_side = len(side_ranges)
    side_in = rest[0:n_side]
    o_ref, nd_ref, ns_ref = rest[n_side:n_side + 3]
    side_out = rest[n_side + 3:2 * n_side + 3]
    ext_ref, s_ref = rest[2 * n_side + 3:]
    t = pl.program_id(1)
    rows = q_ref.shape[0]
    c = DN_CHUNK
    hist = DN_CONV - 1

    @pl.when(t == 0)
    def _():
        ext_ref[:, 0:DN_PAD, :] = jnp.zeros((QKV_W // LANES, DN_PAD, LANES), F32)
        s_ref[...] = jnp.zeros(s_ref.shape, F32)

    n_slab = QKV_W // LANES
    for i, ref in enumerate((q_ref, k_ref, v_ref)):
        for s in range(1024 // LANES):
            ext_ref[i * (1024 // LANES) + s, DN_PAD:DN_PAD + rows, :] = ref[:, s * LANES:(s + 1) * LANES]

    seq = []
    for g in range(rows // c):
        def win(s, j0, j1, g=g):
            return ext_ref[s, g * c + j0:g * c + j1, :]

        def store_o(h, val, g=g):
            o_ref[g * c:(g + 1) * c, h * DN_DV:(h + 1) * DN_DV] = val.astype(o_ref.dtype)

        seq.append(dict(win=win, z=z_ref[g * c:(g + 1) * c, :], ba=ba_ref[g * c:(g + 1) * c, :], s=s_ref,
                        store_o=store_o))
    _delta_run([seq], wdc_ref, arow_ref, dtrow_ref, gdn_ref, c=c, n_valid=c)

    @pl.when(t == pl.num_programs(1) - 1)
    def _():
        for s in range(n_slab):
            nd_ref[0, :, s * LANES:(s + 1) * LANES] = ext_ref[s, DN_PAD + rows - hist:DN_PAD + rows, :]
        ns_ref[0] = s_ref[...]

    ext_ref[:, DN_PAD - hist:DN_PAD, :] = ext_ref[:, DN_PAD + rows - hist:DN_PAD + rows, :]

    step = pl.program_id(0) * pl.num_programs(1) + t
    for src, dst, (start, count) in zip(side_in, side_out, side_ranges):
        @pl.when((step >= start) & (step < start + count))
        def _(src=src, dst=dst):
            dst[...] = src[...].astype(BF16)


def _side_plan(side, n_steps):
    plan, start = [], 0
    for arr, count in side:
        while count > 1 and (count > n_steps or arr.shape[0] % count or (arr.shape[0] // count) % 16):
            count //= 2
        plan.append((min(start, n_steps - count), count))
        start += count
    return plan


def _delta_prompt(proj, ba, wdc, arow, dtrow, gdn, nb, t_len, side=()):
    rows = DN_CHUNK * (DP_CHUNKS if t_len % (DN_CHUNK * DP_CHUNKS) == 0 else 1)
    nt = t_len // rows
    side_ranges = _side_plan(side, nb * nt)
    side_specs = [_side_spec(arr, count, start, lambda b, t: b * nt + t)
                  for (arr, _), (start, count) in zip(side, side_ranges)]
    row_spec = lambda col: pl.BlockSpec((rows, 1024), lambda b, t: (b * nt + t, col))
    small = lambda shape: pl.BlockSpec(shape, lambda b, t: (0,) * len(shape))
    outs = pl.pallas_call(
        functools.partial(_delta_prompt_kernel, side_ranges=tuple(side_ranges)),
        grid=(nb, nt),
        in_specs=[
            row_spec(COL_Q), row_spec(COL_K), row_spec(COL_V), row_spec(COL_Z),
            pl.BlockSpec((rows, SMALL_W), lambda b, t: (b * nt + t, 0)),
            small((DN_CONV, QKV_W)), small((1, SMALL_W)), small((1, SMALL_W)), small((1, DN_DV)),
        ] + side_specs,
        out_specs=[
            pl.BlockSpec((rows, 1024), lambda b, t: (b * nt + t, 0)),
            pl.BlockSpec((1, DN_CONV - 1, QKV_W), lambda b, t: (b, 0, 0)),
            pl.BlockSpec((1, DN_HEADS, DN_DK, DN_DV), lambda b, t: (b, 0, 0, 0)),
        ] + side_specs,
        out_shape=[
            jax.ShapeDtypeStruct((nb * t_len, 1024), BF16),
            jax.ShapeDtypeStruct((nb, DN_CONV - 1, QKV_W), F32),
            jax.ShapeDtypeStruct((nb, DN_HEADS, DN_DK, DN_DV), F32),
        ] + [jax.ShapeDtypeStruct(arr.shape, BF16) for arr, _ in side],
        scratch_shapes=[pltpu.VMEM((QKV_W // LANES, DN_PAD + rows, LANES), F32),
                        pltpu.VMEM((DN_HEADS, DN_DK, DN_DV), F32)],
        compiler_params=_params(("arbitrary", "arbitrary")),
        name="delta_prompt",
    )(proj, proj, proj, proj, ba, wdc, arow, dtrow, gdn, *[arr for arr, _ in side])
    return outs[0], outs[1], outs[2], list(outs[3:])


DS_ROWS = 8
DS_GROUPS = 2


def _delta_sample_kernel(q_ref, k_ref, v_ref, z_ref, ba_ref, dst_ref, s0_ref, wdc_ref, arow_ref, dtrow_ref,
                         gdn_ref, o_ref, nd_ref, ns_ref, ext_ref, *, t_len):
    hist = DN_CONV - 1
    per = DS_ROWS // t_len
    n_slab = QKV_W // LANES
    ns_ref[...] = s0_ref[...]
    seqs = []
    n_seq = s0_ref.shape[0]
    for i in range(n_seq):
        g0 = (i // per) * DS_ROWS
        r0 = (i % per) * t_len
        ext_ref[i] = jnp.zeros((n_slab, DN_PAD + DS_ROWS, LANES), F32)
        seq = pl.program_id(0) * n_seq + i
        old_rows = [dst_ref[r, pl.ds(seq, 1), :] for r in range(hist)]
        for s in range(n_slab):
            sl = slice(s * LANES, (s + 1) * LANES)
            for r in range(hist):
                ext_ref[i, s, DN_PAD - hist + r:DN_PAD - hist + r + 1, :] = old_rows[r][:, sl]
            src = (q_ref, k_ref, v_ref)[s // (1024 // LANES)]
            col = (s % (1024 // LANES)) * LANES
            ext_ref[i, s, DN_PAD:DN_PAD + t_len, :] = src[g0 + r0:g0 + r0 + t_len, col:col + LANES]
        for r in range(hist):
            row = DN_PAD + t_len - hist + r
            nd_ref[r, pl.ds(seq, 1), :] = jnp.concatenate([ext_ref[i, s, row:row + 1, :] for s in range(n_slab)],
                                                          axis=1)
        z = z_ref[g0:g0 + DS_ROWS, :]
        ba = ba_ref[g0:g0 + DS_ROWS, :]
        if r0:
            z = pltpu.roll(z, DS_ROWS - r0, 0)
            ba = pltpu.roll(ba, DS_ROWS - r0, 0)

        def win(s, j0, j1, i=i):
            return ext_ref[i, s, j0:j1, :]

        def store_o(h, val, g0=g0, r0=r0):
            o_ref[g0 + r0:g0 + r0 + t_len, h * DN_DV:(h + 1) * DN_DV] = val[0:t_len].astype(o_ref.dtype)

        seqs.append([dict(win=win, z=z, ba=ba, s=ns_ref.at[i], store_o=store_o)])
    _delta_run(seqs, wdc_ref, arow_ref, dtrow_ref, gdn_ref, c=DS_ROWS, n_valid=t_len)


def _delta_sample(proj, ba, dstate, s0, wdc, arow, dtrow, gdn, nb, t_len):
    per = DS_ROWS // t_len
    groups = DS_GROUPS if nb % (per * DS_GROUPS) == 0 else 1
    bb = per * groups
    rows = DS_ROWS * groups
    hist = DN_CONV - 1
    row_spec = lambda col: pl.BlockSpec((rows, 1024), lambda i: (i, col))
    small = lambda shape: pl.BlockSpec(shape, lambda i: (0,) * len(shape))
    return pl.pallas_call(
        functools.partial(_delta_sample_kernel, t_len=t_len),
        grid=(nb // bb,),
        in_specs=[
            row_spec(COL_Q), row_spec(COL_K), row_spec(COL_V), row_spec(COL_Z),
            pl.BlockSpec((rows, SMALL_W), lambda i: (i, 0)),
            pl.BlockSpec((hist, nb, QKV_W), lambda i: (0, 0, 0)),
            pl.BlockSpec((bb, DN_HEADS, DN_DK, DN_DV), lambda i: (i, 0, 0, 0)),
            small((DN_CONV, QKV_W)), small((1, SMALL_W)), small((1, SMALL_W)), small((1, DN_DV)),
        ],
        out_specs=[
            pl.BlockSpec((rows, 1024), lambda i: (i, 0)),
            pl.BlockSpec((hist, nb, QKV_W), lambda i: (0, 0, 0)),
            pl.BlockSpec((bb, DN_HEADS, DN_DK, DN_DV), lambda i: (i, 0, 0, 0)),
        ],
        out_shape=[
            jax.ShapeDtypeStruct((nb * t_len, 1024), F32),
            jax.ShapeDtypeStruct((hist, nb, QKV_W), F32),
            jax.ShapeDtypeStruct((nb, DN_HEADS, DN_DK, DN_DV), F32),
        ],
        scratch_shapes=[pltpu.VMEM((bb, QKV_W // LANES, DN_PAD + DS_ROWS, LANES), F32)],
        compiler_params=_params(("arbitrary",)),
        name="delta_sample",
    )(proj, proj, proj, proj, ba, dstate, s0, wdc, arow, dtrow, gdn)


def _softmax_rows(s):
    m = jnp.max(s, axis=-1, keepdims=True)
    p = jnp.exp(s - m)
    return p / jnp.sum(p, axis=-1, keepdims=True)


def _attn_pairs(pairs):
    scores = [_dot_nt(q16, k.astype(BF16)) * (XA_DIM ** -0.5) for q16, k, _ in pairs]
    probs = [_softmax_rows(s).astype(BF16) for s in scores]
    return [_dot(p, v.astype(BF16)) for p, (_, _, v) in zip(probs, pairs)]


def _attn_prompt_kernel(xq_ref, k_ref, v_ref, o_ref):
    sls = [slice(h * XA_DIM, (h + 1) * XA_DIM) for h in range(XA_HEADS)]
    outs = _attn_pairs([(xq_ref[:, sl].astype(BF16), k_ref[:, sl], v_ref[:, sl]) for sl in sls])
    for sl, o in zip(sls, outs):
        o_ref[:, sl] = o.astype(o_ref.dtype)


def _attn_prompt(proj, mkv, nb, t_len, tq):
    nt = t_len // tq
    return pl.pallas_call(
        _attn_prompt_kernel,
        grid=(nb, nt),
        in_specs=[
            pl.BlockSpec((tq, XW), lambda b, t: (b * nt + t, COL_XQ)),
            pl.BlockSpec((N_MEM, XW), lambda b, t: (b, 0)),
            pl.BlockSpec((N_MEM, XW), lambda b, t: (b, 1)),
        ],
        out_specs=pl.BlockSpec((tq, XW), lambda b, t: (b * nt + t, 0)),
        out_shape=jax.ShapeDtypeStruct((nb * t_len, XW), BF16),
        compiler_params=_params(("parallel", "arbitrary")),
        name="attn_prompt",
    )(proj, mkv, mkv)


def _attn_sample_kernel(xq_ref, k_hbm, v_hbm, o_ref, kbuf, vbuf, sem, *, layer, bb, t_len):
    i = pl.program_id(0)
    slot = i % 2

    def copies(step, slot_):
        out = []
        for h in range(XA_HEADS):
            for kv, (hbm, buf) in enumerate(((k_hbm, kbuf), (v_hbm, vbuf))):
                out.append(pltpu.make_async_copy(hbm.at[layer, pl.ds(step * bb, bb), :, h, :], buf.at[slot_, h],
                                                 sem.at[kv, slot_, h]))
        return out

    @pl.when(i == 0)
    def _():
        for cp in copies(0, 0):
            cp.start()

    @pl.when(i + 1 < pl.num_programs(0))
    def _():
        for cp in copies(i + 1, 1 - slot):
            cp.start()

    for cp in copies(i, slot):
        cp.wait()

    per = DS_ROWS // t_len
    rows = lax.broadcasted_iota(jnp.int32, (DS_ROWS, XA_DIM), 0)
    pairs = []
    for b in range(bb):
        r0 = (b // per) * DS_ROWS
        for h in range(XA_HEADS):
            q16 = xq_ref[r0:r0 + DS_ROWS, h * XA_DIM:(h + 1) * XA_DIM].astype(BF16)
            pairs.append((q16, kbuf[slot, h, b], vbuf[slot, h, b]))
    outs = _attn_pairs(pairs)
    for grp in range(bb // per):
        r0 = grp * DS_ROWS
        for h in range(XA_HEADS):
            o = outs[(grp * per) * XA_HEADS + h]
            for s in range(1, per):
                o = jnp.where(rows < s * t_len, o, outs[(grp * per + s) * XA_HEADS + h])
            o_ref[r0:r0 + DS_ROWS, h * XA_DIM:(h + 1) * XA_DIM] = o.astype(o_ref.dtype)


def _attn_sample(proj, ck, cv, layer, nb, t_len, bb):
    rows = bb * t_len
    buf = pltpu.VMEM((2, XA_HEADS, bb, N_MEM, XA_DIM), F32)
    return pl.pallas_call(
        functools.partial(_attn_sample_kernel, layer=layer, bb=bb, t_len=t_len),
        grid=(nb // bb,),
        in_specs=[pl.BlockSpec((rows, XW), lambda i: (i, COL_XQ)), pl.BlockSpec(memory_space=pl.ANY),
                  pl.BlockSpec(memory_space=pl.ANY)],
        out_specs=pl.BlockSpec((rows, XW), lambda i: (i, 0)),
        out_shape=jax.ShapeDtypeStruct((nb * t_len, XW), F32),
        scratch_shapes=[buf, buf, pltpu.SemaphoreType.DMA((2, 2, XA_HEADS))],
        compiler_params=_params(("arbitrary",)),
        name="attn_sample",
    )(proj, ck, cv)


def _merge_others(d_ref, m_ref, gd_ref, gx_ref, wd_ref, wm_ref):
    mix = _sigmoid(gd_ref[...]) * _dot(d_ref[...].astype(BF16), wd_ref[...])
    return mix + _sigmoid(gx_ref[...]) * _dot(m_ref[...].astype(BF16), wm_ref[...])


def _merge_finish(mix, c16, gc_ref, x_ref, wc_ref, bc_ref, wo_ref, gpm_ref, h_ref):
    mix = mix + _sigmoid(gc_ref[...]) * (_dot(c16, wc_ref[...]) + bc_ref[...])
    out = _dot(mix.astype(BF16), wo_ref[...])
    h_ref[...] = x_ref[...] + _rms_rows(out, gpm_ref[...])


def _merge_kernel(c_ref, d_ref, m_ref, gc_ref, gd_ref, gx_ref, x_ref, wc_ref, bc_ref, wd_ref, wm_ref, wo_ref,
                  gpm_ref, h_ref):
    mix = _merge_others(d_ref, m_ref, gd_ref, gx_ref, wd_ref, wm_ref)
    _merge_finish(mix, c_ref[...].astype(BF16), gc_ref, x_ref, wc_ref, bc_ref, wo_ref, gpm_ref, h_ref)


def _merge_conv_kernel(ga_ref, gb_ref, wdw_ref, bdw_ref, lng_ref, lnb_ref, d_ref, m_ref, gc_ref, gd_ref, gx_ref,
                       x_ref, wc_ref, bc_ref, wd_ref, wm_ref, wo_ref, gpm_ref, h_ref, st_ref, ext_ref, conv_ref,
                       c_ref):
    t = pl.program_id(1)
    _conv_tile_start(ext_ref, t)
    mix = _merge_others(d_ref, m_ref, gd_ref, gx_ref, wd_ref, wm_ref)
    _conv_tile(ga_ref, gb_ref, wdw_ref, bdw_ref, lng_ref, lnb_ref, ext_ref, conv_ref, c_ref)
    _merge_finish(mix, c_ref[...], gc_ref, x_ref, wc_ref, bc_ref, wo_ref, gpm_ref, h_ref)
    _conv_tile_finish(st_ref, ext_ref, ga_ref.shape[0], t, pl.num_programs(1))


_MERGE_WEIGHT_SHAPES = ((C_CONV, D_MODEL), (1, D_MODEL), (1024, D_MODEL), (XW, D_MODEL), (D_MODEL, D_MODEL),
                        (1, D_MODEL))


def _merge(c, d, m, proj, x2d, wc, bc, wd, wm, wo, gpm, tm):
    rows = x2d.shape[0]
    row = lambda width, col=0: pl.BlockSpec((tm, width), lambda i: (i, col))
    return pl.pallas_call(
        _merge_kernel,
        grid=(rows // tm,),
        in_specs=[
            row(1024), row(1024), row(1024),
            row(D_MODEL, 0), row(D_MODEL, 1), row(D_MODEL, 2),
            row(D_MODEL),
        ] + [_const_spec(shape) for shape in _MERGE_WEIGHT_SHAPES],
        out_specs=row(D_MODEL),
        out_shape=jax.ShapeDtypeStruct((rows, D_MODEL), F32),
        compiler_params=_params(("parallel",)),
        name="merge",
    )(c, d, m, proj, proj, proj, x2d, wc, bc, wd, wm, wo, gpm)


def _merge_conv(d, m, proj, x2d, wdw, bdw, lng, lnb, wc, bc, wd, wm, wo, gpm, nb, t_len, tt):
    nt = t_len // tt
    row = lambda width, col=0: pl.BlockSpec((tt, width), lambda b, t: (b * nt + t, col))
    vec = lambda n: pl.BlockSpec((1, n), lambda b, t: (0, 0))
    return pl.pallas_call(
        _merge_conv_kernel,
        grid=(nb, nt),
        in_specs=[
            row(C_CONV, COL_GA), row(C_CONV, COL_GB),
            pl.BlockSpec((CONV_WIDTH, C_CONV), lambda b, t: (0, 0)), vec(C_CONV), vec(C_CONV), vec(C_CONV),
            row(1024), row(1024),
            row(D_MODEL, 0), row(D_MODEL, 1), row(D_MODEL, 2),
            row(D_MODEL),
        ] + [_const_spec(shape) for shape in _MERGE_WEIGHT_SHAPES],
        out_specs=[row(D_MODEL), pl.BlockSpec((1, CONV_WIDTH - 1, C_CONV), lambda b, t: (b, 0, 0))],
        out_shape=[
            jax.ShapeDtypeStruct((nb * t_len, D_MODEL), F32),
            jax.ShapeDtypeStruct((nb, CONV_WIDTH - 1, C_CONV), F32),
        ],
        scratch_shapes=[pltpu.VMEM((C_CONV // LANES, CONV_PAD + tt, LANES), F32), pltpu.VMEM((tt, C_CONV), F32),
                        pltpu.VMEM((tt, C_CONV), BF16)],
        compiler_params=pltpu.CompilerParams(dimension_semantics=("parallel", "arbitrary"),
                                             vmem_limit_bytes=MERGE_CONV_VMEM_LIMIT),
        name="merge_conv",
    )(proj, proj, wdw, bdw, lng, lnb, d, m, proj, proj, proj, x2d, wc, bc, wd, wm, wo, gpm)


FFN_TF = 512
FFN_TN = 512


def _ffn_kernel(h_ref, g1_ref, wg_ref, wu_ref, wd_ref, g2_ref, y_ref, fn_ref, act_ref, *, n_up):
    f = pl.program_id(1)

    @pl.when(f == 0)
    def _():
        _norm_rows_to(h_ref, g1_ref, fn_ref, 64)

    @pl.when(f < n_up)
    def _():
        fn = fn_ref[...]
        col = pl.multiple_of(f * FFN_TF, FFN_TF)
        act_ref[:, pl.ds(col, FFN_TF)] = (_silu(_dot(fn, wg_ref[...])) * _dot(fn, wu_ref[...])).astype(BF16)

    @pl.when(f >= n_up)
    def _():
        col = pl.multiple_of((f - n_up) * FFN_TN, FFN_TN)
        y_ref[:, pl.ds(col, FFN_TN)] = _dot(act_ref[...], wd_ref[...])

    @pl.when(f == pl.num_programs(1) - 1)
    def _():
        g2 = g2_ref[...]
        rows = 64

        def body(i, c):
            r = pl.multiple_of(i * rows, rows)
            y_ref[pl.ds(r, rows), :] = h_ref[pl.ds(r, rows), :] + _rms_rows(y_ref[pl.ds(r, rows), :], g2)
            return c

        lax.fori_loop(0, h_ref.shape[0] // rows, body, 0)


def _ffn(h2d, g1, wg, wu, wd, g2, tm):
    rows = h2d.shape[0]
    d_ff = wg.shape[1]
    n_up = d_ff // FFN_TF
    n_down = D_MODEL // FFN_TN
    return pl.pallas_call(
        functools.partial(_ffn_kernel, n_up=n_up),
        grid=(rows // tm, n_up + n_down),
        in_specs=[
            pl.BlockSpec((tm, D_MODEL), lambda i, f: (i, 0)),
            pl.BlockSpec((1, D_MODEL), lambda i, f: (0, 0)),
            pl.BlockSpec((D_MODEL, FFN_TF), lambda i, f: (0, jnp.minimum(f, n_up - 1))),
            pl.BlockSpec((D_MODEL, FFN_TF), lambda i, f: (0, jnp.minimum(f, n_up - 1))),
            pl.BlockSpec((d_ff, FFN_TN), lambda i, f: (0, jnp.maximum(f - n_up, 0))),
            pl.BlockSpec((1, D_MODEL), lambda i, f: (0, 0)),
        ],
        out_specs=pl.BlockSpec((tm, D_MODEL), lambda i, f: (i, 0)),
        out_shape=jax.ShapeDtypeStruct((rows, D_MODEL), F32),
        scratch_shapes=[pltpu.VMEM((tm, D_MODEL), BF16), pltpu.VMEM((tm, d_ff), BF16)],
        compiler_params=_params(("parallel", "arbitrary")),
        name="ffn",
    )(h2d, g1, wg, wu, wd, g2)


def _row(v):
    return v.reshape(1, -1).astype(F32)


def _layer_weights(w_in, w_dw, b_dw, ln_g, ln_b, w_p_conv, b_p_conv, w_dconv, a_log, dt_bias, g_dn_norm,
                   w_p_delta, w_p_mem, w_o, w_gate, w_up, w_down):
    lane_pad = (0, SMALL_W - 2 * DN_HEADS)
    arow = jnp.pad(jnp.concatenate([jnp.zeros((DN_HEADS,), F32), -jnp.exp(a_log.astype(F32))]), lane_pad)
    dtrow = jnp.pad(jnp.concatenate([jnp.zeros((DN_HEADS,), F32), dt_bias.astype(F32)]), lane_pad)
    return dict(
        w_t=jnp.swapaxes(w_in, 0, 1), w_dw=w_dw, b_dw=_row(b_dw), ln_g=_row(ln_g), ln_b=_row(ln_b),
        w_p_conv=w_p_conv, b_p_conv=_row(b_p_conv), w_dconv=w_dconv, arow=_row(arow),
        dtrow=_row(dtrow), g_dn=_row(g_dn_norm), w_p_delta=w_p_delta, w_p_mem=w_p_mem, w_o=w_o, w_gate=w_gate,
        w_up=w_up, w_down=w_down)


def _tile(n, pref):
    return pref if n % pref == 0 else n


def kernel(x_prompt, x_sample, mem_prompt, cache_mem_k, cache_mem_v, state_conv, state_delta_conv, state_delta, g_pre_mix, w_in, w_dw, b_dw, ln_g, ln_b, w_p_conv, b_p_conv, w_dconv, a_log, dt_bias, g_dn_norm, w_p_delta, g_mem, w_mem_kv, w_p_mem, w_o, g_post_mix, g_pre_ffn, w_gate, w_up, w_down, g_post_ffn):
    depth = w_in.shape[0]
    assert depth == 1, "single-layer stack"
    l = 0
    pb, pt, _ = x_prompt.shape
    sb, st, _ = x_sample.shape
    assert pt % DN_CHUNK == 0 and st <= DS_ROWS and DS_ROWS % st == 0

    lw = _layer_weights(w_in[l], w_dw[l], b_dw[l], ln_g[l], ln_b[l], w_p_conv[l], b_p_conv[l], w_dconv[l],
                        a_log[l], dt_bias[l], g_dn_norm[l], w_p_delta[l], w_p_mem[l], w_o[l], w_gate[l],
                        w_up[l], w_down[l])
    gpre, gpm, gff1, gff2 = _row(g_pre_mix[l]), _row(g_post_mix[l]), _row(g_pre_ffn[l]), _row(g_post_ffn[l])

    def ffn(h):
        return _ffn(h, gff1, *ffn_w, w_down16, gff2, _tile(h.shape[0], 512))

    xp = x_prompt.reshape(pb * pt, D_MODEL)
    mkv = _norm_proj(mem_prompt.reshape(pb * N_MEM, D_MODEL), _row(g_mem[l]), w_mem_kv[l].astype(BF16),
                     _tile(pb * N_MEM, 512), 1024)
    proj_p, ba_p, ffn_w = _in_proj(xp, gpre, lw["w_t"], _tile(pb * pt, 1024), side=(lw["w_gate"], lw["w_up"]))
    d_p, dconv_p, s_p, (w_down16, w_o16, w_pc16, w_pd16, w_pm16) = _delta_prompt(
        proj_p, ba_p, lw["w_dconv"], lw["arow"], lw["dtrow"], lw["g_dn"], pb, pt,
        side=((lw["w_down"], 32), (lw["w_o"], 8), (lw["w_p_conv"], 8), (lw["w_p_delta"], 8), (lw["w_p_mem"], 8)))
    merge_w = (w_pc16, lw["b_p_conv"], w_pd16, w_pm16, w_o16, gpm)
    m_p = _attn_prompt(proj_p, mkv, pb, pt, _tile(pt, 512))
    h_p, conv_p = _merge_conv(d_p, m_p, proj_p, xp, lw["w_dw"], lw["b_dw"], lw["ln_g"], lw["ln_b"], *merge_w, pb, pt,
                              _tile(pt, 256))
    y_p = ffn(h_p).reshape(pb, pt, D_MODEL)
    mk = mkv[:, 0:XW].reshape(pb, N_MEM, XA_HEADS, XA_DIM)
    mv = mkv[:, XW:2 * XW].reshape(pb, N_MEM, XA_HEADS, XA_DIM)

    xs = x_sample.reshape(sb * st, D_MODEL)
    proj_s, ba_s, _ = _in_proj(xs, gpre, lw["w_t"], _tile(sb * st, 512))
    c_s, conv_s_t = _conv_sample(proj_s, jnp.swapaxes(state_conv[l], 0, 1), lw["w_dw"], lw["b_dw"], lw["ln_g"],
                                 lw["ln_b"], sb, st, _tile(sb, 8))
    conv_s = jnp.swapaxes(conv_s_t, 0, 1)
    d_s, dconv_s_t, s_s = _delta_sample(proj_s, ba_s, jnp.swapaxes(state_delta_conv[l], 0, 1), state_delta[l],
                                        lw["w_dconv"], lw["arow"], lw["dtrow"], lw["g_dn"], sb, st)
    dconv_s = jnp.swapaxes(dconv_s_t, 0, 1)
    m_s = _attn_sample(proj_s, cache_mem_k, cache_mem_v, l, sb, st, 4)
    h_s = _merge(c_s, d_s, m_s, proj_s, xs, *merge_w, _tile(sb * st, 256))
    y_s = ffn(h_s).reshape(sb, st, D_MODEL)

    return (y_p, y_s, mk[None], mv[None], conv_p[None], dconv_p[None], s_p[None],
            conv_s[None], dconv_s[None], s_s[None])
```
